```python
import math
import jax, jax.numpy as jnp
from jax import lax
import numpy as np

D_MODEL = 1024
BATCH = 2
SEQ = 8192
DEPTH = 1
DEC_BATCH = 128
DEC_SEQ = 4
PAST_LEN = 2048
PAGE_SIZE = 128

MIX_WIDTH = D_MODEL
M_HEADS = 4
M_HEAD_DIM = MIX_WIDTH // 2 // M_HEADS
M_WIDTH = M_HEADS * M_HEAD_DIM
M_CHUNK = 64
A_HEADS = 8
A_HEAD_DIM = (MIX_WIDTH - M_WIDTH) // A_HEADS
A_WIDTH = A_HEADS * A_HEAD_DIM
IDX_HEADS = 8
IDX_DIM = 64
TOPK_MAX = 256
Q_BLOCK = 128
ROPE_THETA = 500000.0
ROT_FRACTION = 4
D_FF = -(-8 * D_MODEL // (3 * 256)) * 256
ALPHA = (2 * DEPTH) ** 0.25
BETA = (8 * DEPTH) ** -0.25
LN_EPS = 1e-5
IN_SPLITS = (M_WIDTH, M_WIDTH, M_WIDTH, M_WIDTH, M_HEADS, M_HEADS,
             A_WIDTH, A_WIDTH, A_WIDTH,
             IDX_HEADS * IDX_DIM, IDX_DIM, IDX_HEADS)
IN_DIM = sum(IN_SPLITS)

kernel_name = "hymba_mlstm_dsa_deepnorm_step"


def layer_norm(x, g, b):
    xf = x.astype(jnp.float32)
    mu = xf.mean(-1, keepdims=True)
    var = jnp.square(xf - mu).mean(-1, keepdims=True)
    return ((xf - mu) * lax.rsqrt(var + LN_EPS) * g.astype(jnp.float32) + b.astype(jnp.float32)).astype(x.dtype)


def partial_rope(x, pos):
    rot = x.shape[-1] // ROT_FRACTION
    half = rot // 2
    inv = ROPE_THETA ** (-jnp.arange(half, dtype=jnp.float32) / half)
    ang = pos.astype(jnp.float32)[:, None] * inv[None, :]
    cos = jnp.cos(ang)[:, None, :]
    sin = jnp.sin(ang)[:, None, :]
    x1 = x[..., :half]
    x2 = x[..., half:rot]
    return jnp.concatenate([x1 * cos - x2 * sin, x2 * cos + x1 * sin, x[..., rot:]], axis=-1)


def mixer_streams(x, pos, w_in, b_gate):
    B, T, _ = x.shape
    z = jnp.einsum('btd,de->bte', x, w_in).astype(jnp.float32)
    split_points = np.cumsum(IN_SPLITS)[:-1].tolist()
    mq, mk, mv, mo, mi, mf, aq, ak, av, iq, ik, iw = jnp.split(z, split_points, axis=-1)
    heads = lambda a, h: a.reshape(B, T, h, -1)
    bg = b_gate.astype(jnp.float32)
    i_pre = mi + bg[:M_HEADS]
    log_f = jax.nn.log_sigmoid(mf + bg[M_HEADS:])
    mlstm = (heads(mq, M_HEADS), heads(mk, M_HEADS) * (M_HEAD_DIM ** -0.5), heads(mv, M_HEADS), mo, i_pre, log_f)
    q = partial_rope(heads(aq, A_HEADS), pos)
    k = partial_rope(heads(ak, A_HEADS), pos)
    v = heads(av, A_HEADS)
    qi = partial_rope(heads(iq, IDX_HEADS), pos) * (IDX_DIM ** -0.5)
    ki = partial_rope(ik[:, :, None, :], pos)[:, :, 0, :]
    wi = iw * (IDX_HEADS ** -0.5)
    return mlstm, (q, k, v, qi, ki, wi)


def mlstm_mixer(streams, C0, n0, m0):
    q, k, v, o_pre, i_pre, log_f = streams
    B, T, H, Dh = q.shape
    c = math.gcd(T, M_CHUNK)
    nc = T // c

    def to_chunks(a):
        a = a.reshape((B, nc, c) + a.shape[2:])
        return jnp.moveaxis(jnp.moveaxis(a, 1, 0), 3, 2)

    causal = jnp.tril(jnp.ones((c, c), dtype=bool))

    def step(carry, inp):
        C, n, m = carry
        qc, kc, vc, ic, fc = inp
        F = jnp.cumsum(fc, axis=-1)
        Dlog = jnp.where(causal, F[..., :, None] - F[..., None, :] + ic[..., None, :], -jnp.inf)
        inter = m[..., None] + F
        m_t = jnp.maximum(Dlog.max(-1), inter)
        W = jnp.exp(Dlog - m_t[..., None])
        g = jnp.exp(inter - m_t)
        S = jnp.einsum('bhtd,bhsd->bhts', qc, kc) * W
        num = jnp.einsum('bhts,bhse->bhte', S, vc) + g[..., None] * jnp.einsum('bhtd,bhde->bhte', qc, C)
        den = S.sum(-1) + g * jnp.einsum('bhtd,bhd->bht', qc, n)
        h = num / jnp.maximum(jnp.abs(den), jnp.exp(-m_t))[..., None]
        FL = F[..., -1]
        a = FL[..., None] - F + ic
        m_new = jnp.maximum(m + FL, a.max(-1))
        wk = jnp.exp(a - m_new[..., None])
        gC = jnp.exp(m + FL - m_new)
        C_new = gC[..., None, None] * C + jnp.einsum('bhs,bhsd,bhse->bhde', wk, kc, vc)
        n_new = gC[..., None] * n + jnp.einsum('bhs,bhsd->bhd', wk, kc)
        return (C_new, n_new, m_new), h

    init = (C0.astype(jnp.float32), n0.astype(jnp.float32), m0.astype(jnp.float32))
    (C, n, m), h = lax.scan(step, init, (to_chunks(q), to_chunks(k), to_chunks(v), to_chunks(i_pre), to_chunks(log_f)))
    h = jnp.moveaxis(jnp.moveaxis(h, 2, 3), 0, 1).reshape(B, T, H * Dh)
    return h * jax.nn.sigmoid(o_pre), (C, n, m)


def index_scores(qi, wi, ki):
    r = jax.nn.relu(jnp.einsum('bthd,bsd->bths', qi, ki.astype(jnp.float32)))
    return jnp.einsum('bths,bth->bts', r, wi)


def sparse_attend(q, k_sel, v_sel, valid):
    s = jnp.einsum('bthd,btkhd->bthk', q, k_sel.astype(jnp.float32)) * (q.shape[-1] ** -0.5)
    p = jax.nn.softmax(jnp.where(valid[:, :, None, :], s, -jnp.inf), axis=-1)
    return jnp.einsum('bthk,btkhd->bthd', p, v_sel.astype(jnp.float32))


def dsa_prompt(q, k, v, qi, ki, wi):
    B, S, H, D = q.shape
    topk = min(TOPK_MAX, S // 4)
    nb = S // Q_BLOCK
    pos = jnp.arange(S, dtype=jnp.int32)
    blocks = lambda a: jnp.moveaxis(a.reshape((B, nb, Q_BLOCK) + a.shape[2:]), 1, 0)
    gather = jax.vmap(lambda rows, idx: rows[idx])

    def one_block(inp):
        qb, qib, wib, pb = inp
        sc = index_scores(qib, wib, ki)
        sc = jnp.where(pos[None, None, :] <= pb[None, :, None], sc, -jnp.inf)
        _, idx = lax.top_k(sc, topk)
        valid = idx <= pb[None, :, None]
        return sparse_attend(qb, gather(k, idx), gather(v, idx), valid)

    out = lax.map(one_block, (blocks(q), blocks(qi), blocks(wi), pos.reshape(nb, Q_BLOCK)))
    return jnp.moveaxis(out, 0, 1).reshape(B, S, H * D)


def dsa_sample(q, k_new, v_new, qi, ki_new, wi, cache_k, cache_v, cache_kidx, page_table):
    Bd, T, H, D = q.shape
    past = page_table.shape[1] * PAGE_SIZE
    L = past + T
    topk = min(TOPK_MAX, L // 4)
    ki_past = cache_kidx[page_table].reshape(Bd, past, -1)
    ki_all = jnp.concatenate([ki_past.astype(jnp.float32), ki_new], axis=1)
    pos_q = past + jnp.arange(T, dtype=jnp.int32)
    sc = index_scores(qi, wi, ki_all)
    sc = jnp.where(jnp.arange(L, dtype=jnp.int32)[None, None, :] <= pos_q[None, :, None], sc, -jnp.inf)
    _, idx = lax.top_k(sc, topk)
    valid = idx <= pos_q[None, :, None]
    in_past = (idx < past)[..., None, None]
    bidx = jnp.arange(Bd)[:, None, None]
    ip = jnp.minimum(idx, past - 1)
    phys = page_table[bidx, ip // PAGE_SIZE]
    off = ip % PAGE_SIZE
    inew = jnp.clip(idx - past, 0, T - 1)
    k_sel = jnp.where(in_past, cache_k[phys, off].astype(jnp.float32), k_new[bidx, inew])
    v_sel = jnp.where(in_past, cache_v[phys, off].astype(jnp.float32), v_new[bidx, inew])
    return sparse_attend(q, k_sel, v_sel, valid).reshape(Bd, T, H * D)


def block_output(x, h_m, h_a, w_out, ln1_g, ln1_b, w_gate, w_up, w_down, ln2_g, ln2_b):
    mix = jnp.concatenate([h_m, h_a], axis=-1).astype(x.dtype)
    x = layer_norm(ALPHA * x + mix @ w_out, ln1_g, ln1_b)
    ff = (jax.nn.silu(x @ w_gate) * (x @ w_up)) @ w_down
    return layer_norm(ALPHA * x + ff, ln2_g, ln2_b)


def setup_inputs(seed: int = 0) -> dict:
    key = jax.random.key(seed)
    ks = jax.random.split(key, 24)
    n_pages = PAST_LEN // PAGE_SIZE
    n_used = DEC_BATCH * n_pages
    n_pool = n_used + n_used // 4
    nrm = lambda k, shape, scale: jax.random.normal(k, shape, jnp.float32) * scale
    f_bias = jnp.linspace(3.0, 6.0, M_HEADS, dtype=jnp.float32)
    b_gate = jnp.concatenate([nrm(ks[10], (DEPTH, M_HEADS), 0.1),
                              f_bias[None, :] + nrm(ks[11], (DEPTH, M_HEADS), 0.1)], axis=-1)
    return {
        "x_prompt": nrm(ks[0], (BATCH, SEQ, D_MODEL), 1.0),
        "x_sample": nrm(ks[1], (DEC_BATCH, DEC_SEQ, D_MODEL), 1.0),
        "cache_k": nrm(ks[2], (DEPTH, n_pool, PAGE_SIZE, A_HEADS, A_HEAD_DIM), 1.0),
        "cache_v": nrm(ks[3], (DEPTH, n_pool, PAGE_SIZE, A_HEADS, A_HEAD_DIM), 1.0),
        "cache_kidx": nrm(ks[4], (DEPTH, n_pool, PAGE_SIZE, IDX_DIM), 1.0),
        "state_C": nrm(ks[5], (DEPTH, DEC_BATCH, M_HEADS, M_HEAD_DIM, M_HEAD_DIM), 0.1),
        "state_n": nrm(ks[6], (DEPTH, DEC_BATCH, M_HEADS, M_HEAD_DIM), 0.1),
        "state_m": nrm(ks[7], (DEPTH, DEC_BATCH, M_HEADS), 1.0),
        "page_table": jax.random.permutation(ks[8], n_pool)[:n_used].reshape(DEC_BATCH, n_pages).astype(jnp.int32),
        "w_in": nrm(ks[9], (DEPTH, D_MODEL, IN_DIM), D_MODEL ** -0.5),
        "b_gate": b_gate,
        "w_out": nrm(ks[12], (DEPTH, MIX_WIDTH, D_MODEL), MIX_WIDTH ** -0.5 * BETA),
        "ln1_g": 1.0 + nrm(ks[13], (DEPTH, D_MODEL), 0.05),
        "ln1_b": nrm(ks[14], (DEPTH, D_MODEL), 0.02),
        "w_gate": nrm(ks[15], (DEPTH, D_MODEL, D_FF), D_MODEL ** -0.5),
        "w_up": nrm(ks[16], (DEPTH, D_MODEL, D_FF), D_MODEL ** -0.5),
        "w_down": nrm(ks[17], (DEPTH, D_FF, D_MODEL), D_FF ** -0.5 * BETA),
        "ln2_g": 1.0 + nrm(ks[18], (DEPTH, D_MODEL), 0.05),
        "ln2_b": nrm(ks[19], (DEPTH, D_MODEL), 0.02),
    }


def reference(x_prompt, x_sample, cache_k, cache_v, cache_kidx, state_C, state_n, state_m, page_table,
              w_in, b_gate, w_out, ln1_g, ln1_b, w_gate, w_up, w_down, ln2_g, ln2_b):
    Bp, Sp, _ = x_prompt.shape
    Bs, Ts, _ = x_sample.shape
    past = page_table.shape[1] * PAGE_SIZE
    pos_p = jnp.arange(Sp, dtype=jnp.int32)
    pos_s = past + jnp.arange(Ts, dtype=jnp.int32)
    xp, xs = x_prompt, x_sample
    kp_l, vp_l, kip_l, Cp_l, np_l, mp_l = [], [], [], [], [], []
    ks_l, vs_l, kis_l, Cs_l, ns_l, ms_l = [], [], [], [], [], []
    for l in range(DEPTH):
        ffn = (ln1_g[l], ln1_b[l], w_gate[l], w_up[l], w_down[l], ln2_g[l], ln2_b[l])
        m_str, a_str = mixer_streams(xp, pos_p, w_in[l], b_gate[l])
        h_m, (Cp, n_p, m_p) = mlstm_mixer(m_str, jnp.zeros((Bp, M_HEADS, M_HEAD_DIM, M_HEAD_DIM), jnp.float32),
                                          jnp.zeros((Bp, M_HEADS, M_HEAD_DIM), jnp.float32),
                                          jnp.zeros((Bp, M_HEADS), jnp.float32))
        q, k, v, qi, ki, wi = a_str
        h_a = dsa_prompt(q, k, v, qi, ki, wi)
        xp = block_output(xp, h_m, h_a, w_out[l], *ffn)
        kp_l.append(k); vp_l.append(v); kip_l.append(ki); Cp_l.append(Cp); np_l.append(n_p); mp_l.append(m_p)
        m_str, a_str = mixer_streams(xs, pos_s, w_in[l], b_gate[l])
        h_m, (Cs, n_s, m_s) = mlstm_mixer(m_str, state_C[l], state_n[l], state_m[l])
        q, k, v, qi, ki, wi = a_str
        h_a = dsa_sample(q, k, v, qi, ki, wi, cache_k[l], cache_v[l], cache_kidx[l], page_table)
        xs = block_output(xs, h_m, h_a, w_out[l], *ffn)
        ks_l.append(k); vs_l.append(v); kis_l.append(ki); Cs_l.append(Cs); ns_l.append(n_s); ms_l.append(m_s)
    k_prompt = jnp.stack(kp_l); v_prompt = jnp.stack(vp_l); kidx_prompt = jnp.stack(kip_l)
    C_prompt = jnp.stack(Cp_l); n_prompt = jnp.stack(np_l); m_prompt = jnp.stack(mp_l)
    k_sample = jnp.stack(ks_l); v_sample = jnp.stack(vs_l); kidx_sample = jnp.stack(kis_l)
    C_sample = jnp.stack(Cs_l); n_sample = jnp.stack(ns_l); m_sample = jnp.stack(ms_l)
    return (xp, xs, k_prompt, v_prompt, kidx_prompt, C_prompt, n_prompt, m_prompt,
            k_sample, v_sample, kidx_sample, C_sample, n_sample, m_sample)
```

```python
import functools
import math

import jax
import jax.numpy as jnp
from jax import lax
from jax.experimental import pallas as pl
from jax.experimental.pallas import tpu as pltpu

D_MODEL = 1024
M_HEADS = 4
M_HEAD_DIM = 128
M_WIDTH = 512
A_HEADS = 8
A_HEAD_DIM = 64
A_WIDTH = 512
IDX_HEADS = 8
IDX_DIM = 64
TOPK_MAX = 256
PAGE_SIZE = 128
ROPE_THETA = 500000.0
ROT = A_HEAD_DIM // 4
HALF = ROT // 2
D_FF = 2816
DEPTH = 1
ALPHA = (2 * DEPTH) ** 0.25
LN_EPS = 1e-5

LANES = 128
SUBLANES = 8
VMEM_LIMIT = 56 * 1024 * 1024

NEG_INF = float("-inf")
BF16 = jnp.bfloat16
F32 = jnp.float32

C_MQ, C_MK, C_MV, C_MO = 0, 512, 1024, 1536
C_AQ, C_AK, C_AV, C_IQ = 2048, 2560, 3072, 3584
C_SM = 4096
N_PERM = 4224
L_WI, L_IG, L_FG = 64, 72, 76


def _cparams(sem):
    return pltpu.CompilerParams(dimension_semantics=sem, vmem_limit_bytes=VMEM_LIMIT)


def _dot(a, b):
    return jnp.dot(a, b, preferred_element_type=F32)


def _dot_nt(a, b):
    return lax.dot_general(a, b, (((1,), (1,)), ((), ())), preferred_element_type=F32)


def _dot_tn(a, b):
    return lax.dot_general(a, b, (((0,), (0,)), ((), ())), preferred_element_type=F32)


def _rope128(x, cos, sa, sb):
    return x * cos + pltpu.roll(x, LANES - HALF, 1) * sa + pltpu.roll(x, HALF, 1) * sb


def _proj_kernel(x_ref, w_ref, wgt_ref, bias_ref, biast_ref, cos_ref, sa_ref, sb_ref,
                 mqkv_ref, so_ref, gates_ref, gatest_ref, qpad_ref, k_ref, kb_ref, v_ref, vb_ref,
                 qipad_ref, ki_ref, ki2_ref):
    xb = x_ref[...].astype(BF16)
    cos, sa, sb = cos_ref[...], sa_ref[...], sb_ref[...]
    lane = lax.broadcasted_iota(jnp.int32, (1, LANES), 1)
    lo_half = lane < A_HEAD_DIM

    zm = _dot(xb, w_ref[:, C_MQ:C_MO])
    mqkv_ref[:, 0:M_WIDTH] = zm[:, 0:M_WIDTH].astype(BF16)
    mqkv_ref[:, M_WIDTH:2 * M_WIDTH] = (zm[:, M_WIDTH:2 * M_WIDTH] * (M_HEAD_DIM ** -0.5)).astype(BF16)
    mqkv_ref[:, 2 * M_WIDTH:3 * M_WIDTH] = zm[:, 2 * M_WIDTH:3 * M_WIDTH].astype(BF16)
    so_ref[...] = jax.nn.sigmoid(_dot(xb, w_ref[:, C_MO:C_AQ]))

    def padded_heads(z, scale, out_ref):
        for p in range(A_WIDTH // LANES):
            r = _rope128(z[:, p * LANES:(p + 1) * LANES], cos, sa, sb) * scale
            out_ref[:, (2 * p) * LANES:(2 * p + 1) * LANES] = jnp.where(lo_half, r, 0.0).astype(BF16)
            out_ref[:, (2 * p + 1) * LANES:(2 * p + 2) * LANES] = jnp.where(lo_half, 0.0, r).astype(BF16)

    padded_heads(_dot(xb, w_ref[:, C_AQ:C_AK]), A_HEAD_DIM ** -0.5, qpad_ref)
    padded_heads(_dot(xb, w_ref[:, C_IQ:C_SM]), IDX_DIM ** -0.5, qipad_ref)

    zk = _dot(xb, w_ref[:, C_AK:C_AV])
    for p in range(A_WIDTH // LANES):
        r = _rope128(zk[:, p * LANES:(p + 1) * LANES], cos, sa, sb)
        k_ref[:, p * LANES:(p + 1) * LANES] = r
        kb_ref[:, p * LANES:(p + 1) * LANES] = r.astype(BF16)
    zv = _dot(xb, w_ref[:, C_AV:C_IQ])
    v_ref[...] = zv
    vb_ref[...] = zv.astype(BF16)

    zs = _dot(xb, w_ref[:, C_SM:N_PERM]) + bias_ref[...]
    rs = _rope128(zs, jnp.where(lo_half, cos, 1.0), jnp.where(lo_half, sa, 0.0), jnp.where(lo_half, sb, 0.0))
    is_wi = (lane >= L_WI) & (lane < L_IG)
    is_fg = (lane >= L_FG) & (lane < L_FG + M_HEADS)
    g = jnp.where(is_wi, rs * (IDX_HEADS ** -0.5), rs)
    g = jnp.where(is_fg, jax.nn.log_sigmoid(rs), g)
    gates_ref[...] = g
    ki = jnp.where(lo_half, rs, 0.0)
    ki_ref[...] = rs[:, 0:IDX_DIM]
    ki2_ref[...] = (ki + pltpu.roll(ki, A_HEAD_DIM, 1)).astype(BF16)

    zt = _dot_nt(wgt_ref[...], xb) + biast_ref[...]
    row = lax.broadcasted_iota(jnp.int32, (2 * M_HEADS, 1), 0)
    gatest_ref[...] = jnp.where(row >= M_HEADS, jax.nn.log_sigmoid(zt), zt)


def _rope_tables(pos):
    inv = ROPE_THETA ** (-jnp.arange(HALF, dtype=F32) / HALF)
    ang = pos.astype(F32)[:, None] * inv[None, :]
    c, s = jnp.cos(ang), jnp.sin(ang)
    n = pos.shape[0]
    one = jnp.ones((n, A_HEAD_DIM - ROT), F32)
    zero = jnp.zeros((n, A_HEAD_DIM - ROT), F32)
    z8 = jnp.zeros((n, HALF), F32)
    cos64 = jnp.concatenate([c, c, one], axis=1)
    sa64 = jnp.concatenate([-s, z8, zero], axis=1)
    sb64 = jnp.concatenate([z8, s, zero], axis=1)
    two = lambda t: jnp.concatenate([t, t], axis=1)
    return two(cos64), two(sa64), two(sb64)


def _proj(x, wp, wgt, bias, biast, tabs, tm, tab_blocks):
    m = x.shape[0]
    grid = (m // tm,)
    row = lambda w: pl.BlockSpec((tm, w), lambda i: (i, 0))
    full = lambda a: pl.BlockSpec(a.shape, lambda i: (0, 0))
    tab = pl.BlockSpec((tm, LANES), lambda i: (i % tab_blocks, 0))
    outs = [
        ((m, 3 * M_WIDTH), BF16, row(3 * M_WIDTH)),
        ((m, M_WIDTH), F32, row(M_WIDTH)),
        ((m, LANES), F32, row(LANES)),
        ((2 * M_HEADS, m), F32, pl.BlockSpec((2 * M_HEADS, tm), lambda i: (0, i))),
        ((m, 2 * A_WIDTH), BF16, row(2 * A_WIDTH)),
        ((m, A_WIDTH), F32, row(A_WIDTH)),
        ((m, A_WIDTH), BF16, row(A_WIDTH)),
        ((m, A_WIDTH), F32, row(A_WIDTH)),
        ((m, A_WIDTH), BF16, row(A_WIDTH)),
        ((m, 2 * A_WIDTH), BF16, row(2 * A_WIDTH)),
        ((m, IDX_DIM), F32, row(IDX_DIM)),
        ((m, LANES), BF16, row(LANES)),
    ]
    return pl.pallas_call(
        _proj_kernel,
        grid=grid,
        in_specs=[row(D_MODEL), full(wp), full(wgt), full(bias), full(biast), tab, tab, tab],
        out_specs=[o[2] for o in outs],
        out_shape=[jax.ShapeDtypeStruct(o[0], o[1]) for o in outs],
        compiler_params=_cparams(("parallel",)),
        name="proj",
    )(x, wp, wgt, bias, biast, *tabs)


def _prep_w_in(w_in, b_gate):
    s = [0, 512, 1024, 1536, 2048, 2052, 2056, 2568, 3080, 3592, 4104, 4168, 4176]
    mq, mk, mv, mo, mi, mf, aq, ak, av, iq, ik, iw = [w_in[:, s[i]:s[i + 1]] for i in range(12)]
    pad = jnp.zeros((D_MODEL, N_PERM - C_SM - 80), w_in.dtype)
    wp = jnp.concatenate([mq, mk, mv, mo, aq, ak, av, iq, ik, iw, mi, mf, pad], axis=1).astype(BF16)
    wgt = jnp.concatenate([mi, mf], axis=1).T.astype(BF16)
    bg = b_gate.astype(F32)
    bias = jnp.zeros((1, LANES), F32).at[0, L_IG:L_IG + 2 * M_HEADS].set(bg)
    return wp, wgt, bias, bg[:, None]


FF_CHUNK = D_FF // 2


def _layer_norm(x, g, b):
    mu = jnp.mean(x, axis=-1, keepdims=True)
    xc = x - mu
    var = jnp.mean(xc * xc, axis=-1, keepdims=True)
    return xc * lax.rsqrt(var + LN_EPS) * g + b


def _block_out_kernel(x_ref, hm_ref, ha_ref, wo_ref, g1_ref, b1_ref, wg_ref, wu_ref, wd_ref, g2_ref, b2_ref, y_ref):
    mix = _dot(hm_ref[...], wo_ref[0:M_WIDTH, :]) + _dot(ha_ref[...], wo_ref[M_WIDTH:2 * M_WIDTH, :])
    x1 = _layer_norm(ALPHA * x_ref[...] + mix, g1_ref[...], b1_ref[...])
    xb = x1.astype(BF16)
    ff = jnp.zeros_like(x1)
    for c in range(D_FF // FF_CHUNK):
        sl = slice(c * FF_CHUNK, (c + 1) * FF_CHUNK)
        act = jax.nn.silu(_dot(xb, wg_ref[:, sl])) * _dot(xb, wu_ref[:, sl])
        ff = ff + _dot(act.astype(BF16), wd_ref[sl, :])
    y_ref[...] = _layer_norm(ALPHA * x1 + ff, g2_ref[...], b2_ref[...])


def _block_out(x, hm, ha, wo, g1, b1, wg, wu, wd, g2, b2, tm):
    m = x.shape[0]
    row = lambda w: pl.BlockSpec((tm, w), lambda i: (i, 0))
    full = lambda a: pl.BlockSpec(a.shape, lambda i: (0, 0), pipeline_mode=pl.Buffered(1))
    return pl.pallas_call(
        _block_out_kernel,
        grid=(m // tm,),
        in_specs=[row(D_MODEL), row(M_WIDTH), row(A_WIDTH), full(wo), full(g1), full(b1),
                  full(wg), full(wu), full(wd), full(g2), full(b2)],
        out_specs=row(D_MODEL),
        out_shape=jax.ShapeDtypeStruct((m, D_MODEL), F32),
        compiler_params=_cparams(("parallel",)),
        name="block_out",
    )(x, hm, ha, wo, g1, b1, wg, wu, wd, g2, b2)


def _split3(x):
    h1 = x.astype(BF16)
    r1 = x - h1.astype(F32)
    h2 = r1.astype(BF16)
    h3 = (r1 - h2.astype(F32)).astype(BF16)
    return h1, h2, h3


def _dot3(x, w):
    h1, h2, h3 = _split3(x)
    return _dot(h1, w) + _dot(h2, w) + _dot(h3, w)


def _mlstm_chunk(q, k, v, lf_row, i_row, i_col, c_aug, m_prev):
    L = q.shape[0]
    t_idx = lax.broadcasted_iota(jnp.int32, (L, L), 0)
    s_idx = lax.broadcasted_iota(jnp.int32, (L, L), 1)
    causal = s_idx <= t_idx
    ones_b = jnp.ones((L, LANES), BF16)
    G = _dot3(jnp.where(causal, lf_row, 0.0), ones_b)
    tri_u = jnp.where(t_idx <= s_idx, 1.0, 0.0).astype(BF16)
    f_row = _dot3(jnp.broadcast_to(lf_row, (SUBLANES, L)), tri_u)[0:1, :]
    f_col = G[:, 0:1]
    dlog = jnp.where(causal, f_col - f_row + i_row, NEG_INF)
    inter = m_prev + f_col
    m_t = jnp.maximum(jnp.max(dlog, axis=1, keepdims=True), inter)
    w = jnp.exp(dlog - m_t)
    g = jnp.exp(inter - m_t)
    s = _dot_nt(q, k) * w
    lane = lax.broadcasted_iota(jnp.int32, (L, LANES), 1)
    v_aug = jnp.concatenate([v, jnp.where(lane == 0, 1.0, 0.0).astype(BF16)], axis=1)
    nd = _dot(s.astype(BF16), v_aug) + g * _dot(q, c_aug.astype(BF16))
    den = nd[:, M_HEAD_DIM:M_HEAD_DIM + 1]
    h = nd[:, 0:M_HEAD_DIM] / jnp.maximum(jnp.abs(den), jnp.exp(-m_t))
    m_new = m_t[L - 1:L, :]
    g_c = g[L - 1:L, :]
    wk = jnp.exp(f_col[L - 1:L, :] - f_col + i_col - m_new)
    kw = (k.astype(F32) * wk).astype(BF16)
    c_new = g_c * c_aug + _dot_tn(kw, v_aug)
    return h, c_new, m_new


def _mlstm_prompt_kernel(*refs, nb):
    mqkv_ref, so_ref, g_ref = refs[0:3]
    gt_refs = refs[3:3 + nb]
    hm_ref, c_ref, m_ref = refs[3 + nb:]
    step = pl.program_id(0)

    @pl.when(step == 0)
    def _():
        c_ref[...] = jnp.zeros_like(c_ref)
        m_ref[...] = jnp.zeros_like(m_ref)

    for b in range(nb):
        for h in range(M_HEADS):
            hs = slice(h * M_HEAD_DIM, (h + 1) * M_HEAD_DIM)
            q = mqkv_ref[b, :, hs]
            k = mqkv_ref[b, :, M_WIDTH + h * M_HEAD_DIM:M_WIDTH + (h + 1) * M_HEAD_DIM]
            v = mqkv_ref[b, :, 2 * M_WIDTH + h * M_HEAD_DIM:2 * M_WIDTH + (h + 1) * M_HEAD_DIM]
            i_row = gt_refs[b][h:h + 1, :]
            lf_row = gt_refs[b][M_HEADS + h:M_HEADS + h + 1, :]
            i_col = g_ref[b, :, L_IG + h:L_IG + h + 1]
            r = b * M_HEADS + h
            m_prev = m_ref[r:r + 1, 0:1]
            hh, c_new, m_new = _mlstm_chunk(q, k, v, lf_row, i_row, i_col, c_ref[b, h], m_prev)
            c_ref[b, h] = c_new
            m_ref[r:r + 1, :] = jnp.broadcast_to(m_new, (1, LANES))
            hm_ref[b, :, hs] = (hh * so_ref[b, :, hs]).astype(BF16)


def _mlstm_prompt(mqkv, so, gates, gatest, nb, seq, chunk):
    nc = seq // chunk
    blk = lambda w: pl.BlockSpec((nb, chunk, w), lambda c: (0, c, 0))
    gt_specs = [pl.BlockSpec((2 * M_HEADS, chunk), lambda c, b=b: (0, b * nc + c)) for b in range(nb)]
    return pl.pallas_call(
        functools.partial(_mlstm_prompt_kernel, nb=nb),
        grid=(nc,),
        in_specs=[blk(3 * M_WIDTH), blk(M_WIDTH), blk(LANES)] + gt_specs,
        out_specs=[blk(M_WIDTH),
                   pl.BlockSpec((nb, M_HEADS, M_HEAD_DIM, 2 * M_HEAD_DIM), lambda c: (0, 0, 0, 0)),
                   pl.BlockSpec((nb * M_HEADS, LANES), lambda c: (0, 0))],
        out_shape=[jax.ShapeDtypeStruct((nb, seq, M_WIDTH), BF16),
                   jax.ShapeDtypeStruct((nb, M_HEADS, M_HEAD_DIM, 2 * M_HEAD_DIM), F32),
                   jax.ShapeDtypeStruct((nb * M_HEADS, LANES), F32)],
        compiler_params=_cparams(("arbitrary",)),
        name="mlstm_prompt",
    )(mqkv, so, gates, *([gatest] * nb))


ROWS_PAD = SUBLANES


def _mlstm_sample_kernel(mqkv_ref, so_ref, g_ref, c0_ref, n0_ref, m0_ref, hm_ref, c_ref, n_ref, m_ref, *, nseq, t_real):
    row = lax.broadcasted_iota(jnp.int32, (ROWS_PAD, 1), 0)
    real = row < t_real

    def per_seq(s, carry):
        r0 = pl.multiple_of(s * ROWS_PAD, ROWS_PAD)
        gt = g_ref[pl.ds(r0, ROWS_PAD), :]
        cum = gt
        for d in range(1, t_real):
            cum = cum + jnp.where(row >= d, pltpu.roll(gt, d, 0), 0.0)
        for h in range(M_HEADS):
            hs = slice(h * M_HEAD_DIM, (h + 1) * M_HEAD_DIM)
            q = mqkv_ref[pl.ds(r0, ROWS_PAD), hs]
            k = mqkv_ref[pl.ds(r0, ROWS_PAD), M_WIDTH + h * M_HEAD_DIM:M_WIDTH + (h + 1) * M_HEAD_DIM]
            v = mqkv_ref[pl.ds(r0, ROWS_PAD), 2 * M_WIDTH + h * M_HEAD_DIM:2 * M_WIDTH + (h + 1) * M_HEAD_DIM]
            qf, kf, vf = q.astype(F32), k.astype(F32), v.astype(F32)
            i_col = gt[:, L_IG + h:L_IG + h + 1]
            f_col = cum[:, L_FG + h:L_FG + h + 1]
            c0 = c0_ref[s, h]
            n0 = n0_ref[s, h:h + 1, :]
            m0 = m0_ref[pl.ds(s, 1), h:h + 1]
            inter = m0 + f_col
            dl = [jnp.where(real & (row >= u), f_col - f_col[u:u + 1, :] + i_col[u:u + 1, :], NEG_INF)
                  for u in range(t_real)]
            m_t = inter
            for u in range(t_real):
                m_t = jnp.maximum(m_t, dl[u])
            g = jnp.exp(inter - m_t)
            qc = _dot(q, c0.astype(BF16))
            num = g * qc
            den = g * jnp.sum(qf * n0, axis=1, keepdims=True)
            for u in range(t_real):
                su = jnp.sum(qf * kf[u:u + 1, :], axis=1, keepdims=True) * jnp.exp(dl[u] - m_t)
                num = num + su * vf[u:u + 1, :]
                den = den + su
            hh = num / jnp.maximum(jnp.abs(den), jnp.exp(-m_t))
            hm_ref[pl.ds(r0, ROWS_PAD), hs] = (hh * so_ref[pl.ds(r0, ROWS_PAD), hs]).astype(BF16)
            last = t_real - 1
            m_new = m_t[last:last + 1, :]
            g_c = g[last:last + 1, :]
            wk = jnp.where(real, jnp.exp(f_col[last:last + 1, :] - f_col + i_col - m_new), 0.0)
            kw = kf * wk
            c_ref[s, h] = g_c * c0 + _dot_tn(kw.astype(BF16), v)
            n_ref[s, h:h + 1, :] = g_c * n0 + jnp.sum(kw, axis=0, keepdims=True)
            m_ref[pl.ds(s, 1), h:h + 1] = m_new
        return carry

    lax.fori_loop(0, nseq, per_seq, 0)


def _mlstm_sample(mqkv, so, gates, c0, n0, m0, nseq_blk, t_real):
    nseq = c0.shape[0]
    rows = nseq_blk * ROWS_PAD
    rblk = lambda w: pl.BlockSpec((rows, w), lambda i: (i, 0))
    cblk = pl.BlockSpec((nseq_blk, M_HEADS, M_HEAD_DIM, M_HEAD_DIM), lambda i: (i, 0, 0, 0))
    nblk = pl.BlockSpec((nseq_blk, M_HEADS, M_HEAD_DIM), lambda i: (i, 0, 0))
    mblk = pl.BlockSpec((nseq_blk, M_HEADS), lambda i: (i, 0))
    return pl.pallas_call(
        functools.partial(_mlstm_sample_kernel, nseq=nseq_blk, t_real=t_real),
        grid=(nseq // nseq_blk,),
        in_specs=[rblk(3 * M_WIDTH), rblk(M_WIDTH), rblk(LANES), cblk, nblk, mblk],
        out_specs=[rblk(M_WIDTH), cblk, nblk, mblk],
        out_shape=[jax.ShapeDtypeStruct((nseq * ROWS_PAD, M_WIDTH), BF16),
                   jax.ShapeDtypeStruct(c0.shape, F32),
                   jax.ShapeDtypeStruct(n0.shape, F32),
                   jax.ShapeDtypeStruct(m0.shape, F32)],
        compiler_params=_cparams(("parallel",)),
        name="mlstm_sample",
    )(mqkv, so, gates, c0, n0, m0)


N_BISECT = 20
MASKED = -1e30


def _row_pass(sc_ref, nch, tk, init, fn):
    def body(c, acc):
        c0 = pl.multiple_of(c * tk, tk)
        blk = sc_ref[:, pl.ds(c0, tk)]
        for j in range(tk // LANES):
            acc = fn(acc, blk[:, j * LANES:(j + 1) * LANES], c0 + j * LANES)
        return acc
    return lax.fori_loop(0, nch, body, init)


def _select_threshold(sc_ref, nch, tk, kr, smax):
    rows = sc_ref.shape[0]
    zeros = jnp.zeros((rows, LANES), F32)
    lsum = lambda a: jnp.sum(a, axis=1, keepdims=True)

    def count_ge(thr):
        return lsum(_row_pass(sc_ref, nch, tk, zeros, lambda a, x, _: a + jnp.where(x >= thr, 1.0, 0.0)))

    def max_below(thr):
        acc = _row_pass(sc_ref, nch, tk, jnp.full((rows, LANES), NEG_INF, F32),
                        lambda a, x, _: jnp.maximum(a, jnp.where(x < thr, x, NEG_INF)))
        return jnp.max(acc, axis=1, keepdims=True)

    mx, mn = _row_pass(
        sc_ref, nch, tk, (jnp.full((rows, LANES), NEG_INF, F32), jnp.full((rows, LANES), jnp.inf, F32)),
        lambda a, x, _: (jnp.maximum(a[0], x), jnp.minimum(a[1], jnp.where(x > NEG_INF, x, jnp.inf))))
    rmax = jnp.max(mx, axis=1, keepdims=True)
    rmin = jnp.min(mn, axis=1, keepdims=True)
    c_max = count_ge(rmax)

    def bisect(_, st):
        lo, hi, c_hi = st
        mid = 0.5 * (lo + hi)
        c = count_ge(mid)
        ge = c >= kr
        return jnp.where(ge, mid, lo), jnp.where(ge, hi, mid), jnp.where(ge, c_hi, c)

    lo, hi, c_hi = lax.fori_loop(0, N_BISECT, bisect, (rmin, rmax, c_max))

    done0 = jnp.where(c_max >= kr, 1.0, 0.0)

    def peel_cond(st):
        it, _, _, _, _, _, done = st
        return jnp.logical_and(it <= smax, jnp.min(done) < 0.5)

    def peel(st):
        it, hi, c_hi, tau, c_gt, c_ge, done = st
        t1 = max_below(hi)
        c1 = count_ge(t1)
        fin = jnp.logical_and(c1 >= kr, done < 0.5)
        tau = jnp.where(fin, t1, tau)
        c_gt = jnp.where(fin, c_hi, c_gt)
        c_ge = jnp.where(fin, c1, c_ge)
        done = jnp.where(fin, 1.0, done)
        live = done < 0.5
        return it + 1, jnp.where(live, t1, hi), jnp.where(live, c1, c_hi), tau, c_gt, c_ge, done

    st = (jnp.int32(0), hi, c_hi, rmax, jnp.zeros_like(rmax), c_max, done0)
    _, _, _, tau, c_gt, c_ge, _ = lax.while_loop(peel_cond, peel, st)

    need = kr - c_gt
    surplus = jnp.max((c_ge - c_gt) - need)

    @pl.when(surplus > 0.5)
    def _():
        lane = lax.broadcasted_iota(jnp.int32, (1, LANES), 1)

        def count_eq_le(idx):
            return lsum(_row_pass(
                sc_ref, nch, tk, zeros,
                lambda a, x, c0: a + jnp.where(jnp.logical_and(x == tau, (lane + c0) <= idx), 1.0, 0.0)))

        def isearch(_, st):
            lo_i, hi_i = st
            mid = (lo_i + hi_i) >> 1
            ok = count_eq_le(mid) >= need
            return jnp.where(ok, lo_i, mid), jnp.where(ok, mid, hi_i)

        n_it = int(math.ceil(math.log2(smax + 1))) + 1
        _, sigma = lax.fori_loop(0, n_it, isearch,
                                 (jnp.full((rows, 1), -1, jnp.int32), jnp.full((rows, 1), smax - 1, jnp.int32)))

        def rewrite(c, carry):
            c0 = pl.multiple_of(c * tk, tk)
            x = sc_ref[:, pl.ds(c0, tk)]
            col = lax.broadcasted_iota(jnp.int32, (1, tk), 1) + c0
            sc_ref[:, pl.ds(c0, tk)] = jnp.where(jnp.logical_and(x == tau, col > sigma), NEG_INF, x)
            return carry

        lax.fori_loop(0, nch, rewrite, 0)

    return tau


def _dsa_prompt_kernel(qi_ref, g_ref, q_ref, ki_ref, k_ref, v_ref, o_ref, sc_ref, m_scr, l_scr, acc_scr, *, tq, tk, topk):
    i = pl.program_id(1)
    smax = sc_ref.shape[1]
    nch = ((i + 1) * tq + tk - 1) // tk
    row = lax.broadcasted_iota(jnp.int32, (tq, 1), 0) + i * tq

    def score_chunk(c, carry):
        c0 = pl.multiple_of(c * tk, tk)
        kblk = ki_ref[pl.ds(c0, tk), :]
        acc = jnp.zeros((tq, tk), F32)
        for h in range(IDX_HEADS):
            x = _dot_nt(qi_ref[:, h * LANES:(h + 1) * LANES], kblk)
            acc = acc + jnp.maximum(x, 0.0) * g_ref[:, L_WI + h:L_WI + h + 1]
        col = lax.broadcasted_iota(jnp.int32, (1, tk), 1) + c0
        sc_ref[:, pl.ds(c0, tk)] = jnp.where(col <= row, acc, NEG_INF)
        return carry

    lax.fori_loop(0, nch, score_chunk, 0)

    kr = jnp.minimum(row + 1, topk).astype(F32)
    tau = _select_threshold(sc_ref, nch, tk, kr, smax)

    m_scr[...] = jnp.full(m_scr.shape, MASKED, F32)
    l_scr[...] = jnp.zeros(l_scr.shape, F32)
    acc_scr[...] = jnp.zeros(acc_scr.shape, F32)

    def attend_chunk(c, carry):
        c0 = pl.multiple_of(c * tk, tk)
        sel = sc_ref[:, pl.ds(c0, tk)] >= tau
        for h in range(A_HEADS):
            pr = slice((h // 2) * LANES, (h // 2 + 1) * LANES)
            s = _dot_nt(q_ref[:, h * LANES:(h + 1) * LANES], k_ref[pl.ds(c0, tk), pr])
            s = jnp.where(sel, s, MASKED)
            m_old = m_scr[h][:, 0:1]
            m_new = jnp.maximum(m_old, jnp.max(s, axis=1, keepdims=True))
            alpha = jnp.exp(m_old - m_new)
            p = jnp.exp(s - m_new)
            l_scr[h] = jnp.broadcast_to(alpha * l_scr[h][:, 0:1] + jnp.sum(p, axis=1, keepdims=True), (tq, LANES))
            acc_scr[h] = alpha * acc_scr[h] + _dot(p.astype(BF16), v_ref[pl.ds(c0, tk), pr])
            m_scr[h] = jnp.broadcast_to(m_new, (tq, LANES))
        return carry

    lax.fori_loop(0, nch, attend_chunk, 0)

    lo_half = lax.broadcasted_iota(jnp.int32, (1, LANES), 1) < A_HEAD_DIM
    for p in range(A_HEADS // 2):
        even = acc_scr[2 * p] / l_scr[2 * p]
        odd = acc_scr[2 * p + 1] / l_scr[2 * p + 1]
        o_ref[:, p * LANES:(p + 1) * LANES] = jnp.where(lo_half, even, odd).astype(BF16)


def _dsa_prompt(qipad, gates, qpad, ki2, kb, vb, nb, seq, tq, tk):
    nq = seq // tq
    topk = min(TOPK_MAX, seq // 4)
    qrow = lambda w: pl.BlockSpec((tq, w), lambda b, i: (b * nq + i, 0))
    kfull = lambda w: pl.BlockSpec((seq, w), lambda b, i: (b, 0), pipeline_mode=pl.Buffered(1))
    return pl.pallas_call(
        functools.partial(_dsa_prompt_kernel, tq=tq, tk=tk, topk=topk),
        grid=(nb, nq),
        in_specs=[qrow(2 * A_WIDTH), qrow(LANES), qrow(2 * A_WIDTH), kfull(LANES), kfull(A_WIDTH), kfull(A_WIDTH)],
        out_specs=qrow(A_WIDTH),
        out_shape=jax.ShapeDtypeStruct((nb * seq, A_WIDTH), BF16),
        scratch_shapes=[pltpu.VMEM((tq, seq), F32),
                        pltpu.VMEM((A_HEADS, tq, LANES), F32),
                        pltpu.VMEM((A_HEADS, tq, LANES), F32),
                        pltpu.VMEM((A_HEADS, tq, LANES), F32)],
        compiler_params=_cparams(("parallel", "arbitrary")),
        name="dsa_prompt",
    )(qipad, gates, qpad, ki2, kb, vb)


def _dsa_sample_scores_kernel(pt_ref, q_ref, w_ref, *refs, n_pages, t_real):
    page_refs, new_ref, o_ref = refs[:n_pages], refs[n_pages], refs[n_pages + 1]
    q = q_ref[0]
    w = w_ref[0]
    past = n_pages * PAGE_SIZE
    trow = lax.broadcasted_iota(jnp.int32, (t_real, 1), 0)
    for p in range(n_pages + 1):
        keys = (page_refs[p][0].astype(BF16) if p < n_pages else new_ref[0])
        r = jnp.maximum(_dot_nt(q, keys), 0.0) * w
        sc = jnp.sum(r.reshape(t_real, IDX_HEADS, PAGE_SIZE), axis=1)
        if p == n_pages:
            col = lax.broadcasted_iota(jnp.int32, (1, PAGE_SIZE), 1)
            sc = jnp.where(col <= trow, sc, NEG_INF)
        o_ref[0, :, p * PAGE_SIZE:(p + 1) * PAGE_SIZE] = sc


def _dsa_sample_scores(page_table, qi, wi, cache_kidx, ki_new, t_real):
    nseq, n_pages = page_table.shape
    ncol = (n_pages + 1) * PAGE_SIZE
    per_seq = lambda a: pl.BlockSpec((1,) + a.shape[1:], lambda b, pt: (b, 0, 0))
    page = lambda p: pl.BlockSpec((1, PAGE_SIZE, IDX_DIM), lambda b, pt, p=p: (pt[b * n_pages + p], 0, 0))
    return pl.pallas_call(
        functools.partial(_dsa_sample_scores_kernel, n_pages=n_pages, t_real=t_real),
        grid_spec=pltpu.PrefetchScalarGridSpec(
            num_scalar_prefetch=1,
            grid=(nseq,),
            in_specs=[per_seq(qi), per_seq(wi)] + [page(p) for p in range(n_pages)] + [per_seq(ki_new)],
            out_specs=pl.BlockSpec((1, t_real, ncol), lambda b, pt: (b, 0, 0)),
        ),
        out_shape=jax.ShapeDtypeStruct((nseq, t_real, ncol), F32),
        compiler_params=_cparams(("parallel",)),
        name="dsa_sample_scores",
    )(page_table.reshape(-1), qi, wi, *([cache_kidx] * n_pages), ki_new)


def _dsa_sample_select_kernel(sc_ref, adj_ref, tau_ref, *, tk, topk):
    adj_ref[...] = sc_ref[...]
    rows, ncol = adj_ref.shape
    kr = jnp.full((rows, 1), float(topk), F32)
    tau = _select_threshold(adj_ref, ncol // tk, tk, kr, ncol)
    tau_ref[...] = jnp.broadcast_to(tau, (rows, LANES))


def _dsa_sample_select(sc, topk, rblk):
    rows, ncol = sc.shape
    return pl.pallas_call(
        functools.partial(_dsa_sample_select_kernel, tk=LANES, topk=topk),
        grid=(rows // rblk,),
        in_specs=[pl.BlockSpec((rblk, ncol), lambda i: (i, 0))],
        out_specs=[pl.BlockSpec((rblk, ncol), lambda i: (i, 0)), pl.BlockSpec((rblk, LANES), lambda i: (i, 0))],
        out_shape=[jax.ShapeDtypeStruct((rows, ncol), F32), jax.ShapeDtypeStruct((rows, LANES), F32)],
        compiler_params=_cparams(("parallel",)),
        name="dsa_sample_select",
    )(sc)


def _dsa_sample_attend_kernel(pt_ref, q_ref, sc_ref, tau_ref, *refs, n_pages, t_real):
    kp, vp = refs[:n_pages], refs[n_pages:2 * n_pages]
    knew_ref, vnew_ref, o_ref = refs[2 * n_pages:]
    q = q_ref[0]
    nrow = t_real * A_HEADS
    tau = tau_ref[0][:, 0:1]
    tau = jnp.broadcast_to(tau[:, None, :], (t_real, A_HEADS, 1)).reshape(nrow, 1)
    m = jnp.full((nrow, 1), MASKED, F32)
    l = jnp.zeros((nrow, 1), F32)
    acc = jnp.zeros((nrow, A_WIDTH), F32)
    for p in range(n_pages + 1):
        kk = kp[p][0].astype(BF16) if p < n_pages else knew_ref[0]
        vv = vp[p][0].astype(BF16) if p < n_pages else vnew_ref[0]
        x = sc_ref[0, :, p * PAGE_SIZE:(p + 1) * PAGE_SIZE]
        x = jnp.broadcast_to(x[:, None, :], (t_real, A_HEADS, PAGE_SIZE)).reshape(nrow, PAGE_SIZE)
        s = jnp.where(x >= tau, _dot_nt(q, kk), MASKED)
        m_new = jnp.maximum(m, jnp.max(s, axis=1, keepdims=True))
        alpha = jnp.exp(m - m_new)
        pr = jnp.exp(s - m_new)
        l = alpha * l + jnp.sum(pr, axis=1, keepdims=True)
        acc = alpha * acc + _dot(pr.astype(BF16), vv)
        m = m_new
    out = acc / l
    head_of_row = lax.broadcasted_iota(jnp.int32, (nrow, 1), 0) % A_HEADS
    head_of_lane = lax.broadcasted_iota(jnp.int32, (1, A_WIDTH), 1) // A_HEAD_DIM
    out = jnp.where(head_of_row == head_of_lane, out, 0.0)
    o_ref[0] = jnp.sum(out.reshape(t_real, A_HEADS, A_WIDTH), axis=1).astype(BF16)


def _dsa_sample_attend(page_table, qbd, sc_adj, tau, cache_k, cache_v, k_new, v_new, t_real):
    nseq, n_pages = page_table.shape
    per_seq = lambda a: pl.BlockSpec((1,) + a.shape[1:], lambda b, pt: (b, 0, 0))
    page = lambda p: pl.BlockSpec((1, PAGE_SIZE, A_WIDTH), lambda b, pt, p=p: (pt[b * n_pages + p], 0, 0))
    pages = lambda: [page(p) for p in range(n_pages)]
    return pl.pallas_call(
        functools.partial(_dsa_sample_attend_kernel, n_pages=n_pages, t_real=t_real),
        grid_spec=pltpu.PrefetchScalarGridSpec(
            num_scalar_prefetch=1,
            grid=(nseq,),
            in_specs=[per_seq(qbd), per_seq(sc_adj), per_seq(tau)] + pages() + pages() + [per_seq(k_new), per_seq(v_new)],
            out_specs=pl.BlockSpec((1, t_real, A_WIDTH), lambda b, pt: (b, 0, 0)),
        ),
        out_shape=jax.ShapeDtypeStruct((nseq, t_real, A_WIDTH), BF16),
        compiler_params=_cparams(("parallel",)),
        name="dsa_sample_attend",
    )(page_table.reshape(-1), qbd, sc_adj, tau, *([cache_k] * n_pages), *([cache_v] * n_pages), k_new, v_new)


PROJ_TM = 256
OUT_TM = 256
MLSTM_CHUNK = 128
DSA_TQ = 128
DSA_TK = 256
SAMPLE_SEQ_BLK = 8
SELECT_ROWS = 128


def _unpad_heads(xpad, n_heads):
    x = xpad.reshape(xpad.shape[:-1] + (n_heads, 2, LANES // 2))
    return x[..., 0, :] + x[..., 1, :]


def kernel(x_prompt, x_sample, cache_k, cache_v, cache_kidx, state_C, state_n, state_m, page_table,
           w_in, b_gate, w_out, ln1_g, ln1_b, w_gate, w_up, w_down, ln2_g, ln2_b):
    bp, sp, _ = x_prompt.shape
    bs, ts, _ = x_sample.shape
    n_pages = page_table.shape[1]
    past = n_pages * PAGE_SIZE
    n_pool = cache_k.shape[1]
    assert DEPTH == 1 and w_in.shape[0] == 1

    wp, wgt, bias, biast = _prep_w_in(w_in[0], b_gate[0])
    wo, wg, wu, wd = (w[0].astype(BF16) for w in (w_out, w_gate, w_up, w_down))
    g1, b1, g2, b2 = (v[0].astype(F32)[None, :] for v in (ln1_g, ln1_b, ln2_g, ln2_b))

    mp = bp * sp
    xp = x_prompt.reshape(mp, D_MODEL)
    tabs_p = _rope_tables(jnp.arange(sp, dtype=jnp.int32))
    (mqkv, so, gates, gatest, qpad, k_p, kb, v_p, vb, qipad, ki_p, ki2) = _proj(
        xp, wp, wgt, bias, biast, tabs_p, PROJ_TM, sp // PROJ_TM)
    hm_p, caug, m_p = _mlstm_prompt(mqkv.reshape(bp, sp, -1), so.reshape(bp, sp, -1), gates.reshape(bp, sp, -1),
                                    gatest, bp, sp, MLSTM_CHUNK)
    ha_p = _dsa_prompt(qipad, gates, qpad, ki2, kb, vb, bp, sp, DSA_TQ, DSA_TK)
    y_p = _block_out(xp, hm_p.reshape(mp, -1), ha_p, wo, g1, b1, wg, wu, wd, g2, b2, OUT_TM)

    ms = bs * ts
    xs = x_sample.reshape(ms, D_MODEL)
    tabs_s = _rope_tables(jnp.tile(past + jnp.arange(ts, dtype=jnp.int32), bs))
    (mqkv_s, so_s, gates_s, _, qpad_s, k_s, kb_s, v_s, vb_s, qipad_s, ki_s, _) = _proj(
        xs, wp, wgt, bias, biast, tabs_s, ms, 1)
    pad_rows = lambda a: jnp.pad(a.reshape(bs, ts, -1), ((0, 0), (0, ROWS_PAD - ts), (0, 0))).reshape(bs * ROWS_PAD, -1)
    hm_s, c_s, n_s, m_s = _mlstm_sample(pad_rows(mqkv_s), pad_rows(so_s), pad_rows(gates_s),
                                        state_C[0].astype(F32), state_n[0].astype(F32), state_m[0].astype(F32),
                                        SAMPLE_SEQ_BLK, ts)
    hm_s = hm_s.reshape(bs, ROWS_PAD, -1)[:, :ts].reshape(ms, -1)

    qi_s = _unpad_heads(qipad_s, IDX_HEADS).reshape(bs, ts * IDX_HEADS, IDX_DIM)
    wi_s = gates_s[:, L_WI:L_WI + IDX_HEADS].reshape(bs, ts * IDX_HEADS, 1)
    pad_page = lambda a: jnp.pad(a.reshape(bs, ts, -1), ((0, 0), (0, PAGE_SIZE - ts), (0, 0)))
    sc_s = _dsa_sample_scores(page_table, qi_s, wi_s, cache_kidx[0], pad_page(ki_s).astype(BF16), ts)
    ncol = sc_s.shape[-1]
    topk_s = min(TOPK_MAX, (past + ts) // 4)
    sc_adj, tau_s = _dsa_sample_select(sc_s.reshape(ms, ncol), topk_s, SELECT_ROWS)
    q_s = _unpad_heads(qpad_s, A_HEADS).reshape(bs, ts, A_HEADS, A_HEAD_DIM)
    eye = jnp.eye(A_HEADS, dtype=q_s.dtype)
    qbd = (q_s[:, :, :, None, :] * eye[None, None, :, :, None]).reshape(bs, ts * A_HEADS, A_WIDTH)
    ha_s = _dsa_sample_attend(page_table, qbd, sc_adj.reshape(bs, ts, ncol), tau_s.reshape(bs, ts, LANES),
                              cache_k[0].reshape(n_pool, PAGE_SIZE, A_WIDTH), cache_v[0].reshape(n_pool, PAGE_SIZE, A_WIDTH),
                              pad_page(kb_s), pad_page(vb_s), ts)
    y_s = _block_out(xs, hm_s, ha_s.reshape(ms, -1), wo, g1, b1, wg, wu, wd, g2, b2, min(OUT_TM, ms))

    heads = lambda a, b, t: a.reshape(1, b, t, A_HEADS, A_HEAD_DIM)
    return (y_p.reshape(bp, sp, D_MODEL), y_s.reshape(bs, ts, D_MODEL),
            heads(k_p, bp, sp), heads(v_p, bp, sp), ki_p.reshape(1, bp, sp, IDX_DIM),
            caug[None, :, :, :, 0:M_HEAD_DIM], caug[None, :, :, :, M_HEAD_DIM], m_p[:, 0].reshape(1, bp, M_HEADS),
            heads(k_s, bs, ts), heads(v_s, bs, ts), ki_s.reshape(1, bs, ts, IDX_DIM),
            c_s[None], n_s[None], m_s[None])
```

```python
import functools
import math

import jax
import jax.numpy as jnp
from jax import lax
from jax.experimental import pallas as pl
from jax.experimental.pallas import tpu as pltpu

D_MODEL = 1024
M_HEADS = 4
M_HEAD_DIM = 128
M_WIDTH = 512
A_HEADS = 8
A_HEAD_DIM = 64
A_WIDTH = 512
IDX_HEADS = 8
IDX_DIM = 64
TOPK_MAX = 256
PAGE_SIZE = 128
ROPE_THETA = 500000.0
ROT = A_HEAD_DIM // 4
HALF = ROT // 2
D_FF = 2816
DEPTH = 1
ALPHA = (2 * DEPTH) ** 0.25
LN_EPS = 1e-5

LANES = 128
SUBLANES = 8
VMEM_LIMIT = 56 * 1024 * 1024

NEG_INF = float("-inf")
BF16 = jnp.bfloat16
F32 = jnp.float32

C_MQ, C_MK, C_MV, C_MO = 0, 512, 1024, 1536
C_AQ, C_AK, C_AV, C_IQ = 2048, 2560, 3072, 3584
C_SM = 4096
N_PERM = 4224
L_WI, L_IG, L_FG = 64, 72, 76


def _cparams(sem):
    return pltpu.CompilerParams(dimension_semantics=sem, vmem_limit_bytes=VMEM_LIMIT)


def _dot(a, b):
    return jnp.dot(a, b, preferred_element_type=F32)


def _dot_nt(a, b):
    return lax.dot_general(a, b, (((1,), (1,)), ((), ())), preferred_element_type=F32)


def _dot_tn(a, b):
    return lax.dot_general(a, b, (((0,), (0,)), ((), ())), preferred_element_type=F32)


def _rope128(x, cos, sa, sb):
    return x * cos + pltpu.roll(x, LANES - HALF, 1) * sa + pltpu.roll(x, HALF, 1) * sb


def _proj_kernel(x_ref, w_ref, wgt_ref, bias_ref, biast_ref, cos_ref, sa_ref, sb_ref,
                 mqkv_ref, so_ref, gates_ref, gatest_ref, qpad_ref, k_ref, kb_ref, v_ref, vb_ref,
                 qipad_ref, ki_ref, ki2_ref):
    xb = x_ref[...].astype(BF16)
    cos, sa, sb = cos_ref[...], sa_ref[...], sb_ref[...]
    lane = lax.broadcasted_iota(jnp.int32, (1, LANES), 1)
    lo_half = lane < A_HEAD_DIM

    zm = _dot(xb, w_ref[:, C_MQ:C_MO])
    mqkv_ref[:, 0:M_WIDTH] = zm[:, 0:M_WIDTH].astype(BF16)
    mqkv_ref[:, M_WIDTH:2 * M_WIDTH] = (zm[:, M_WIDTH:2 * M_WIDTH] * (M_HEAD_DIM ** -0.5)).astype(BF16)
    mqkv_ref[:, 2 * M_WIDTH:3 * M_WIDTH] = zm[:, 2 * M_WIDTH:3 * M_WIDTH].astype(BF16)
    so_ref[...] = jax.nn.sigmoid(_dot(xb, w_ref[:, C_MO:C_AQ]))

    def padded_heads(z, scale, out_ref):
        for p in range(A_WIDTH // LANES):
            r = _rope128(z[:, p * LANES:(p + 1) * LANES], cos, sa, sb) * scale
            out_ref[:, (2 * p) * LANES:(2 * p + 1) * LANES] = jnp.where(lo_half, r, 0.0).astype(BF16)
            out_ref[:, (2 * p + 1) * LANES:(2 * p + 2) * LANES] = jnp.where(lo_half, 0.0, r).astype(BF16)

    padded_heads(_dot(xb, w_ref[:, C_AQ:C_AK]), A_HEAD_DIM ** -0.5, qpad_ref)
    padded_heads(_dot(xb, w_ref[:, C_IQ:C_SM]), IDX_DIM ** -0.5, qipad_ref)

    zk = _dot(xb, w_ref[:, C_AK:C_AV])
    for p in range(A_WIDTH // LANES):
        r = _rope128(zk[:, p * LANES:(p + 1) * LANES], cos, sa, sb)
        k_ref[:, p * LANES:(p + 1) * LANES] = r
        kb_ref[:, p * LANES:(p + 1) * LANES] = r.astype(BF16)
    zv = _dot(xb, w_ref[:, C_AV:C_IQ])
    v_ref[...] = zv
    vb_ref[...] = zv.astype(BF16)

    zs = _dot(xb, w_ref[:, C_SM:N_PERM]) + bias_ref[...]
    rs = _rope128(zs, jnp.where(lo_half, cos, 1.0), jnp.where(lo_half, sa, 0.0), jnp.where(lo_half, sb, 0.0))
    is_wi = (lane >= L_WI) & (lane < L_IG)
    is_fg = (lane >= L_FG) & (lane < L_FG + M_HEADS)
    g = jnp.where(is_wi, rs * (IDX_HEADS ** -0.5), rs)
    g = jnp.where(is_fg, jax.nn.log_sigmoid(rs), g)
    gates_ref[...] = g
    ki = jnp.where(lo_half, rs, 0.0)
    ki_ref[...] = rs[:, 0:IDX_DIM]
    ki2_ref[...] = (ki + pltpu.roll(ki, A_HEAD_DIM, 1)).astype(BF16)

    zt = _dot_nt(wgt_ref[...], xb) + biast_ref[...]
    row = lax.broadcasted_iota(jnp.int32, (2 * M_HEADS, 1), 0)
    gatest_ref[...] = jnp.where(row >= M_HEADS, jax.nn.log_sigmoid(zt), zt)


def _rope_tables(pos):
    inv = ROPE_THETA ** (-jnp.arange(HALF, dtype=F32) / HALF)
    ang = pos.astype(F32)[:, None] * inv[None, :]
    c, s = jnp.cos(ang), jnp.sin(ang)
    n = pos.shape[0]
    one = jnp.ones((n, A_HEAD_DIM - ROT), F32)
    zero = jnp.zeros((n, A_HEAD_DIM - ROT), F32)
    z8 = jnp.zeros((n, HALF), F32)
    cos64 = jnp.concatenate([c, c, one], axis=1)
    sa64 = jnp.concatenate([-s, z8, zero], axis=1)
    sb64 = jnp.concatenate([z8, s, zero], axis=1)
    two = lambda t: jnp.concatenate([t, t], axis=1)
    return two(cos64), two(sa64), two(sb64)


def _proj(x, wp, wgt, bias, biast, tabs, tm, tab_blocks):
    m = x.shape[0]
    grid = (m // tm,)
    row = lambda w: pl.BlockSpec((tm, w), lambda i: (i, 0))
    full = lambda a: pl.BlockSpec(a.shape, lambda i: (0, 0))
    tab = pl.BlockSpec((tm, LANES), lambda i: (i % tab_blocks, 0))
    outs = [
        ((m, 3 * M_WIDTH), BF16, row(3 * M_WIDTH)),
        ((m, M_WIDTH), F32, row(M_WIDTH)),
        ((m, LANES), F32, row(LANES)),
        ((2 * M_HEADS, m), F32, pl.BlockSpec((2 * M_HEADS, tm), lambda i: (0, i))),
        ((m, 2 * A_WIDTH), BF16, row(2 * A_WIDTH)),
        ((m, A_WIDTH), F32, row(A_WIDTH)),
        ((m, A_WIDTH), BF16, row(A_WIDTH)),
        ((m, A_WIDTH), F32, row(A_WIDTH)),
        ((m, A_WIDTH), BF16, row(A_WIDTH)),
        ((m, 2 * A_WIDTH), BF16, row(2 * A_WIDTH)),
        ((m, IDX_DIM), F32, row(IDX_DIM)),
        ((m, LANES), BF16, row(LANES)),
    ]
    return pl.pallas_call(
        _proj_kernel,
        grid=grid,
        in_specs=[row(D_MODEL), full(wp), full(wgt), full(bias), full(biast), tab, tab, tab],
        out_specs=[o[2] for o in outs],
        out_shape=[jax.ShapeDtypeStruct(o[0], o[1]) for o in outs],
        compiler_params=_cparams(("parallel",)),
        name="proj",
    )(x, wp, wgt, bias, biast, *tabs)


def _prep_w_in(w_in, b_gate):
    s = [0, 512, 1024, 1536, 2048, 2052, 2056, 2568, 3080, 3592, 4104, 4168, 4176]
    mq, mk, mv, mo, mi, mf, aq, ak, av, iq, ik, iw = [w_in[:, s[i]:s[i + 1]] for i in range(12)]
    pad = jnp.zeros((D_MODEL, N_PERM - C_SM - 80), w_in.dtype)
    wp = jnp.concatenate([mq, mk, mv, mo, aq, ak, av, iq, ik, iw, mi, mf, pad], axis=1).astype(BF16)
    wgt = jnp.concatenate([mi, mf], axis=1).T.astype(BF16)
    bg = b_gate.astype(F32)
    bias = jnp.zeros((1, LANES), F32).at[0, L_IG:L_IG + 2 * M_HEADS].set(bg)
    return wp, wgt, bias, bg[:, None]


FF_CHUNK = D_FF // 2


def _layer_norm(x, g, b):
    mu = jnp.mean(x, axis=-1, keepdims=True)
    xc = x - mu
    var = jnp.mean(xc * xc, axis=-1, keepdims=True)
    return xc * lax.rsqrt(var + LN_EPS) * g + b


def _block_out_kernel(x_ref, hm_ref, ha_ref, wo_ref, g1_ref, b1_ref, wg_ref, wu_ref, wd_ref, g2_ref, b2_ref, y_ref):
    mix = _dot(hm_ref[...], wo_ref[0:M_WIDTH, :]) + _dot(ha_ref[...], wo_ref[M_WIDTH:2 * M_WIDTH, :])
    x1 = _layer_norm(ALPHA * x_ref[...] + mix, g1_ref[...], b1_ref[...])
    xb = x1.astype(BF16)
    ff = jnp.zeros_like(x1)
    for c in range(D_FF // FF_CHUNK):
        sl = slice(c * FF_CHUNK, (c + 1) * FF_CHUNK)
        act = jax.nn.silu(_dot(xb, wg_ref[:, sl])) * _dot(xb, wu_ref[:, sl])
        ff = ff + _dot(act.astype(BF16), wd_ref[sl, :])
    y_ref[...] = _layer_norm(ALPHA * x1 + ff, g2_ref[...], b2_ref[...])


def _block_out(x, hm, ha, wo, g1, b1, wg, wu, wd, g2, b2, tm):
    m = x.shape[0]
    row = lambda w: pl.BlockSpec((tm, w), lambda i: (i, 0))
    full = lambda a: pl.BlockSpec(a.shape, lambda i: (0, 0), pipeline_mode=pl.Buffered(1))
    return pl.pallas_call(
        _block_out_kernel,
        grid=(m // tm,),
        in_specs=[row(D_MODEL), row(M_WIDTH), row(A_WIDTH), full(wo), full(g1), full(b1),
                  full(wg), full(wu), full(wd), full(g2), full(b2)],
        out_specs=row(D_MODEL),
        out_shape=jax.ShapeDtypeStruct((m, D_MODEL), F32),
        compiler_params=_cparams(("parallel",)),
        name="block_out",
    )(x, hm, ha, wo, g1, b1, wg, wu, wd, g2, b2)


def _split3(x):
    h1 = x.astype(BF16)
    r1 = x - h1.astype(F32)
    h2 = r1.astype(BF16)
    h3 = (r1 - h2.astype(F32)).astype(BF16)
    return h1, h2, h3


def _dot3(x, w):
    h1, h2, h3 = _split3(x)
    return _dot(h1, w) + _dot(h2, w) + _dot(h3, w)


def _mlstm_chunk(q, k, v, lf_row, i_row, i_col, c_aug, m_prev):
    L = q.shape[0]
    t_idx = lax.broadcasted_iota(jnp.int32, (L, L), 0)
    s_idx = lax.broadcasted_iota(jnp.int32, (L, L), 1)
    causal = s_idx <= t_idx
    ones_b = jnp.ones((L, LANES), BF16)
    G = _dot3(jnp.where(causal, lf_row, 0.0), ones_b)
    tri_u = jnp.where(t_idx <= s_idx, 1.0, 0.0).astype(BF16)
    f_row = _dot3(jnp.broadcast_to(lf_row, (SUBLANES, L)), tri_u)[0:1, :]
    f_col = G[:, 0:1]
    dlog = jnp.where(causal, f_col - f_row + i_row, NEG_INF)
    inter = m_prev + f_col
    m_t = jnp.maximum(jnp.max(dlog, axis=1, keepdims=True), inter)
    w = jnp.exp(dlog - m_t)
    g = jnp.exp(inter - m_t)
    s = _dot_nt(q, k) * w
    lane = lax.broadcasted_iota(jnp.int32, (L, LANES), 1)
    v_aug = jnp.concatenate([v, jnp.where(lane == 0, 1.0, 0.0).astype(BF16)], axis=1)
    nd = _dot(s.astype(BF16), v_aug) + g * _dot(q, c_aug.astype(BF16))
    den = nd[:, M_HEAD_DIM:M_HEAD_DIM + 1]
    h = nd[:, 0:M_HEAD_DIM] / jnp.maximum(jnp.abs(den), jnp.exp(-m_t))
    m_new = m_t[L - 1:L, :]
    g_c = g[L - 1:L, :]
    wk = jnp.exp(f_col[L - 1:L, :] - f_col + i_col - m_new)
    kw = (k.astype(F32) * wk).astype(BF16)
    c_new = g_c * c_aug + _dot_tn(kw, v_aug)
    return h, c_new, m_new


def _mlstm_prompt_kernel(*refs, nb):
    mqkv_ref, so_ref, g_ref = refs[0:3]
    gt_refs = refs[3:3 + nb]
    hm_ref, c_ref, m_ref = refs[3 + nb:]
    step = pl.program_id(0)

    @pl.when(step == 0)
    def _():
        c_ref[...] = jnp.zeros_like(c_ref)
        m_ref[...] = jnp.zeros_like(m_ref)

    for b in range(nb):
        for h in range(M_HEADS):
            hs = slice(h * M_HEAD_DIM, (h + 1) * M_HEAD_DIM)
            q = mqkv_ref[b, :, hs]
            k = mqkv_ref[b, :, M_WIDTH + h * M_HEAD_DIM:M_WIDTH + (h + 1) * M_HEAD_DIM]
            v = mqkv_ref[b, :, 2 * M_WIDTH + h * M_HEAD_DIM:2 * M_WIDTH + (h + 1) * M_HEAD_DIM]
            i_row = gt_refs[b][h:h + 1, :]
            lf_row = gt_refs[b][M_HEADS + h:M_HEADS + h + 1, :]
            i_col = g_ref[b, :, L_IG + h:L_IG + h + 1]
            r = b * M_HEADS + h
            m_prev = m_ref[r:r + 1, 0:1]
            hh, c_new, m_new = _mlstm_chunk(q, k, v, lf_row, i_row, i_col, c_ref[b, h], m_prev)
            c_ref[b, h] = c_new
            m_ref[r:r + 1, :] = jnp.broadcast_to(m_new, (1, LANES))
            hm_ref[b, :, hs] = (hh * so_ref[b, :, hs]).astype(BF16)


def _mlstm_prompt(mqkv, so, gates, gatest, nb, seq, chunk):
    nc = seq // chunk
    blk = lambda w: pl.BlockSpec((nb, chunk, w), lambda c: (0, c, 0))
    gt_specs = [pl.BlockSpec((2 * M_HEADS, chunk), lambda c, b=b: (0, b * nc + c)) for b in range(nb)]
    return pl.pallas_call(
        functools.partial(_mlstm_prompt_kernel, nb=nb),
        grid=(nc,),
        in_specs=[blk(3 * M_WIDTH), blk(M_WIDTH), blk(LANES)] + gt_specs,
        out_specs=[blk(M_WIDTH),
                   pl.BlockSpec((nb, M_HEADS, M_HEAD_DIM, 2 * M_HEAD_DIM), lambda c: (0, 0, 0, 0)),
                   pl.BlockSpec((nb * M_HEADS, LANES), lambda c: (0, 0))],
        out_shape=[jax.ShapeDtypeStruct((nb, seq, M_WIDTH), BF16),
                   jax.ShapeDtypeStruct((nb, M_HEADS, M_HEAD_DIM, 2 * M_HEAD_DIM), F32),
                   jax.ShapeDtypeStruct((nb * M_HEADS, LANES), F32)],
        compiler_params=_cparams(("arbitrary",)),
        name="mlstm_prompt",
    )(mqkv, so, gates, *([gatest] * nb))


ROWS_PAD = SUBLANES


def _mlstm_sample_kernel(mqkv_ref, so_ref, g_ref, c0_ref, n0_ref, m0_ref, hm_ref, c_ref, n_ref, m_ref, *, nseq, t_real):
    row = lax.broadcasted_iota(jnp.int32, (ROWS_PAD, 1), 0)
    real = row < t_real

    def per_seq(s, carry):
        r0 = pl.multiple_of(s * ROWS_PAD, ROWS_PAD)
        gt = g_ref[pl.ds(r0, ROWS_PAD), :]
        cum = gt
        for d in range(1, t_real):
            cum = cum + jnp.where(row >= d, pltpu.roll(gt, d, 0), 0.0)
        for h in range(M_HEADS):
            hs = slice(h * M_HEAD_DIM, (h + 1) * M_HEAD_DIM)
            q = mqkv_ref[pl.ds(r0, ROWS_PAD), hs]
            k = mqkv_ref[pl.ds(r0, ROWS_PAD), M_WIDTH + h * M_HEAD_DIM:M_WIDTH + (h + 1) * M_HEAD_DIM]
            v = mqkv_ref[pl.ds(r0, ROWS_PAD), 2 * M_WIDTH + h * M_HEAD_DIM:2 * M_WIDTH + (h + 1) * M_HEAD_DIM]
            qf, kf, vf = q.astype(F32), k.astype(F32), v.astype(F32)
            i_col = gt[:, L_IG + h:L_IG + h + 1]
            f_col = cum[:, L_FG + h:L_FG + h + 1]
            c0 = c0_ref[s, h]
            n0 = n0_ref[s, h:h + 1, :]
            m0 = m0_ref[pl.ds(s, 1), h:h + 1]
            inter = m0 + f_col
            dl = [jnp.where(real & (row >= u), f_col - f_col[u:u + 1, :] + i_col[u:u + 1, :], NEG_INF)
                  for u in range(t_real)]
            m_t = inter
            for u in range(t_real):
                m_t = jnp.maximum(m_t, dl[u])
            g = jnp.exp(inter - m_t)
            qc = _dot(q, c0.astype(BF16))
            num = g * qc
            den = g * jnp.sum(qf * n0, axis=1, keepdims=True)
            for u in range(t_real):
                su = jnp.sum(qf * kf[u:u + 1, :], axis=1, keepdims=True) * jnp.exp(dl[u] - m_t)
                num = num + su * vf[u:u + 1, :]
                den = den + su
            hh = num / jnp.maximum(jnp.abs(den), jnp.exp(-m_t))
            hm_ref[pl.ds(r0, ROWS_PAD), hs] = (hh * so_ref[pl.ds(r0, ROWS_PAD), hs]).astype(BF16)
            last = t_real - 1
            m_new = m_t[last:last + 1, :]
            g_c = g[last:last + 1, :]
            wk = jnp.where(real, jnp.exp(f_col[last:last + 1, :] - f_col + i_col - m_new), 0.0)
            kw = kf * wk
            c_ref[s, h] = g_c * c0 + _dot_tn(kw.astype(BF16), v)
            n_ref[s, h:h + 1, :] = g_c * n0 + jnp.sum(kw, axis=0, keepdims=True)
            m_ref[pl.ds(s, 1), h:h + 1] = m_new
        return carry

    lax.fori_loop(0, nseq, per_seq, 0)


def _mlstm_sample(mqkv, so, gates, c0, n0, m0, nseq_blk, t_real):
    nseq = c0.shape[0]
    rows = nseq_blk * ROWS_PAD
    rblk = lambda w: pl.BlockSpec((rows, w), lambda i: (i, 0))
    cblk = pl.BlockSpec((nseq_blk, M_HEADS, M_HEAD_DIM, M_HEAD_DIM), lambda i: (i, 0, 0, 0))
    nblk = pl.BlockSpec((nseq_blk, M_HEADS, M_HEAD_DIM), lambda i: (i, 0, 0))
    mblk = pl.BlockSpec((nseq_blk, M_HEADS), lambda i: (i, 0))
    return pl.pallas_call(
        functools.partial(_mlstm_sample_kernel, nseq=nseq_blk, t_real=t_real),
        grid=(nseq // nseq_blk,),
        in_specs=[rblk(3 * M_WIDTH), rblk(M_WIDTH), rblk(LANES), cblk, nblk, mblk],
        out_specs=[rblk(M_WIDTH), cblk, nblk, mblk],
        out_shape=[jax.ShapeDtypeStruct((nseq * ROWS_PAD, M_WIDTH), BF16),
                   jax.ShapeDtypeStruct(c0.shape, F32),
                   jax.ShapeDtypeStruct(n0.shape, F32),
                   jax.ShapeDtypeStruct(m0.shape, F32)],
        compiler_params=_cparams(("parallel",)),
        name="mlstm_sample",
    )(mqkv, so, gates, c0, n0, m0)


SEARCH_SCHEDULE = "bbbbiiibii"
MASKED = -1e30


ROW_BLOCK = 128
SELECT_TK = 256


def _row_pass(sc_ref, nch, tk, init, fn, params=()):
    rows = sc_ref.shape[0]
    rb = min(ROW_BLOCK, rows)
    outs = []
    for r0 in range(0, rows, rb):
        rsl = lambda t: jax.tree.map(lambda a: a[r0:r0 + rb], t)
        prm = rsl(params)

        def body(c, acc, r0=r0, prm=prm):
            c0 = pl.multiple_of(c * tk, tk)
            blk = sc_ref[r0:r0 + rb, pl.ds(c0, tk)]
            for j in range(tk // LANES):
                acc = fn(acc, blk[:, j * LANES:(j + 1) * LANES], c0 + j * LANES, prm)
            return acc

        outs.append(lax.fori_loop(0, nch, body, rsl(init)))
    return jax.tree.map(lambda *a: jnp.concatenate(a, axis=0), *outs)


def _select_threshold(sc_ref, nch, tk, kr, smax):
    rows = sc_ref.shape[0]
    zeros = jnp.zeros((rows, LANES), F32)
    lsum = lambda a: jnp.sum(a, axis=1, keepdims=True)

    full = lambda col: jnp.broadcast_to(col, (rows, LANES))

    def count_ge(thr):
        return lsum(_row_pass(sc_ref, nch, tk, zeros, lambda a, x, _, t: a + jnp.where(x >= t, 1.0, 0.0), full(thr)))

    def max_below(thr):
        acc = _row_pass(sc_ref, nch, tk, jnp.full((rows, LANES), NEG_INF, F32),
                        lambda a, x, _, t: jnp.maximum(a, jnp.where(x < t, x, NEG_INF)), full(thr))
        return jnp.max(acc, axis=1, keepdims=True)

    def range_fn(a, x, _, prm):
        fin = x > NEG_INF
        return (jnp.maximum(a[0], x), jnp.minimum(a[1], jnp.where(fin, x, jnp.inf)), a[2] + jnp.where(fin, 1.0, 0.0))

    mx, mn, nf = _row_pass(
        sc_ref, nch, tk, (jnp.full((rows, LANES), NEG_INF, F32), jnp.full((rows, LANES), jnp.inf, F32), zeros), range_fn)
    rmax = jnp.max(mx, axis=1, keepdims=True)
    rmin = jnp.min(mn, axis=1, keepdims=True)
    c_max = count_ge(rmax)

    lo, hi, c_lo, c_hi = rmin, rmax, lsum(nf), c_max
    for kind in SEARCH_SCHEDULE:
        if kind == "b":
            mid = 0.5 * (lo + hi)
        else:
            frac = jnp.clip((c_lo - kr + 0.5) / jnp.maximum(c_lo - c_hi, 1.0), 0.02, 0.98)
            mid = lo + (hi - lo) * frac
        c = count_ge(mid)
        ge = c >= kr
        lo, c_lo = jnp.where(ge, mid, lo), jnp.where(ge, c, c_lo)
        hi, c_hi = jnp.where(ge, hi, mid), jnp.where(ge, c_hi, c)

    done0 = jnp.where(c_max >= kr, 1.0, 0.0)

    def peel_cond(st):
        it, _, _, _, _, _, done = st
        return jnp.logical_and(it <= smax, jnp.min(done) < 0.5)

    def peel(st):
        it, hi, c_hi, tau, c_gt, c_ge, done = st
        t1 = max_below(hi)
        c1 = count_ge(t1)
        fin = jnp.logical_and(c1 >= kr, done < 0.5)
        tau = jnp.where(fin, t1, tau)
        c_gt = jnp.where(fin, c_hi, c_gt)
        c_ge = jnp.where(fin, c1, c_ge)
        done = jnp.where(fin, 1.0, done)
        live = done < 0.5
        return it + 1, jnp.where(live, t1, hi), jnp.where(live, c1, c_hi), tau, c_gt, c_ge, done

    st = (jnp.int32(0), hi, c_hi, rmax, jnp.zeros_like(rmax), c_max, done0)
    _, _, _, tau, c_gt, c_ge, _ = lax.while_loop(peel_cond, peel, st)

    need = kr - c_gt
    surplus = jnp.max((c_ge - c_gt) - need)

    @pl.when(surplus > 0.5)
    def _():
        lane = lax.broadcasted_iota(jnp.int32, (1, LANES), 1)

        tau_f = full(tau)

        def count_eq_le(idx):
            return lsum(_row_pass(
                sc_ref, nch, tk, zeros,
                lambda a, x, c0, p: a + jnp.where(jnp.logical_and(x == p[0], (lane + c0) <= p[1]), 1.0, 0.0),
                (tau_f, jnp.broadcast_to(idx, (rows, LANES)))))

        def isearch(_, st):
            lo_i, hi_i = st
            mid = (lo_i + hi_i) >> 1
            ok = count_eq_le(mid) >= need
            return jnp.where(ok, lo_i, mid), jnp.where(ok, mid, hi_i)

        n_it = int(math.ceil(math.log2(smax + 1))) + 1
        _, sigma = lax.fori_loop(0, n_it, isearch,
                                 (jnp.full((rows, 1), -1, jnp.int32), jnp.full((rows, 1), smax - 1, jnp.int32)))

        def rewrite(c, carry):
            c0 = pl.multiple_of(c * tk, tk)
            x = sc_ref[:, pl.ds(c0, tk)]
            col = lax.broadcasted_iota(jnp.int32, (1, tk), 1) + c0
            sc_ref[:, pl.ds(c0, tk)] = jnp.where(jnp.logical_and(x == tau, col > sigma), NEG_INF, x)
            return carry

        lax.fori_loop(0, nch, rewrite, 0)

    return tau


def _dsa_prompt_kernel(qi_ref, g_ref, q_ref, ki_ref, k_ref, v_ref, o_ref, sc_ref, tau_scr, m_scr, l_scr, acc_scr,
                       *, tq, tk, topk):
    i = pl.program_id(1)
    smax = sc_ref.shape[1]
    nch = ((i + 1) * tq + tk - 1) // tk
    row = lax.broadcasted_iota(jnp.int32, (tq, 1), 0) + i * tq

    def score_chunk(c, carry):
        c0 = pl.multiple_of(c * tk, tk)
        kblk = ki_ref[pl.ds(c0, tk), :]
        acc = jnp.zeros((tq, tk), F32)
        for h in range(IDX_HEADS):
            x = _dot_nt(qi_ref[:, h * LANES:(h + 1) * LANES], kblk)
            acc = acc + jnp.maximum(x, 0.0) * g_ref[:, L_WI + h:L_WI + h + 1]
        col = lax.broadcasted_iota(jnp.int32, (1, tk), 1) + c0
        sc_ref[:, pl.ds(c0, tk)] = jnp.where(col <= row, acc, NEG_INF)
        return carry

    lax.fori_loop(0, nch, score_chunk, 0)

    kr = jnp.minimum(row + 1, topk).astype(F32)
    tau = _select_threshold(sc_ref, nch * (tk // SELECT_TK), SELECT_TK, kr, smax)

    nslab = tk // LANES
    tau_scr[...] = jnp.broadcast_to(tau, (tq, LANES))

    def masked_logits(c0, h):
        pr = slice((h // 2) * LANES, (h // 2 + 1) * LANES)
        s = _dot_nt(q_ref[:, h * LANES:(h + 1) * LANES], k_ref[pl.ds(c0, tk), pr])
        x = sc_ref[:, pl.ds(c0, tk)]
        thr = tau_scr[...]
        return [jnp.where(x[:, j * LANES:(j + 1) * LANES] >= thr, s[:, j * LANES:(j + 1) * LANES], MASKED)
                for j in range(nslab)]

    m_scr[...] = jnp.full(m_scr.shape, MASKED, F32)

    def max_chunk(c, carry):
        c0 = pl.multiple_of(c * tk, tk)
        for h in range(A_HEADS):
            slabs = masked_logits(c0, h)
            part = slabs[0]
            for j in range(1, nslab):
                part = jnp.maximum(part, slabs[j])
            m_scr[h] = jnp.maximum(m_scr[h], part)
        return carry

    lax.fori_loop(0, nch, max_chunk, 0)
    for h in range(A_HEADS):
        m_scr[h] = jnp.broadcast_to(jnp.max(m_scr[h], axis=1, keepdims=True), (tq, LANES))

    l_scr[...] = jnp.zeros(l_scr.shape, F32)
    acc_scr[...] = jnp.zeros(acc_scr.shape, F32)

    def attend_chunk(c, carry):
        c0 = pl.multiple_of(c * tk, tk)
        for h in range(A_HEADS):
            pr = slice((h // 2) * LANES, (h // 2 + 1) * LANES)
            m_h = m_scr[h]
            p = [jnp.exp(sl - m_h) for sl in masked_logits(c0, h)]
            psum = p[0]
            for j in range(1, nslab):
                psum = psum + p[j]
            l_scr[h] = l_scr[h] + psum
            pb = jnp.concatenate([pj.astype(BF16) for pj in p], axis=1)
            acc_scr[h] = acc_scr[h] + _dot(pb, v_ref[pl.ds(c0, tk), pr])
        return carry

    lax.fori_loop(0, nch, attend_chunk, 0)

    lo_half = lax.broadcasted_iota(jnp.int32, (1, LANES), 1) < A_HEAD_DIM
    for p in range(A_HEADS // 2):
        even = acc_scr[2 * p] / jnp.sum(l_scr[2 * p], axis=1, keepdims=True)
        odd = acc_scr[2 * p + 1] / jnp.sum(l_scr[2 * p + 1], axis=1, keepdims=True)
        o_ref[:, p * LANES:(p + 1) * LANES] = jnp.where(lo_half, even, odd).astype(BF16)


def _dsa_prompt(qipad, gates, qpad, ki2, kb, vb, nb, seq, tq, tk):
    nq = seq // tq
    topk = min(TOPK_MAX, seq // 4)
    qrow = lambda w: pl.BlockSpec((tq, w), lambda b, i: (b * nq + i, 0))
    kfull = lambda w: pl.BlockSpec((seq, w), lambda b, i: (b, 0), pipeline_mode=pl.Buffered(1))
    return pl.pallas_call(
        functools.partial(_dsa_prompt_kernel, tq=tq, tk=tk, topk=topk),
        grid=(nb, nq),
        in_specs=[qrow(2 * A_WIDTH), qrow(LANES), qrow(2 * A_WIDTH), kfull(LANES), kfull(A_WIDTH), kfull(A_WIDTH)],
        out_specs=qrow(A_WIDTH),
        out_shape=jax.ShapeDtypeStruct((nb * seq, A_WIDTH), BF16),
        scratch_shapes=[pltpu.VMEM((tq, seq), F32),
                        pltpu.VMEM((tq, LANES), F32),
                        pltpu.VMEM((A_HEADS, tq, LANES), F32),
                        pltpu.VMEM((A_HEADS, tq, LANES), F32),
                        pltpu.VMEM((A_HEADS, tq, LANES), F32)],
        compiler_params=_cparams(("parallel", "arbitrary")),
        name="dsa_prompt",
    )(qipad, gates, qpad, ki2, kb, vb)


def _dsa_sample_scores_kernel(pt_ref, q_ref, w_ref, *refs, n_pages, t_real):
    page_refs, new_ref, o_ref = refs[:n_pages], refs[n_pages], refs[n_pages + 1]
    q = q_ref[0]
    w = w_ref[0]
    past = n_pages * PAGE_SIZE
    trow = lax.broadcasted_iota(jnp.int32, (t_real, 1), 0)
    for p in range(n_pages + 1):
        keys = (page_refs[p][0].astype(BF16) if p < n_pages else new_ref[0])
        r = jnp.maximum(_dot_nt(q, keys), 0.0) * w
        sc = jnp.sum(r.reshape(t_real, IDX_HEADS, PAGE_SIZE), axis=1)
        if p == n_pages:
            col = lax.broadcasted_iota(jnp.int32, (1, PAGE_SIZE), 1)
            sc = jnp.where(col <= trow, sc, NEG_INF)
        o_ref[0, :, p * PAGE_SIZE:(p + 1) * PAGE_SIZE] = sc


def _dsa_sample_scores(page_table, qi, wi, cache_kidx, ki_new, t_real):
    nseq, n_pages = page_table.shape
    ncol = (n_pages + 1) * PAGE_SIZE
    per_seq = lambda a: pl.BlockSpec((1,) + a.shape[1:], lambda b, pt: (b, 0, 0))
    page = lambda p: pl.BlockSpec((1, PAGE_SIZE, IDX_DIM), lambda b, pt, p=p: (pt[b * n_pages + p], 0, 0))
    return pl.pallas_call(
        functools.partial(_dsa_sample_scores_kernel, n_pages=n_pages, t_real=t_real),
        grid_spec=pltpu.PrefetchScalarGridSpec(
            num_scalar_prefetch=1,
            grid=(nseq,),
            in_specs=[per_seq(qi), per_seq(wi)] + [page(p) for p in range(n_pages)] + [per_seq(ki_new)],
            out_specs=pl.BlockSpec((1, t_real, ncol), lambda b, pt: (b, 0, 0)),
        ),
        out_shape=jax.ShapeDtypeStruct((nseq, t_real, ncol), F32),
        compiler_params=_cparams(("parallel",)),
        name="dsa_sample_scores",
    )(page_table.reshape(-1), qi, wi, *([cache_kidx] * n_pages), ki_new)


def _dsa_sample_select_kernel(sc_ref, adj_ref, tau_ref, *, tk, topk):
    adj_ref[...] = sc_ref[...]
    rows, ncol = adj_ref.shape
    kr = jnp.full((rows, 1), float(topk), F32)
    tau = _select_threshold(adj_ref, ncol // tk, tk, kr, ncol)
    tau_ref[...] = jnp.broadcast_to(tau, (rows, LANES))


def _dsa_sample_select(sc, topk, rblk):
    rows, ncol = sc.shape
    return pl.pallas_call(
        functools.partial(_dsa_sample_select_kernel, tk=LANES, topk=topk),
        grid=(rows // rblk,),
        in_specs=[pl.BlockSpec((rblk, ncol), lambda i: (i, 0))],
        out_specs=[pl.BlockSpec((rblk, ncol), lambda i: (i, 0)), pl.BlockSpec((rblk, LANES), lambda i: (i, 0))],
        out_shape=[jax.ShapeDtypeStruct((rows, ncol), F32), jax.ShapeDtypeStruct((rows, LANES), F32)],
        compiler_params=_cparams(("parallel",)),
        name="dsa_sample_select",
    )(sc)


NEW_ROWS = PAGE_SIZE
PAGE_ROWS = PAGE_SIZE * A_HEADS


def _dsa_sample_attend_kernel(pt_ref, q_ref, sc_ref, tau_ref, ex_ref, *refs, n_pages, t_real):
    kp, vp = refs[:n_pages], refs[n_pages:2 * n_pages]
    knew_ref, vnew_ref, o_ref, s_scr = refs[2 * n_pages:]
    q = q_ref[0]
    nrow = t_real * A_HEADS
    tau = tau_ref[0][:, 0:1]
    ex = ex_ref[...]
    head_match = (lax.broadcasted_iota(jnp.int32, (nrow, PAGE_ROWS), 0) % A_HEADS
                  == lax.broadcasted_iota(jnp.int32, (nrow, PAGE_ROWS), 1) % A_HEADS)

    def expand_sel(p, width):
        sel = jnp.where(sc_ref[0, :, p * PAGE_SIZE:(p + 1) * PAGE_SIZE] >= tau, 1.0, 0.0)
        sel8 = jnp.concatenate([sel, jnp.zeros((SUBLANES - t_real, PAGE_SIZE), F32)], axis=0).astype(BF16)
        e = _dot(sel8, ex[:, 0:width])[0:t_real]
        e = jnp.broadcast_to(e[:, None, :], (t_real, A_HEADS, width)).reshape(nrow, width)
        return jnp.logical_and(e > 0.5, head_match[:, 0:width])

    for p in range(n_pages):
        k2 = kp[p][0].reshape(PAGE_ROWS, A_HEAD_DIM).astype(BF16)
        s_scr[:, p * PAGE_ROWS:(p + 1) * PAGE_ROWS] = jnp.where(expand_sel(p, PAGE_ROWS), _dot_nt(q, k2), MASKED)
    c_new = n_pages * PAGE_ROWS
    s_scr[:, c_new:c_new + NEW_ROWS] = jnp.where(expand_sel(n_pages, NEW_ROWS), _dot_nt(q, knew_ref[0]), MASKED)

    s_all = s_scr[...]
    m = jnp.max(s_all, axis=1, keepdims=True)
    pr = jnp.exp(s_all - m)
    l = jnp.sum(pr, axis=1, keepdims=True)
    s_scr[...] = pr
    acc = _dot(s_scr[:, c_new:c_new + NEW_ROWS].astype(BF16), vnew_ref[0])
    for p in range(n_pages):
        v2 = vp[p][0].reshape(PAGE_ROWS, A_HEAD_DIM).astype(BF16)
        acc = acc + _dot(s_scr[:, p * PAGE_ROWS:(p + 1) * PAGE_ROWS].astype(BF16), v2)
    o_ref[0] = (acc / l).astype(BF16)


def _dsa_sample_attend(page_table, q, sc_adj, tau, cache_k, cache_v, k_new, v_new, t_real):
    nseq, n_pages = page_table.shape
    nrow = t_real * A_HEADS
    expand = (jnp.arange(PAGE_ROWS)[None, :] // A_HEADS == jnp.arange(PAGE_SIZE)[:, None]).astype(BF16)
    per_seq = lambda a: pl.BlockSpec((1,) + a.shape[1:], lambda b, pt: (b, 0, 0))
    page = lambda p: pl.BlockSpec((1, PAGE_SIZE, A_HEADS, A_HEAD_DIM), lambda b, pt, p=p: (pt[b * n_pages + p], 0, 0, 0))
    pages = lambda: [page(p) for p in range(n_pages)]
    return pl.pallas_call(
        functools.partial(_dsa_sample_attend_kernel, n_pages=n_pages, t_real=t_real),
        grid_spec=pltpu.PrefetchScalarGridSpec(
            num_scalar_prefetch=1,
            grid=(nseq,),
            in_specs=[per_seq(q), per_seq(sc_adj), per_seq(tau), pl.BlockSpec(expand.shape, lambda b, pt: (0, 0))]
            + pages() + pages() + [per_seq(k_new), per_seq(v_new)],
            out_specs=pl.BlockSpec((1, nrow, A_HEAD_DIM), lambda b, pt: (b, 0, 0)),
            scratch_shapes=[pltpu.VMEM((nrow, n_pages * PAGE_ROWS + NEW_ROWS), F32)],
        ),
        out_shape=jax.ShapeDtypeStruct((nseq, nrow, A_HEAD_DIM), BF16),
        compiler_params=_cparams(("parallel",)),
        name="dsa_sample_attend",
    )(page_table.reshape(-1), q, sc_adj, tau, expand, *([cache_k] * n_pages), *([cache_v] * n_pages), k_new, v_new)


PROJ_TM = 256
OUT_TM = 256
MLSTM_CHUNK = 128
DSA_TQ = 256
DSA_TK = 512
SAMPLE_SEQ_BLK = 8
SELECT_ROWS = 128


def _unpad_heads(xpad, n_heads):
    x = xpad.reshape(xpad.shape[:-1] + (n_heads, 2, LANES // 2))
    return x[..., 0, :] + x[..., 1, :]


def kernel(x_prompt, x_sample, cache_k, cache_v, cache_kidx, state_C, state_n, state_m, page_table,
           w_in, b_gate, w_out, ln1_g, ln1_b, w_gate, w_up, w_down, ln2_g, ln2_b):
    bp, sp, _ = x_prompt.shape
    bs, ts, _ = x_sample.shape
    n_pages = page_table.shape[1]
    past = n_pages * PAGE_SIZE
    n_pool = cache_k.shape[1]
    assert DEPTH == 1 and w_in.shape[0] == 1

    wp, wgt, bias, biast = _prep_w_in(w_in[0], b_gate[0])
    wo, wg, wu, wd = (w[0].astype(BF16) for w in (w_out, w_gate, w_up, w_down))
    g1, b1, g2, b2 = (v[0].astype(F32)[None, :] for v in (ln1_g, ln1_b, ln2_g, ln2_b))

    mp = bp * sp
    xp = x_prompt.reshape(mp, D_MODEL)
    tabs_p = _rope_tables(jnp.arange(sp, dtype=jnp.int32))
    (mqkv, so, gates, gatest, qpad, k_p, kb, v_p, vb, qipad, ki_p, ki2) = _proj(
        xp, wp, wgt, bias, biast, tabs_p, PROJ_TM, sp // PROJ_TM)
    hm_p, caug, m_p = _mlstm_prompt(mqkv.reshape(bp, sp, -1), so.reshape(bp, sp, -1), gates.reshape(bp, sp, -1),
                                    gatest, bp, sp, MLSTM_CHUNK)
    ha_p = _dsa_prompt(qipad, gates, qpad, ki2, kb, vb, bp, sp, DSA_TQ, DSA_TK)
    y_p = _block_out(xp, hm_p.reshape(mp, -1), ha_p, wo, g1, b1, wg, wu, wd, g2, b2, OUT_TM)

    ms = bs * ts
    xs = x_sample.reshape(ms, D_MODEL)
    tabs_s = _rope_tables(jnp.tile(past + jnp.arange(ts, dtype=jnp.int32), bs))
    (mqkv_s, so_s, gates_s, _, qpad_s, k_s, kb_s, v_s, vb_s, qipad_s, ki_s, _) = _proj(
        xs, wp, wgt, bias, biast, tabs_s, ms, 1)
    pad_rows = lambda a: jnp.pad(a.reshape(bs, ts, -1), ((0, 0), (0, ROWS_PAD - ts), (0, 0))).reshape(bs * ROWS_PAD, -1)
    hm_s, c_s, n_s, m_s = _mlstm_sample(pad_rows(mqkv_s), pad_rows(so_s), pad_rows(gates_s),
                                        state_C[0].astype(F32), state_n[0].astype(F32), state_m[0].astype(F32),
                                        SAMPLE_SEQ_BLK, ts)
    hm_s = hm_s.reshape(bs, ROWS_PAD, -1)[:, :ts].reshape(ms, -1)

    qi_s = _unpad_heads(qipad_s, IDX_HEADS).reshape(bs, ts * IDX_HEADS, IDX_DIM)
    wi_s = gates_s[:, L_WI:L_WI + IDX_HEADS].reshape(bs, ts * IDX_HEADS, 1)
    pad_page = lambda a: jnp.pad(a.reshape(bs, ts, -1), ((0, 0), (0, PAGE_SIZE - ts), (0, 0)))
    sc_s = _dsa_sample_scores(page_table, qi_s, wi_s, cache_kidx[0], pad_page(ki_s).astype(BF16), ts)
    ncol = sc_s.shape[-1]
    topk_s = min(TOPK_MAX, (past + ts) // 4)
    sc_adj, tau_s = _dsa_sample_select(sc_s.reshape(ms, ncol), topk_s, SELECT_ROWS)
    q_s = _unpad_heads(qpad_s, A_HEADS).reshape(bs, ts * A_HEADS, A_HEAD_DIM)
    new_rows = lambda a: jnp.pad(a.reshape(bs, ts * A_HEADS, A_HEAD_DIM), ((0, 0), (0, NEW_ROWS - ts * A_HEADS), (0, 0)))
    ha_s = _dsa_sample_attend(page_table, q_s, sc_adj.reshape(bs, ts, ncol), tau_s.reshape(bs, ts, LANES),
                              cache_k[0], cache_v[0], new_rows(kb_s), new_rows(vb_s), ts)
    y_s = _block_out(xs, hm_s, ha_s.reshape(ms, -1), wo, g1, b1, wg, wu, wd, g2, b2, min(OUT_TM, ms))

    heads = lambda a, b, t: a.reshape(1, b, t, A_HEADS, A_HEAD_DIM)
    return (y_p.reshape(bp, sp, D_MODEL), y_s.reshape(bs, ts, D_MODEL),
            heads(k_p, bp, sp), heads(v_p, bp, sp), ki_p.reshape(1, bp, sp, IDX_DIM),
            caug[None, :, :, :, 0:M_HEAD_DIM], caug[None, :, :, :, M_HEAD_DIM], m_p[:, 0].reshape(1, bp, M_HEADS),
            heads(k_s, bs, ts), heads(v_s, bs, ts), ki_s.reshape(1, bs, ts, IDX_DIM),
            c_s[None], n_s[None], m_s[None])
```

```python
import functools
import math

import jax
import jax.numpy as jnp
from jax import lax
from jax.experimental import pallas as pl
from jax.experimental.pallas import tpu as pltpu

D_MODEL = 1024
M_HEADS = 4
M_HEAD_DIM = 128
M_WIDTH = 512
A_HEADS = 8
A_HEAD_DIM = 64
A_WIDTH = 512
IDX_HEADS = 8
IDX_DIM = 64
TOPK_MAX = 256
PAGE_SIZE = 128
ROPE_THETA = 500000.0
ROT = A_HEAD_DIM // 4
HALF = ROT // 2
D_FF = 2816
DEPTH = 1
ALPHA = (2 * DEPTH) ** 0.25
LN_EPS = 1e-5
LOG2E = math.log2(math.e)

LANES = 128
SUBLANES = 8
VMEM_LIMIT = 56 * 1024 * 1024

NEG_INF = float("-inf")
BF16 = jnp.bfloat16
F32 = jnp.float32

C_MQ, C_MK, C_MV, C_MO = 0, 512, 1024, 1536
C_AQ, C_IQ, C_SM = 2048, 2560, 3072
N_PERM = 3200
L_WI, L_IG, L_FG = 64, 72, 76
R_AK, R_AV, R_IK, N_ROWS_T = 0, 512, 1024, 1152


def _cparams(sem):
    return pltpu.CompilerParams(dimension_semantics=sem, vmem_limit_bytes=VMEM_LIMIT)


def _dot(a, b):
    return jnp.dot(a, b, preferred_element_type=F32)


def _dot_nt(a, b):
    return lax.dot_general(a, b, (((1,), (1,)), ((), ())), preferred_element_type=F32)


def _dot_tn(a, b):
    return lax.dot_general(a, b, (((0,), (0,)), ((), ())), preferred_element_type=F32)


def _rope128(x, cos, sa, sb):
    return x * cos + pltpu.roll(x, LANES - HALF, 1) * sa + pltpu.roll(x, HALF, 1) * sb


def _rope_rows(z, cos_t, sin_t):
    a, b = z[0:HALF], z[HALF:ROT]
    return jnp.concatenate([a * cos_t - b * sin_t, b * cos_t + a * sin_t, z[ROT:]], axis=0)


def _proj_kernel(x_ref, w_ref, wt_ref, wgt_ref, bias_ref, biast_ref, cos_ref, sa_ref, sb_ref, cost_ref, sint_ref,
                 mqkv_ref, so_ref, gates_ref, gatest_ref, qpad_ref, qipad_ref,
                 kt_ref, ktb_ref, vt_ref, vtb_ref, kit_ref, kit2_ref):
    xb = x_ref[...].astype(BF16)
    cos, sa, sb = cos_ref[...], sa_ref[...], sb_ref[...]
    cos_t, sin_t = cost_ref[...], sint_ref[...]
    lane = lax.broadcasted_iota(jnp.int32, (1, LANES), 1)
    lo_half = lane < A_HEAD_DIM

    zm = _dot(xb, w_ref[:, C_MQ:C_MO])
    mqkv_ref[:, 0:M_WIDTH] = zm[:, 0:M_WIDTH].astype(BF16)
    mqkv_ref[:, M_WIDTH:2 * M_WIDTH] = (zm[:, M_WIDTH:2 * M_WIDTH] * (M_HEAD_DIM ** -0.5)).astype(BF16)
    mqkv_ref[:, 2 * M_WIDTH:3 * M_WIDTH] = zm[:, 2 * M_WIDTH:3 * M_WIDTH].astype(BF16)
    so_ref[...] = jax.nn.sigmoid(_dot(xb, w_ref[:, C_MO:C_AQ]))

    def padded_heads(z, scale, out_ref):
        for p in range(A_WIDTH // LANES):
            r = _rope128(z[:, p * LANES:(p + 1) * LANES], cos, sa, sb) * scale
            out_ref[:, (2 * p) * LANES:(2 * p + 1) * LANES] = jnp.where(lo_half, r, 0.0).astype(BF16)
            out_ref[:, (2 * p + 1) * LANES:(2 * p + 2) * LANES] = jnp.where(lo_half, 0.0, r).astype(BF16)

    padded_heads(_dot(xb, w_ref[:, C_AQ:C_IQ]), (A_HEAD_DIM ** -0.5) * LOG2E, qpad_ref)
    padded_heads(_dot(xb, w_ref[:, C_IQ:C_SM]), IDX_DIM ** -0.5, qipad_ref)

    zs = _dot(xb, w_ref[:, C_SM:N_PERM]) + bias_ref[...]
    is_wi = (lane >= L_WI) & (lane < L_IG)
    is_fg = (lane >= L_FG) & (lane < L_FG + M_HEADS)
    g = jnp.where(is_wi, zs * (IDX_HEADS ** -0.5), zs)
    gates_ref[...] = jnp.where(is_fg, jax.nn.log_sigmoid(zs), g)

    zt = _dot_nt(wgt_ref[...], xb) + biast_ref[...]
    row = lax.broadcasted_iota(jnp.int32, (2 * M_HEADS, 1), 0)
    gatest_ref[...] = jnp.where(row >= M_HEADS, jax.nn.log_sigmoid(zt), zt)

    zkt = _dot_nt(wt_ref[R_AK:R_AV, :], xb)
    for h in range(A_HEADS):
        r = _rope_rows(zkt[h * A_HEAD_DIM:(h + 1) * A_HEAD_DIM], cos_t, sin_t)
        kt_ref[h * A_HEAD_DIM:(h + 1) * A_HEAD_DIM, :] = r
        ktb_ref[h * A_HEAD_DIM:(h + 1) * A_HEAD_DIM, :] = r.astype(BF16)
    zvt = _dot_nt(wt_ref[R_AV:R_IK, :], xb)
    vt_ref[...] = zvt
    vtb_ref[...] = zvt.astype(BF16)
    zit = _dot_nt(wt_ref[R_IK:N_ROWS_T, :], xb)
    r = _rope_rows(zit[0:IDX_DIM], cos_t, sin_t)
    kit_ref[...] = r
    kit2_ref[...] = jnp.concatenate([r, r], axis=0).astype(BF16)


def _rope_tables(pos):
    inv = ROPE_THETA ** (-jnp.arange(HALF, dtype=F32) / HALF)
    ang = pos.astype(F32)[:, None] * inv[None, :]
    c, s = jnp.cos(ang), jnp.sin(ang)
    n = pos.shape[0]
    one = jnp.ones((n, A_HEAD_DIM - ROT), F32)
    zero = jnp.zeros((n, A_HEAD_DIM - ROT), F32)
    z8 = jnp.zeros((n, HALF), F32)
    cos64 = jnp.concatenate([c, c, one], axis=1)
    sa64 = jnp.concatenate([-s, z8, zero], axis=1)
    sb64 = jnp.concatenate([z8, s, zero], axis=1)
    two = lambda t: jnp.concatenate([t, t], axis=1)
    return (two(cos64), two(sa64), two(sb64)), (c.T, s.T)


def _proj(x, w, tabs, tabs_t, tm, nb, seq):
    wp, wt, wgt, bias, biast = w
    m = x.shape[0]
    nblk = seq // tm
    row = lambda wd: pl.BlockSpec((tm, wd), lambda i: (i, 0))
    full = lambda a: pl.BlockSpec(a.shape, lambda i: (0, 0))
    tab = pl.BlockSpec((tm, LANES), lambda i: (i % nblk, 0))
    tab_t = pl.BlockSpec((HALF, tm), lambda i: (0, i % nblk))
    fmaj = lambda r: pl.BlockSpec((None, r, tm), lambda i: (i // nblk, 0, i % nblk))
    outs = [
        ((m, 3 * M_WIDTH), BF16, row(3 * M_WIDTH)),
        ((m, M_WIDTH), F32, row(M_WIDTH)),
        ((m, LANES), F32, row(LANES)),
        ((2 * M_HEADS, m), F32, pl.BlockSpec((2 * M_HEADS, tm), lambda i: (0, i))),
        ((m, 2 * A_WIDTH), BF16, row(2 * A_WIDTH)),
        ((m, 2 * A_WIDTH), BF16, row(2 * A_WIDTH)),
        ((nb, A_WIDTH, seq), F32, fmaj(A_WIDTH)),
        ((nb, A_WIDTH, seq), BF16, fmaj(A_WIDTH)),
        ((nb, A_WIDTH, seq), F32, fmaj(A_WIDTH)),
        ((nb, A_WIDTH, seq), BF16, fmaj(A_WIDTH)),
        ((nb, IDX_DIM, seq), F32, fmaj(IDX_DIM)),
        ((nb, 2 * IDX_DIM, seq), BF16, fmaj(2 * IDX_DIM)),
    ]
    return pl.pallas_call(
        _proj_kernel,
        grid=(m // tm,),
        in_specs=[row(D_MODEL), full(wp), full(wt), full(wgt), full(bias), full(biast), tab, tab, tab, tab_t, tab_t],
        out_specs=[o[2] for o in outs],
        out_shape=[jax.ShapeDtypeStruct(o[0], o[1]) for o in outs],
        compiler_params=_cparams(("parallel",)),
        name="proj",
    )(x, wp, wt, wgt, bias, biast, *tabs, *tabs_t)


def _prep_w_in(w_in, b_gate):
    s = [0, 512, 1024, 1536, 2048, 2052, 2056, 2568, 3080, 3592, 4104, 4168, 4176]
    mq, mk, mv, mo, mi, mf, aq, ak, av, iq, ik, iw = [w_in[:, s[i]:s[i + 1]] for i in range(12)]
    z64 = jnp.zeros((D_MODEL, L_WI), w_in.dtype)
    pad = jnp.zeros((D_MODEL, N_PERM - C_SM - L_FG - M_HEADS), w_in.dtype)
    wp = jnp.concatenate([mq, mk, mv, mo, aq, iq, z64, iw, mi, mf, pad], axis=1).astype(BF16)
    wt = jnp.concatenate([ak, av, ik, ik], axis=1).T.astype(BF16)
    wgt = jnp.concatenate([mi, mf], axis=1).T.astype(BF16)
    bg = b_gate.astype(F32)
    bias = jnp.zeros((1, LANES), F32).at[0, L_IG:L_IG + 2 * M_HEADS].set(bg)
    return wp, wt, wgt, bias, bg[:, None]


FF_CHUNK = D_FF // 2


def _layer_norm(x, g, b):
    mu = jnp.mean(x, axis=-1, keepdims=True)
    xc = x - mu
    var = jnp.mean(xc * xc, axis=-1, keepdims=True)
    return xc * lax.rsqrt(var + LN_EPS) * g + b


def _block_out_kernel(x_ref, hm_ref, ha_ref, wo_ref, g1_ref, b1_ref, wg_ref, wu_ref, wd_ref, g2_ref, b2_ref, y_ref):
    mix = _dot(hm_ref[...], wo_ref[0:M_WIDTH, :]) + _dot(ha_ref[...], wo_ref[M_WIDTH:2 * M_WIDTH, :])
    x1 = _layer_norm(ALPHA * x_ref[...] + mix, g1_ref[...], b1_ref[...])
    xb = x1.astype(BF16)
    ff = jnp.zeros_like(x1)
    for c in range(D_FF // FF_CHUNK):
        sl = slice(c * FF_CHUNK, (c + 1) * FF_CHUNK)
        act = jax.nn.silu(_dot(xb, wg_ref[:, sl])) * _dot(xb, wu_ref[:, sl])
        ff = ff + _dot(act.astype(BF16), wd_ref[sl, :])
    y_ref[...] = _layer_norm(ALPHA * x1 + ff, g2_ref[...], b2_ref[...])


def _block_out(x, hm, ha, wo, g1, b1, wg, wu, wd, g2, b2, tm):
    m = x.shape[0]
    row = lambda w: pl.BlockSpec((tm, w), lambda i: (i, 0))
    full = lambda a: pl.BlockSpec(a.shape, lambda i: (0, 0), pipeline_mode=pl.Buffered(1))
    return pl.pallas_call(
        _block_out_kernel,
        grid=(m // tm,),
        in_specs=[row(D_MODEL), row(M_WIDTH), row(A_WIDTH), full(wo), full(g1), full(b1),
                  full(wg), full(wu), full(wd), full(g2), full(b2)],
        out_specs=row(D_MODEL),
        out_shape=jax.ShapeDtypeStruct((m, D_MODEL), F32),
        compiler_params=_cparams(("parallel",)),
        name="block_out",
    )(x, hm, ha, wo, g1, b1, wg, wu, wd, g2, b2)


def _split3(x):
    h1 = x.astype(BF16)
    r1 = x - h1.astype(F32)
    h2 = r1.astype(BF16)
    h3 = (r1 - h2.astype(F32)).astype(BF16)
    return h1, h2, h3


def _dot3(x, w):
    h1, h2, h3 = _split3(x)
    return _dot(h1, w) + _dot(h2, w) + _dot(h3, w)


def _mlstm_chunk(q, k, v, lf_row, i_row, i_col, c_aug, m_prev):
    L = q.shape[0]
    t_idx = lax.broadcasted_iota(jnp.int32, (L, L), 0)
    s_idx = lax.broadcasted_iota(jnp.int32, (L, L), 1)
    causal = s_idx <= t_idx
    ones_b = jnp.ones((L, LANES), BF16)
    G = _dot3(jnp.where(causal, lf_row, 0.0), ones_b)
    tri_u = jnp.where(t_idx <= s_idx, 1.0, 0.0).astype(BF16)
    f_row = _dot3(jnp.broadcast_to(lf_row, (SUBLANES, L)), tri_u)[0:1, :]
    f_col = G[:, 0:1]
    dlog = jnp.where(causal, f_col - f_row + i_row, NEG_INF)
    inter = m_prev + f_col
    m_t = jnp.maximum(jnp.max(dlog, axis=1, keepdims=True), inter)
    w = jnp.exp(dlog - m_t)
    g = jnp.exp(inter - m_t)
    s = _dot_nt(q, k) * w
    lane = lax.broadcasted_iota(jnp.int32, (L, LANES), 1)
    v_aug = jnp.concatenate([v, jnp.where(lane == 0, 1.0, 0.0).astype(BF16)], axis=1)
    nd = _dot(s.astype(BF16), v_aug) + g * _dot(q, c_aug.astype(BF16))
    den = nd[:, M_HEAD_DIM:M_HEAD_DIM + 1]
    h = nd[:, 0:M_HEAD_DIM] / jnp.maximum(jnp.abs(den), jnp.exp(-m_t))
    m_new = m_t[L - 1:L, :]
    g_c = g[L - 1:L, :]
    wk = jnp.exp(f_col[L - 1:L, :] - f_col + i_col - m_new)
    kw = (k.astype(F32) * wk).astype(BF16)
    c_new = g_c * c_aug + _dot_tn(kw, v_aug)
    return h, c_new, m_new


def _mlstm_prompt_kernel(*refs, nb):
    mqkv_ref, so_ref, g_ref = refs[0:3]
    gt_refs = refs[3:3 + nb]
    hm_ref, c_ref, m_ref = refs[3 + nb:]
    step = pl.program_id(0)

    @pl.when(step == 0)
    def _():
        c_ref[...] = jnp.zeros_like(c_ref)
        m_ref[...] = jnp.zeros_like(m_ref)

    for b in range(nb):
        for h in range(M_HEADS):
            hs = slice(h * M_HEAD_DIM, (h + 1) * M_HEAD_DIM)
            q = mqkv_ref[b, :, hs]
            k = mqkv_ref[b, :, M_WIDTH + h * M_HEAD_DIM:M_WIDTH + (h + 1) * M_HEAD_DIM]
            v = mqkv_ref[b, :, 2 * M_WIDTH + h * M_HEAD_DIM:2 * M_WIDTH + (h + 1) * M_HEAD_DIM]
            i_row = gt_refs[b][h:h + 1, :]
            lf_row = gt_refs[b][M_HEADS + h:M_HEADS + h + 1, :]
            i_col = g_ref[b, :, L_IG + h:L_IG + h + 1]
            r = b * M_HEADS + h
            m_prev = m_ref[r:r + 1, 0:1]
            hh, c_new, m_new = _mlstm_chunk(q, k, v, lf_row, i_row, i_col, c_ref[b, h], m_prev)
            c_ref[b, h] = c_new
            m_ref[r:r + 1, :] = jnp.broadcast_to(m_new, (1, LANES))
            hm_ref[b, :, hs] = (hh * so_ref[b, :, hs]).astype(BF16)


def _mlstm_prompt(mqkv, so, gates, gatest, nb, seq, chunk):
    nc = seq // chunk
    blk = lambda w: pl.BlockSpec((nb, chunk, w), lambda c: (0, c, 0))
    gt_specs = [pl.BlockSpec((2 * M_HEADS, chunk), lambda c, b=b: (0, b * nc + c)) for b in range(nb)]
    return pl.pallas_call(
        functools.partial(_mlstm_prompt_kernel, nb=nb),
        grid=(nc,),
        in_specs=[blk(3 * M_WIDTH), blk(M_WIDTH), blk(LANES)] + gt_specs,
        out_specs=[blk(M_WIDTH),
                   pl.BlockSpec((nb, M_HEADS, M_HEAD_DIM, 2 * M_HEAD_DIM), lambda c: (0, 0, 0, 0)),
                   pl.BlockSpec((nb * M_HEADS, LANES), lambda c: (0, 0))],
        out_shape=[jax.ShapeDtypeStruct((nb, seq, M_WIDTH), BF16),
                   jax.ShapeDtypeStruct((nb, M_HEADS, M_HEAD_DIM, 2 * M_HEAD_DIM), F32),
                   jax.ShapeDtypeStruct((nb * M_HEADS, LANES), F32)],
        compiler_params=_cparams(("arbitrary",)),
        name="mlstm_prompt",
    )(mqkv, so, gates, *([gatest] * nb))


ROWS_PAD = SUBLANES


def _mlstm_sample_kernel(mqkv_ref, so_ref, g_ref, c0_ref, n0_ref, m0_ref, hm_ref, c_ref, n_ref, m_ref, *, nseq, t_real):
    row = lax.broadcasted_iota(jnp.int32, (ROWS_PAD, 1), 0)
    real = row < t_real

    def per_seq(s, carry):
        r0 = pl.multiple_of(s * ROWS_PAD, ROWS_PAD)
        gt = g_ref[pl.ds(r0, ROWS_PAD), :]
        cum = gt
        for d in range(1, t_real):
            cum = cum + jnp.where(row >= d, pltpu.roll(gt, d, 0), 0.0)
        for h in range(M_HEADS):
            hs = slice(h * M_HEAD_DIM, (h + 1) * M_HEAD_DIM)
            q = mqkv_ref[pl.ds(r0, ROWS_PAD), hs]
            k = mqkv_ref[pl.ds(r0, ROWS_PAD), M_WIDTH + h * M_HEAD_DIM:M_WIDTH + (h + 1) * M_HEAD_DIM]
            v = mqkv_ref[pl.ds(r0, ROWS_PAD), 2 * M_WIDTH + h * M_HEAD_DIM:2 * M_WIDTH + (h + 1) * M_HEAD_DIM]
            qf, kf, vf = q.astype(F32), k.astype(F32), v.astype(F32)
            i_col = gt[:, L_IG + h:L_IG + h + 1]
            f_col = cum[:, L_FG + h:L_FG + h + 1]
            c0 = c0_ref[s, h]
            n0 = n0_ref[s, h:h + 1, :]
            m0 = m0_ref[pl.ds(s, 1), h:h + 1]
            inter = m0 + f_col
            dl = [jnp.where(real & (row >= u), f_col - f_col[u:u + 1, :] + i_col[u:u + 1, :], NEG_INF)
                  for u in range(t_real)]
            m_t = inter
            for u in range(t_real):
                m_t = jnp.maximum(m_t, dl[u])
            g = jnp.exp(inter - m_t)
            qc = _dot(q, c0.astype(BF16))
            num = g * qc
            den = g * jnp.sum(qf * n0, axis=1, keepdims=True)
            for u in range(t_real):
                su = jnp.sum(qf * kf[u:u + 1, :], axis=1, keepdims=True) * jnp.exp(dl[u] - m_t)
                num = num + su * vf[u:u + 1, :]
                den = den + su
            hh = num / jnp.maximum(jnp.abs(den), jnp.exp(-m_t))
            hm_ref[pl.ds(r0, ROWS_PAD), hs] = (hh * so_ref[pl.ds(r0, ROWS_PAD), hs]).astype(BF16)
            last = t_real - 1
            m_new = m_t[last:last + 1, :]
            g_c = g[last:last + 1, :]
            wk = jnp.where(real, jnp.exp(f_col[last:last + 1, :] - f_col + i_col - m_new), 0.0)
            kw = kf * wk
            c_ref[s, h] = g_c * c0 + _dot_tn(kw.astype(BF16), v)
            n_ref[s, h:h + 1, :] = g_c * n0 + jnp.sum(kw, axis=0, keepdims=True)
            m_ref[pl.ds(s, 1), h:h + 1] = m_new
        return carry

    lax.fori_loop(0, nseq, per_seq, 0)


def _mlstm_sample(mqkv, so, gates, c0, n0, m0, nseq_blk, t_real):
    nseq = c0.shape[0]
    rows = nseq_blk * ROWS_PAD
    rblk = lambda w: pl.BlockSpec((rows, w), lambda i: (i, 0))
    cblk = pl.BlockSpec((nseq_blk, M_HEADS, M_HEAD_DIM, M_HEAD_DIM), lambda i: (i, 0, 0, 0))
    nblk = pl.BlockSpec((nseq_blk, M_HEADS, M_HEAD_DIM), lambda i: (i, 0, 0))
    mblk = pl.BlockSpec((nseq_blk, M_HEADS), lambda i: (i, 0))
    return pl.pallas_call(
        functools.partial(_mlstm_sample_kernel, nseq=nseq_blk, t_real=t_real),
        grid=(nseq // nseq_blk,),
        in_specs=[rblk(3 * M_WIDTH), rblk(M_WIDTH), rblk(LANES), cblk, nblk, mblk],
        out_specs=[rblk(M_WIDTH), cblk, nblk, mblk],
        out_shape=[jax.ShapeDtypeStruct((nseq * ROWS_PAD, M_WIDTH), BF16),
                   jax.ShapeDtypeStruct(c0.shape, F32),
                   jax.ShapeDtypeStruct(n0.shape, F32),
                   jax.ShapeDtypeStruct(m0.shape, F32)],
        compiler_params=_cparams(("parallel",)),
        name="mlstm_sample",
    )(mqkv, so, gates, c0, n0, m0)


SEARCH_SCHEDULE = "bbbbiiibii"
MASKED = -1e30
ROW_BLOCK = 128
SELECT_TK = 256


def _row_pass(sc_ref, nch, tk, init, fn, params=()):
    rows = sc_ref.shape[0]
    rb = min(ROW_BLOCK, rows)
    outs = []
    for r0 in range(0, rows, rb):
        rsl = lambda t: jax.tree.map(lambda a: a[r0:r0 + rb], t)
        prm = rsl(params)

        def body(c, acc, r0=r0, prm=prm):
            c0 = pl.multiple_of(c * tk, tk)
            blk = sc_ref[r0:r0 + rb, pl.ds(c0, tk)]
            for j in range(tk // LANES):
                acc = fn(acc, blk[:, j * LANES:(j + 1) * LANES], c0 + j * LANES, prm)
            return acc

        outs.append(lax.fori_loop(0, nch, body, rsl(init)))
    return jax.tree.map(lambda *a: jnp.concatenate(a, axis=0), *outs)


def _select_threshold(sc_ref, nch, tk, kr, smax):
    rows = sc_ref.shape[0]
    zeros = jnp.zeros((rows, LANES), F32)
    lsum = lambda a: jnp.sum(a, axis=1, keepdims=True)
    full = lambda col: jnp.broadcast_to(col, (rows, LANES))

    def count_ge(thr):
        return lsum(_row_pass(sc_ref, nch, tk, zeros, lambda a, x, _, t: a + jnp.where(x >= t, 1.0, 0.0), full(thr)))

    def max_below(thr):
        acc = _row_pass(sc_ref, nch, tk, jnp.full((rows, LANES), NEG_INF, F32),
                        lambda a, x, _, t: jnp.maximum(a, jnp.where(x < t, x, NEG_INF)), full(thr))
        return jnp.max(acc, axis=1, keepdims=True)

    def range_fn(a, x, _, prm):
        fin = x > NEG_INF
        return (jnp.maximum(a[0], x), jnp.minimum(a[1], jnp.where(fin, x, jnp.inf)), a[2] + jnp.where(fin, 1.0, 0.0))

    mx, mn, nf = _row_pass(
        sc_ref, nch, tk, (jnp.full((rows, LANES), NEG_INF, F32), jnp.full((rows, LANES), jnp.inf, F32), zeros), range_fn)
    rmax = jnp.max(mx, axis=1, keepdims=True)
    rmin = jnp.min(mn, axis=1, keepdims=True)
    c_max = count_ge(rmax)

    lo, hi, c_lo, c_hi = rmin, rmax, lsum(nf), c_max
    for kind in SEARCH_SCHEDULE:
        if kind == "b":
            mid = 0.5 * (lo + hi)
        else:
            frac = jnp.clip((c_lo - kr + 0.5) / jnp.maximum(c_lo - c_hi, 1.0), 0.02, 0.98)
            mid = lo + (hi - lo) * frac
        c = count_ge(mid)
        ge = c >= kr
        lo, c_lo = jnp.where(ge, mid, lo), jnp.where(ge, c, c_lo)
        hi, c_hi = jnp.where(ge, hi, mid), jnp.where(ge, c_hi, c)

    done0 = jnp.where(c_max >= kr, 1.0, 0.0)

    def peel_cond(st):
        it, _, _, _, _, _, done = st
        return jnp.logical_and(it <= smax, jnp.min(done) < 0.5)

    def peel(st):
        it, hi, c_hi, tau, c_gt, c_ge, done = st
        t1 = max_below(hi)
        c1 = count_ge(t1)
        fin = jnp.logical_and(c1 >= kr, done < 0.5)
        tau = jnp.where(fin, t1, tau)
        c_gt = jnp.where(fin, c_hi, c_gt)
        c_ge = jnp.where(fin, c1, c_ge)
        done = jnp.where(fin, 1.0, done)
        live = done < 0.5
        return it + 1, jnp.where(live, t1, hi), jnp.where(live, c1, c_hi), tau, c_gt, c_ge, done

    st = (jnp.int32(0), hi, c_hi, rmax, jnp.zeros_like(rmax), c_max, done0)
    _, _, _, tau, c_gt, c_ge, _ = lax.while_loop(peel_cond, peel, st)

    need = kr - c_gt
    surplus = jnp.max((c_ge - c_gt) - need)

    @pl.when(surplus > 0.5)
    def _():
        lane = lax.broadcasted_iota(jnp.int32, (1, LANES), 1)
        tau_f = full(tau)

        def count_eq_le(idx):
            return lsum(_row_pass(
                sc_ref, nch, tk, zeros,
                lambda a, x, c0, p: a + jnp.where(jnp.logical_and(x == p[0], (lane + c0) <= p[1]), 1.0, 0.0),
                (tau_f, jnp.broadcast_to(idx, (rows, LANES)))))

        def isearch(_, st):
            lo_i, hi_i = st
            mid = (lo_i + hi_i) >> 1
            ok = count_eq_le(mid) >= need
            return jnp.where(ok, lo_i, mid), jnp.where(ok, mid, hi_i)

        n_it = int(math.ceil(math.log2(smax + 1))) + 1
        _, sigma = lax.fori_loop(0, n_it, isearch,
                                 (jnp.full((rows, 1), -1, jnp.int32), jnp.full((rows, 1), smax - 1, jnp.int32)))

        def rewrite(c, carry):
            c0 = pl.multiple_of(c * tk, tk)
            x = sc_ref[:, pl.ds(c0, tk)]
            col = lax.broadcasted_iota(jnp.int32, (1, tk), 1) + c0
            sc_ref[:, pl.ds(c0, tk)] = jnp.where(jnp.logical_and(x == tau, col > sigma), NEG_INF, x)
            return carry

        lax.fori_loop(0, nch, rewrite, 0)

    return tau


def _ones_rows(tk):
    return jnp.where(lax.broadcasted_iota(jnp.int32, (LANES, tk), 0) == 0, 1.0, 0.0).astype(BF16)


def _dsa_prompt_kernel(qi_ref, g_ref, q_ref, kit_ref, kt_ref, vt_ref, o_ref, sc_ref, tau_scr, m_scr, acc_scr,
                       *, tq, tk, topk, smax):
    i = pl.program_id(1)
    nch = ((i + 1) * tq + tk - 1) // tk
    row = lax.broadcasted_iota(jnp.int32, (tq, 1), 0) + i * tq

    def score_chunk(c, carry):
        c0 = pl.multiple_of(c * tk, tk)
        kblk = kit_ref[:, pl.ds(c0, tk)]
        acc = jnp.zeros((tq, tk), F32)
        for h in range(IDX_HEADS):
            x = _dot(qi_ref[:, h * LANES:(h + 1) * LANES], kblk)
            acc = acc + jnp.maximum(x, 0.0) * g_ref[:, L_WI + h:L_WI + h + 1]
        col = lax.broadcasted_iota(jnp.int32, (1, tk), 1) + c0
        sc_ref[:, pl.ds(c0, tk)] = jnp.where(col <= row, acc, NEG_INF)
        return carry

    lax.fori_loop(0, nch, score_chunk, 0)

    kr = jnp.minimum(row + 1, topk).astype(F32)
    tau = _select_threshold(sc_ref, nch * (tk // SELECT_TK), SELECT_TK, kr, smax)

    nslab = tk // LANES
    tau_scr[...] = jnp.broadcast_to(tau, (tq, LANES))

    def masked_logits(c0, h):
        pr = slice((h // 2) * LANES, (h // 2 + 1) * LANES)
        s = _dot(q_ref[:, h * LANES:(h + 1) * LANES], kt_ref[pr, pl.ds(c0, tk)])
        x = sc_ref[:, pl.ds(c0, tk)]
        thr = tau_scr[...]
        return [jnp.where(x[:, j * LANES:(j + 1) * LANES] >= thr, s[:, j * LANES:(j + 1) * LANES], MASKED)
                for j in range(nslab)]

    m_scr[...] = jnp.full(m_scr.shape, MASKED, F32)

    def max_chunk(c, carry):
        c0 = pl.multiple_of(c * tk, tk)
        for h in range(A_HEADS):
            slabs = masked_logits(c0, h)
            part = slabs[0]
            for j in range(1, nslab):
                part = jnp.maximum(part, slabs[j])
            m_scr[h] = jnp.maximum(m_scr[h], part)
        return carry

    lax.fori_loop(0, nch, max_chunk, 0)
    for h in range(A_HEADS):
        m_scr[h] = jnp.broadcast_to(jnp.max(m_scr[h], axis=1, keepdims=True), (tq, LANES))

    acc_scr[...] = jnp.zeros(acc_scr.shape, F32)
    ones = _ones_rows(tk)

    def attend_chunk(c, carry):
        c0 = pl.multiple_of(c * tk, tk)
        for h in range(A_HEADS):
            pr = slice((h // 2) * LANES, (h // 2 + 1) * LANES)
            m_h = m_scr[h]
            pb = jnp.concatenate([jnp.exp2(sl - m_h).astype(BF16) for sl in masked_logits(c0, h)], axis=1)
            vt_aug = jnp.concatenate([vt_ref[pr, pl.ds(c0, tk)], ones], axis=0)
            acc_scr[h] = acc_scr[h] + _dot_nt(pb, vt_aug)
        return carry

    lax.fori_loop(0, nch, attend_chunk, 0)

    lo_half = lax.broadcasted_iota(jnp.int32, (1, LANES), 1) < A_HEAD_DIM
    for p in range(A_HEADS // 2):
        a_e, a_o = acc_scr[2 * p], acc_scr[2 * p + 1]
        even = a_e[:, 0:LANES] / a_e[:, LANES:LANES + 1]
        odd = a_o[:, 0:LANES] / a_o[:, LANES:LANES + 1]
        o_ref[:, p * LANES:(p + 1) * LANES] = jnp.where(lo_half, even, odd).astype(BF16)


def _dsa_prompt(qipad, gates, qpad, kit2, ktb, vtb, nb, seq, tq, tk):
    nq = seq // tq
    topk = min(TOPK_MAX, seq // 4)
    qrow = lambda w: pl.BlockSpec((tq, w), lambda b, i: (b * nq + i, 0))
    kfull = lambda r: pl.BlockSpec((None, r, seq), lambda b, i: (b, 0, 0), pipeline_mode=pl.Buffered(1))
    return pl.pallas_call(
        functools.partial(_dsa_prompt_kernel, tq=tq, tk=tk, topk=topk, smax=seq),
        grid=(nb, nq),
        in_specs=[qrow(2 * A_WIDTH), qrow(LANES), qrow(2 * A_WIDTH), kfull(2 * IDX_DIM), kfull(A_WIDTH), kfull(A_WIDTH)],
        out_specs=qrow(A_WIDTH),
        out_shape=jax.ShapeDtypeStruct((nb * seq, A_WIDTH), BF16),
        scratch_shapes=[pltpu.VMEM((tq, seq + LANES), F32),
                        pltpu.VMEM((tq, LANES), F32),
                        pltpu.VMEM((A_HEADS, tq, LANES), F32),
                        pltpu.VMEM((A_HEADS, tq, 2 * LANES), F32)],
        compiler_params=_cparams(("parallel", "arbitrary")),
        name="dsa_prompt",
    )(qipad, gates, qpad, kit2, ktb, vtb)


def _dsa_sample_scores_kernel(pt_ref, q_ref, w_ref, *refs, n_pages, t_real):
    page_refs, new_ref, o_ref = refs[:n_pages], refs[n_pages], refs[n_pages + 1]
    q = q_ref[0]
    w = w_ref[0]
    trow = lax.broadcasted_iota(jnp.int32, (t_real, 1), 0)
    for p in range(n_pages + 1):
        keys_t = page_refs[p][0].astype(BF16) if p < n_pages else new_ref[0]
        r = jnp.maximum(_dot(q, keys_t), 0.0) * w
        sc = jnp.sum(r.reshape(t_real, IDX_HEADS, PAGE_SIZE), axis=1)
        if p == n_pages:
            col = lax.broadcasted_iota(jnp.int32, (1, PAGE_SIZE), 1)
            sc = jnp.where(col <= trow, sc, NEG_INF)
        o_ref[0, :, p * PAGE_SIZE:(p + 1) * PAGE_SIZE] = sc


def _dsa_sample_scores(page_table, qi, wi, cache_kidx_t, ki_new_t, t_real):
    nseq, n_pages = page_table.shape
    ncol = (n_pages + 1) * PAGE_SIZE
    per_seq = lambda a: pl.BlockSpec((1,) + a.shape[1:], lambda b, pt: (b, 0, 0))
    page = lambda p: pl.BlockSpec((1, IDX_DIM, PAGE_SIZE), lambda b, pt, p=p: (pt[b * n_pages + p], 0, 0))
    return pl.pallas_call(
        functools.partial(_dsa_sample_scores_kernel, n_pages=n_pages, t_real=t_real),
        grid_spec=pltpu.PrefetchScalarGridSpec(
            num_scalar_prefetch=1,
            grid=(nseq,),
            in_specs=[per_seq(qi), per_seq(wi)] + [page(p) for p in range(n_pages)] + [per_seq(ki_new_t)],
            out_specs=pl.BlockSpec((1, t_real, ncol), lambda b, pt: (b, 0, 0)),
        ),
        out_shape=jax.ShapeDtypeStruct((nseq, t_real, ncol), F32),
        compiler_params=_cparams(("parallel",)),
        name="dsa_sample_scores",
    )(page_table.reshape(-1), qi, wi, *([cache_kidx_t] * n_pages), ki_new_t)


def _dsa_sample_select_kernel(sc_ref, adj_ref, tau_ref, *, tk, topk):
    adj_ref[...] = sc_ref[...]
    rows, ncol = adj_ref.shape
    kr = jnp.full((rows, 1), float(topk), F32)
    tau = _select_threshold(adj_ref, ncol // tk, tk, kr, ncol)
    tau_ref[...] = jnp.broadcast_to(tau, (rows, LANES))


def _dsa_sample_select(sc, topk, rblk):
    rows, ncol = sc.shape
    return pl.pallas_call(
        functools.partial(_dsa_sample_select_kernel, tk=LANES, topk=topk),
        grid=(rows // rblk,),
        in_specs=[pl.BlockSpec((rblk, ncol), lambda i: (i, 0))],
        out_specs=[pl.BlockSpec((rblk, ncol), lambda i: (i, 0)), pl.BlockSpec((rblk, LANES), lambda i: (i, 0))],
        out_shape=[jax.ShapeDtypeStruct((rows, ncol), F32), jax.ShapeDtypeStruct((rows, LANES), F32)],
        compiler_params=_cparams(("parallel",)),
        name="dsa_sample_select",
    )(sc)


def _dsa_sample_attend_kernel(pt_ref, q_ref, sc_ref, tau_ref, *refs, n_pages, t_real):
    kp, vp = refs[:n_pages], refs[n_pages:2 * n_pages]
    knew_ref, vnew_ref, o_ref, s_scr = refs[2 * n_pages:]
    q = q_ref[0]
    nrow = t_real * A_HEADS
    rows_of = lambda a: jnp.broadcast_to(a[:, None, :], (t_real, A_HEADS, a.shape[-1])).reshape(nrow, a.shape[-1])
    tau = rows_of(tau_ref[0][:, 0:1])
    for p in range(n_pages + 1):
        kt = kp[p][0].reshape(A_WIDTH, PAGE_SIZE).astype(BF16) if p < n_pages else knew_ref[0]
        x = rows_of(sc_ref[0, :, p * PAGE_SIZE:(p + 1) * PAGE_SIZE])
        s_scr[:, p * PAGE_SIZE:(p + 1) * PAGE_SIZE] = jnp.where(x >= tau, _dot(q, kt), MASKED)
    s_all = s_scr[...]
    pr = jnp.exp2(s_all - jnp.max(s_all, axis=1, keepdims=True))
    l = jnp.sum(pr, axis=1, keepdims=True)
    s_scr[...] = pr
    acc = jnp.zeros((nrow, A_WIDTH), F32)
    for p in range(n_pages + 1):
        vt = vp[p][0].reshape(A_WIDTH, PAGE_SIZE).astype(BF16) if p < n_pages else vnew_ref[0]
        acc = acc + _dot_nt(s_scr[:, p * PAGE_SIZE:(p + 1) * PAGE_SIZE].astype(BF16), vt)
    out = acc / l
    head_of_row = lax.broadcasted_iota(jnp.int32, (nrow, 1), 0) % A_HEADS
    head_of_lane = lax.broadcasted_iota(jnp.int32, (1, A_WIDTH), 1) // A_HEAD_DIM
    out = jnp.where(head_of_row == head_of_lane, out, 0.0)
    o_ref[0] = jnp.sum(out.reshape(t_real, A_HEADS, A_WIDTH), axis=1).astype(BF16)


def _dsa_sample_attend(page_table, qbd, sc_adj, tau, cache_k_t, cache_v_t, k_new_t, v_new_t, t_real):
    nseq, n_pages = page_table.shape
    ncol = (n_pages + 1) * PAGE_SIZE
    per_seq = lambda a: pl.BlockSpec((1,) + a.shape[1:], lambda b, pt: (b, 0, 0))
    page = lambda p: pl.BlockSpec((1, A_HEADS, A_HEAD_DIM, PAGE_SIZE), lambda b, pt, p=p: (pt[b * n_pages + p], 0, 0, 0))
    pages = lambda: [page(p) for p in range(n_pages)]
    return pl.pallas_call(
        functools.partial(_dsa_sample_attend_kernel, n_pages=n_pages, t_real=t_real),
        grid_spec=pltpu.PrefetchScalarGridSpec(
            num_scalar_prefetch=1,
            grid=(nseq,),
            in_specs=[per_seq(qbd), per_seq(sc_adj), per_seq(tau)] + pages() + pages() + [per_seq(k_new_t), per_seq(v_new_t)],
            out_specs=pl.BlockSpec((1, t_real, A_WIDTH), lambda b, pt: (b, 0, 0)),
            scratch_shapes=[pltpu.VMEM((t_real * A_HEADS, ncol), F32)],
        ),
        out_shape=jax.ShapeDtypeStruct((nseq, t_real, A_WIDTH), BF16),
        compiler_params=_cparams(("parallel",)),
        name="dsa_sample_attend",
    )(page_table.reshape(-1), qbd, sc_adj, tau, *([cache_k_t] * n_pages), *([cache_v_t] * n_pages), k_new_t, v_new_t)


PROJ_TM = 256
OUT_TM = 256
MLSTM_CHUNK = 128
DSA_TQ = 256
DSA_TK = 512
SAMPLE_SEQ_BLK = 8
SELECT_ROWS = 128


def _unpad_heads(xpad, n_heads):
    x = xpad.reshape(xpad.shape[:-1] + (n_heads, 2, LANES // 2))
    return x[..., 0, :] + x[..., 1, :]


def kernel(x_prompt, x_sample, cache_k, cache_v, cache_kidx, state_C, state_n, state_m, page_table,
           w_in, b_gate, w_out, ln1_g, ln1_b, w_gate, w_up, w_down, ln2_g, ln2_b):
    bp, sp, _ = x_prompt.shape
    bs, ts, _ = x_sample.shape
    n_pages = page_table.shape[1]
    past = n_pages * PAGE_SIZE
    assert DEPTH == 1 and w_in.shape[0] == 1

    w = _prep_w_in(w_in[0], b_gate[0])
    wo, wg, wu, wd = (a[0].astype(BF16) for a in (w_out, w_gate, w_up, w_down))
    g1, b1, g2, b2 = (v[0].astype(F32)[None, :] for v in (ln1_g, ln1_b, ln2_g, ln2_b))

    mp = bp * sp
    xp = x_prompt.reshape(mp, D_MODEL)
    tabs_p, tabs_pt = _rope_tables(jnp.arange(sp, dtype=jnp.int32))
    (mqkv, so, gates, gatest, qpad, qipad, kt_p, ktb, vt_p, vtb, kit_p, kit2) = _proj(
        xp, w, tabs_p, tabs_pt, PROJ_TM, bp, sp)
    hm_p, caug, m_p = _mlstm_prompt(mqkv.reshape(bp, sp, -1), so.reshape(bp, sp, -1), gates.reshape(bp, sp, -1),
                                    gatest, bp, sp, MLSTM_CHUNK)
    ha_p = _dsa_prompt(qipad, gates, qpad, kit2, ktb, vtb, bp, sp, DSA_TQ, DSA_TK)
    y_p = _block_out(xp, hm_p.reshape(mp, -1), ha_p, wo, g1, b1, wg, wu, wd, g2, b2, OUT_TM)

    ms = bs * ts
    xs = x_sample.reshape(ms, D_MODEL)
    tabs_s, tabs_st = _rope_tables(jnp.tile(past + jnp.arange(ts, dtype=jnp.int32), bs))
    (mqkv_s, so_s, gates_s, _, qpad_s, qipad_s, kt_s, ktb_s, vt_s, vtb_s, kit_s, _) = _proj(
        xs, w, tabs_s, tabs_st, ms, 1, ms)
    pad_rows = lambda a: jnp.pad(a.reshape(bs, ts, -1), ((0, 0), (0, ROWS_PAD - ts), (0, 0))).reshape(bs * ROWS_PAD, -1)
    hm_s, c_s, n_s, m_s = _mlstm_sample(pad_rows(mqkv_s), pad_rows(so_s), pad_rows(gates_s),
                                        state_C[0].astype(F32), state_n[0].astype(F32), state_m[0].astype(F32),
                                        SAMPLE_SEQ_BLK, ts)
    hm_s = hm_s.reshape(bs, ROWS_PAD, -1)[:, :ts].reshape(ms, -1)

    new_page = lambda a: jnp.pad(a.reshape(a.shape[0], bs, ts).transpose(1, 0, 2), ((0, 0), (0, 0), (0, PAGE_SIZE - ts)))
    qi_s = _unpad_heads(qipad_s, IDX_HEADS).reshape(bs, ts * IDX_HEADS, IDX_DIM)
    wi_s = gates_s[:, L_WI:L_WI + IDX_HEADS].reshape(bs, ts * IDX_HEADS, 1)
    sc_s = _dsa_sample_scores(page_table, qi_s, wi_s, cache_kidx[0].transpose(0, 2, 1), new_page(kit_s[0]).astype(BF16), ts)
    ncol = sc_s.shape[-1]
    topk_s = min(TOPK_MAX, (past + ts) // 4)
    sc_adj, tau_s = _dsa_sample_select(sc_s.reshape(ms, ncol), topk_s, SELECT_ROWS)
    q_s = _unpad_heads(qpad_s, A_HEADS).reshape(bs, ts, A_HEADS, A_HEAD_DIM)
    eye = jnp.eye(A_HEADS, dtype=q_s.dtype)
    qbd = (q_s[:, :, :, None, :] * eye[None, None, :, :, None]).reshape(bs, ts * A_HEADS, A_WIDTH)
    ha_s = _dsa_sample_attend(page_table, qbd, sc_adj.reshape(bs, ts, ncol), tau_s.reshape(bs, ts, LANES),
                              cache_k[0].transpose(0, 2, 3, 1), cache_v[0].transpose(0, 2, 3, 1),
                              new_page(ktb_s[0]), new_page(vtb_s[0]), ts)
    y_s = _block_out(xs, hm_s, ha_s.reshape(ms, -1), wo, g1, b1, wg, wu, wd, g2, b2, min(OUT_TM, ms))

    heads = lambda a, b, t: a.reshape(1, b, A_HEADS, A_HEAD_DIM, t).transpose(0, 1, 4, 2, 3)
    heads_s = lambda a: a[0].T.reshape(1, bs, ts, A_HEADS, A_HEAD_DIM)
    return (y_p.reshape(bp, sp, D_MODEL), y_s.reshape(bs, ts, D_MODEL),
            heads(kt_p, bp, sp), heads(vt_p, bp, sp), kit_p.transpose(0, 2, 1)[None],
            caug[None, :, :, :, 0:M_HEAD_DIM], caug[None, :, :, :, M_HEAD_DIM], m_p[:, 0].reshape(1, bp, M_HEADS),
            heads_s(kt_s), heads_s(vt_s), kit_s[0].T.reshape(1, bs, ts, IDX_DIM),
            c_s[None], n_s[None], m_s[None])
```

```python
import functools
import math

import jax
import jax.numpy as jnp
from jax import lax
from jax.experimental import pallas as pl
from jax.experimental.pallas import tpu as pltpu

D_MODEL = 1024
M_HEADS = 4
M_HEAD_DIM = 128
M_WIDTH = 512
A_HEADS = 8
A_HEAD_DIM = 64
A_WIDTH = 512
IDX_HEADS = 8
IDX_DIM = 64
TOPK_MAX = 256
PAGE_SIZE = 128
ROPE_THETA = 500000.0
ROT = A_HEAD_DIM // 4
HALF = ROT // 2
D_FF = 2816
DEPTH = 1
ALPHA = (2 * DEPTH) ** 0.25
LN_EPS = 1e-5
LOG2E = math.log2(math.e)

LANES = 128
SUBLANES = 8
VMEM_LIMIT = 56 * 1024 * 1024

NEG_INF = float("-inf")
BF16 = jnp.bfloat16
F32 = jnp.float32

C_MQ, C_MK, C_MV, C_MO = 0, 512, 1024, 1536
C_AQ, C_IQ, C_SM = 2048, 2560, 3072
N_PERM = 3200
L_WI, L_IG, L_FG = 64, 72, 76
R_AK, R_AV, R_IK, N_ROWS_T = 0, 512, 1024, 1152


def _cparams(sem):
    return pltpu.CompilerParams(dimension_semantics=sem, vmem_limit_bytes=VMEM_LIMIT)


def _dot(a, b):
    return jnp.dot(a, b, preferred_element_type=F32)


def _dot_nt(a, b):
    return lax.dot_general(a, b, (((1,), (1,)), ((), ())), preferred_element_type=F32)


def _dot_tn(a, b):
    return lax.dot_general(a, b, (((0,), (0,)), ((), ())), preferred_element_type=F32)


def _rope128(x, cos, sa, sb):
    return x * cos + pltpu.roll(x, LANES - HALF, 1) * sa + pltpu.roll(x, HALF, 1) * sb


def _rope_rows(z, cos_t, sin_t):
    a, b = z[0:HALF], z[HALF:ROT]
    return jnp.concatenate([a * cos_t - b * sin_t, b * cos_t + a * sin_t, z[ROT:]], axis=0)


def _proj_kernel(x_ref, w_ref, wt_ref, wgt_ref, bias_ref, biast_ref, cos_ref, sa_ref, sb_ref, cost_ref, sint_ref,
                 mqkv_ref, so_ref, gates_ref, gatest_ref, qpad_ref, qipad_ref,
                 kt_ref, ktb_ref, vt_ref, vtb_ref, kit_ref, kit2_ref):
    xb = x_ref[...].astype(BF16)
    cos, sa, sb = cos_ref[...], sa_ref[...], sb_ref[...]
    cos_t, sin_t = cost_ref[...], sint_ref[...]
    lane = lax.broadcasted_iota(jnp.int32, (1, LANES), 1)
    lo_half = lane < A_HEAD_DIM

    zm = _dot(xb, w_ref[:, C_MQ:C_MO])
    mqkv_ref[:, 0:M_WIDTH] = zm[:, 0:M_WIDTH].astype(BF16)
    mqkv_ref[:, M_WIDTH:2 * M_WIDTH] = (zm[:, M_WIDTH:2 * M_WIDTH] * (M_HEAD_DIM ** -0.5)).astype(BF16)
    mqkv_ref[:, 2 * M_WIDTH:3 * M_WIDTH] = zm[:, 2 * M_WIDTH:3 * M_WIDTH].astype(BF16)
    so_ref[...] = jax.nn.sigmoid(_dot(xb, w_ref[:, C_MO:C_AQ]))

    def padded_heads(z, scale, out_ref):
        for p in range(A_WIDTH // LANES):
            r = _rope128(z[:, p * LANES:(p + 1) * LANES], cos, sa, sb) * scale
            out_ref[:, (2 * p) * LANES:(2 * p + 1) * LANES] = jnp.where(lo_half, r, 0.0).astype(BF16)
            out_ref[:, (2 * p + 1) * LANES:(2 * p + 2) * LANES] = jnp.where(lo_half, 0.0, r).astype(BF16)

    padded_heads(_dot(xb, w_ref[:, C_AQ:C_IQ]), (A_HEAD_DIM ** -0.5) * LOG2E, qpad_ref)
    padded_heads(_dot(xb, w_ref[:, C_IQ:C_SM]), IDX_DIM ** -0.5, qipad_ref)

    zs = _dot(xb, w_ref[:, C_SM:N_PERM]) + bias_ref[...]
    is_wi = (lane >= L_WI) & (lane < L_IG)
    is_fg = (lane >= L_FG) & (lane < L_FG + M_HEADS)
    g = jnp.where(is_wi, zs * (IDX_HEADS ** -0.5), zs)
    gates_ref[...] = jnp.where(is_fg, jax.nn.log_sigmoid(zs), g)

    zt = _dot_nt(wgt_ref[...], xb) + biast_ref[...]
    row = lax.broadcasted_iota(jnp.int32, (2 * M_HEADS, 1), 0)
    gatest_ref[...] = jnp.where(row >= M_HEADS, jax.nn.log_sigmoid(zt), zt)

    zkt = _dot_nt(wt_ref[R_AK:R_AV, :], xb)
    for h in range(A_HEADS):
        r = _rope_rows(zkt[h * A_HEAD_DIM:(h + 1) * A_HEAD_DIM], cos_t, sin_t)
        kt_ref[h * A_HEAD_DIM:(h + 1) * A_HEAD_DIM, :] = r
        ktb_ref[h * A_HEAD_DIM:(h + 1) * A_HEAD_DIM, :] = r.astype(BF16)
    zvt = _dot_nt(wt_ref[R_AV:R_IK, :], xb)
    vt_ref[...] = zvt
    vtb_ref[...] = zvt.astype(BF16)
    zit = _dot_nt(wt_ref[R_IK:N_ROWS_T, :], xb)
    r = _rope_rows(zit[0:IDX_DIM], cos_t, sin_t)
    kit_ref[...] = r
    kit2_ref[...] = jnp.concatenate([r, r], axis=0).astype(BF16)


def _rope_tables(pos):
    inv = ROPE_THETA ** (-jnp.arange(HALF, dtype=F32) / HALF)
    ang = pos.astype(F32)[:, None] * inv[None, :]
    c, s = jnp.cos(ang), jnp.sin(ang)
    n = pos.shape[0]
    one = jnp.ones((n, A_HEAD_DIM - ROT), F32)
    zero = jnp.zeros((n, A_HEAD_DIM - ROT), F32)
    z8 = jnp.zeros((n, HALF), F32)
    cos64 = jnp.concatenate([c, c, one], axis=1)
    sa64 = jnp.concatenate([-s, z8, zero], axis=1)
    sb64 = jnp.concatenate([z8, s, zero], axis=1)
    two = lambda t: jnp.concatenate([t, t], axis=1)
    return (two(cos64), two(sa64), two(sb64)), (c.T, s.T)


def _proj(x, w, tabs, tabs_t, tm, nb, seq):
    wp, wt, wgt, bias, biast = w
    m = x.shape[0]
    nblk = seq // tm
    row = lambda wd: pl.BlockSpec((tm, wd), lambda i: (i, 0))
    full = lambda a: pl.BlockSpec(a.shape, lambda i: (0, 0))
    tab = pl.BlockSpec((tm, LANES), lambda i: (i % nblk, 0))
    tab_t = pl.BlockSpec((HALF, tm), lambda i: (0, i % nblk))
    fmaj = lambda r: pl.BlockSpec((None, r, tm), lambda i: (i // nblk, 0, i % nblk))
    outs = [
        ((m, 3 * M_WIDTH), BF16, row(3 * M_WIDTH)),
        ((m, M_WIDTH), F32, row(M_WIDTH)),
        ((m, LANES), F32, row(LANES)),
        ((2 * M_HEADS, m), F32, pl.BlockSpec((2 * M_HEADS, tm), lambda i: (0, i))),
        ((m, 2 * A_WIDTH), BF16, row(2 * A_WIDTH)),
        ((m, 2 * A_WIDTH), BF16, row(2 * A_WIDTH)),
        ((nb, A_WIDTH, seq), F32, fmaj(A_WIDTH)),
        ((nb, A_WIDTH, seq), BF16, fmaj(A_WIDTH)),
        ((nb, A_WIDTH, seq), F32, fmaj(A_WIDTH)),
        ((nb, A_WIDTH, seq), BF16, fmaj(A_WIDTH)),
        ((nb, IDX_DIM, seq), F32, fmaj(IDX_DIM)),
        ((nb, 2 * IDX_DIM, seq), BF16, fmaj(2 * IDX_DIM)),
    ]
    return pl.pallas_call(
        _proj_kernel,
        grid=(m // tm,),
        in_specs=[row(D_MODEL), full(wp), full(wt), full(wgt), full(bias), full(biast), tab, tab, tab, tab_t, tab_t],
        out_specs=[o[2] for o in outs],
        out_shape=[jax.ShapeDtypeStruct(o[0], o[1]) for o in outs],
        compiler_params=_cparams(("parallel",)),
        name="proj",
    )(x, wp, wt, wgt, bias, biast, *tabs, *tabs_t)


def _prep_w_in(w_in, b_gate):
    s = [0, 512, 1024, 1536, 2048, 2052, 2056, 2568, 3080, 3592, 4104, 4168, 4176]
    mq, mk, mv, mo, mi, mf, aq, ak, av, iq, ik, iw = [w_in[:, s[i]:s[i + 1]] for i in range(12)]
    z64 = jnp.zeros((D_MODEL, L_WI), w_in.dtype)
    pad = jnp.zeros((D_MODEL, N_PERM - C_SM - L_FG - M_HEADS), w_in.dtype)
    wp = jnp.concatenate([mq, mk, mv, mo, aq, iq, z64, iw, mi, mf, pad], axis=1).astype(BF16)
    wt = jnp.concatenate([ak, av, ik, ik], axis=1).T.astype(BF16)
    wgt = jnp.concatenate([mi, mf], axis=1).T.astype(BF16)
    bg = b_gate.astype(F32)
    bias = jnp.zeros((1, LANES), F32).at[0, L_IG:L_IG + 2 * M_HEADS].set(bg)
    return wp, wt, wgt, bias, bg[:, None]


FF_CHUNK = D_FF // 2


def _layer_norm(x, g, b):
    mu = jnp.mean(x, axis=-1, keepdims=True)
    xc = x - mu
    var = jnp.mean(xc * xc, axis=-1, keepdims=True)
    return xc * lax.rsqrt(var + LN_EPS) * g + b


def _block_out_kernel(x_ref, hm_ref, ha_ref, wo_ref, g1_ref, b1_ref, wg_ref, wu_ref, wd_ref, g2_ref, b2_ref, y_ref):
    mix = _dot(hm_ref[...], wo_ref[0:M_WIDTH, :]) + _dot(ha_ref[...], wo_ref[M_WIDTH:2 * M_WIDTH, :])
    x1 = _layer_norm(ALPHA * x_ref[...] + mix, g1_ref[...], b1_ref[...])
    xb = x1.astype(BF16)
    ff = jnp.zeros_like(x1)
    for c in range(D_FF // FF_CHUNK):
        sl = slice(c * FF_CHUNK, (c + 1) * FF_CHUNK)
        act = jax.nn.silu(_dot(xb, wg_ref[:, sl])) * _dot(xb, wu_ref[:, sl])
        ff = ff + _dot(act.astype(BF16), wd_ref[sl, :])
    y_ref[...] = _layer_norm(ALPHA * x1 + ff, g2_ref[...], b2_ref[...])


def _block_out(x, hm, ha, wo, g1, b1, wg, wu, wd, g2, b2, tm):
    m = x.shape[0]
    row = lambda w: pl.BlockSpec((tm, w), lambda i: (i, 0))
    full = lambda a: pl.BlockSpec(a.shape, lambda i: (0, 0), pipeline_mode=pl.Buffered(1))
    return pl.pallas_call(
        _block_out_kernel,
        grid=(m // tm,),
        in_specs=[row(D_MODEL), row(M_WIDTH), row(A_WIDTH), full(wo), full(g1), full(b1),
                  full(wg), full(wu), full(wd), full(g2), full(b2)],
        out_specs=row(D_MODEL),
        out_shape=jax.ShapeDtypeStruct((m, D_MODEL), F32),
        compiler_params=_cparams(("parallel",)),
        name="block_out",
    )(x, hm, ha, wo, g1, b1, wg, wu, wd, g2, b2)


def _split3(x):
    h1 = x.astype(BF16)
    r1 = x - h1.astype(F32)
    h2 = r1.astype(BF16)
    h3 = (r1 - h2.astype(F32)).astype(BF16)
    return h1, h2, h3


def _dot3(x, w):
    h1, h2, h3 = _split3(x)
    return _dot(h1, w) + _dot(h2, w) + _dot(h3, w)


def _mlstm_chunk(q, k, v, lf_row, i_row, i_col, c_aug, m_prev):
    L = q.shape[0]
    t_idx = lax.broadcasted_iota(jnp.int32, (L, L), 0)
    s_idx = lax.broadcasted_iota(jnp.int32, (L, L), 1)
    causal = s_idx <= t_idx
    ones_b = jnp.ones((L, LANES), BF16)
    G = _dot3(jnp.where(causal, lf_row, 0.0), ones_b)
    tri_u = jnp.where(t_idx <= s_idx, 1.0, 0.0).astype(BF16)
    f_row = _dot3(jnp.broadcast_to(lf_row, (SUBLANES, L)), tri_u)[0:1, :]
    f_col = G[:, 0:1]
    dlog = jnp.where(causal, f_col - f_row + i_row, NEG_INF)
    inter = m_prev + f_col
    m_t = jnp.maximum(jnp.max(dlog, axis=1, keepdims=True), inter)
    w = jnp.exp(dlog - m_t)
    g = jnp.exp(inter - m_t)
    s = _dot_nt(q, k) * w
    lane = lax.broadcasted_iota(jnp.int32, (L, LANES), 1)
    v_aug = jnp.concatenate([v, jnp.where(lane == 0, 1.0, 0.0).astype(BF16)], axis=1)
    nd = _dot(s.astype(BF16), v_aug) + g * _dot(q, c_aug.astype(BF16))
    den = nd[:, M_HEAD_DIM:M_HEAD_DIM + 1]
    h = nd[:, 0:M_HEAD_DIM] / jnp.maximum(jnp.abs(den), jnp.exp(-m_t))
    m_new = m_t[L - 1:L, :]
    g_c = g[L - 1:L, :]
    wk = jnp.exp(f_col[L - 1:L, :] - f_col + i_col - m_new)
    kw = (k.astype(F32) * wk).astype(BF16)
    c_new = g_c * c_aug + _dot_tn(kw, v_aug)
    return h, c_new, m_new


def _mlstm_prompt_kernel(*refs, nb):
    mqkv_ref, so_ref, g_ref = refs[0:3]
    gt_refs = refs[3:3 + nb]
    hm_ref, c_ref, m_ref = refs[3 + nb:]
    step = pl.program_id(0)

    @pl.when(step == 0)
    def _():
        c_ref[...] = jnp.zeros_like(c_ref)
        m_ref[...] = jnp.zeros_like(m_ref)

    for b in range(nb):
        for h in range(M_HEADS):
            hs = slice(h * M_HEAD_DIM, (h + 1) * M_HEAD_DIM)
            q = mqkv_ref[b, :, hs]
            k = mqkv_ref[b, :, M_WIDTH + h * M_HEAD_DIM:M_WIDTH + (h + 1) * M_HEAD_DIM]
            v = mqkv_ref[b, :, 2 * M_WIDTH + h * M_HEAD_DIM:2 * M_WIDTH + (h + 1) * M_HEAD_DIM]
            i_row = gt_refs[b][h:h + 1, :]
            lf_row = gt_refs[b][M_HEADS + h:M_HEADS + h + 1, :]
            i_col = g_ref[b, :, L_IG + h:L_IG + h + 1]
            r = b * M_HEADS + h
            m_prev = m_ref[r:r + 1, 0:1]
            hh, c_new, m_new = _mlstm_chunk(q, k, v, lf_row, i_row, i_col, c_ref[b, h], m_prev)
            c_ref[b, h] = c_new
            m_ref[r:r + 1, :] = jnp.broadcast_to(m_new, (1, LANES))
            hm_ref[b, :, hs] = (hh * so_ref[b, :, hs]).astype(BF16)


def _mlstm_prompt(mqkv, so, gates, gatest, nb, seq, chunk):
    nc = seq // chunk
    blk = lambda w: pl.BlockSpec((nb, chunk, w), lambda c: (0, c, 0))
    gt_specs = [pl.BlockSpec((2 * M_HEADS, chunk), lambda c, b=b: (0, b * nc + c)) for b in range(nb)]
    return pl.pallas_call(
        functools.partial(_mlstm_prompt_kernel, nb=nb),
        grid=(nc,),
        in_specs=[blk(3 * M_WIDTH), blk(M_WIDTH), blk(LANES)] + gt_specs,
        out_specs=[blk(M_WIDTH),
                   pl.BlockSpec((nb, M_HEADS, M_HEAD_DIM, 2 * M_HEAD_DIM), lambda c: (0, 0, 0, 0)),
                   pl.BlockSpec((nb * M_HEADS, LANES), lambda c: (0, 0))],
        out_shape=[jax.ShapeDtypeStruct((nb, seq, M_WIDTH), BF16),
                   jax.ShapeDtypeStruct((nb, M_HEADS, M_HEAD_DIM, 2 * M_HEAD_DIM), F32),
                   jax.ShapeDtypeStruct((nb * M_HEADS, LANES), F32)],
        compiler_params=_cparams(("arbitrary",)),
        name="mlstm_prompt",
    )(mqkv, so, gates, *([gatest] * nb))


ROWS_PAD = SUBLANES


def _mlstm_sample_kernel(mqkv_ref, so_ref, g_ref, c0_ref, n0_ref, m0_ref, hm_ref, c_ref, n_ref, m_ref, *, nseq, t_real):
    row = lax.broadcasted_iota(jnp.int32, (ROWS_PAD, 1), 0)
    real = row < t_real

    def per_seq(s, carry):
        r0 = pl.multiple_of(s * ROWS_PAD, ROWS_PAD)
        gt = g_ref[pl.ds(r0, ROWS_PAD), :]
        cum = gt
        for d in range(1, t_real):
            cum = cum + jnp.where(row >= d, pltpu.roll(gt, d, 0), 0.0)
        for h in range(M_HEADS):
            hs = slice(h * M_HEAD_DIM, (h + 1) * M_HEAD_DIM)
            q = mqkv_ref[pl.ds(r0, ROWS_PAD), hs]
            k = mqkv_ref[pl.ds(r0, ROWS_PAD), M_WIDTH + h * M_HEAD_DIM:M_WIDTH + (h + 1) * M_HEAD_DIM]
            v = mqkv_ref[pl.ds(r0, ROWS_PAD), 2 * M_WIDTH + h * M_HEAD_DIM:2 * M_WIDTH + (h + 1) * M_HEAD_DIM]
            qf, kf, vf = q.astype(F32), k.astype(F32), v.astype(F32)
            i_col = gt[:, L_IG + h:L_IG + h + 1]
            f_col = cum[:, L_FG + h:L_FG + h + 1]
            c0 = c0_ref[s, h]
            n0 = n0_ref[s, h:h + 1, :]
            m0 = m0_ref[pl.ds(s, 1), h:h + 1]
            inter = m0 + f_col
            dl = [jnp.where(real & (row >= u), f_col - f_col[u:u + 1, :] + i_col[u:u + 1, :], NEG_INF)
                  for u in range(t_real)]
            m_t = inter
            for u in range(t_real):
                m_t = jnp.maximum(m_t, dl[u])
            g = jnp.exp(inter - m_t)
            qc = _dot(q, c0.astype(BF16))
            num = g * qc
            den = g * jnp.sum(qf * n0, axis=1, keepdims=True)
            for u in range(t_real):
                su = jnp.sum(qf * kf[u:u + 1, :], axis=1, keepdims=True) * jnp.exp(dl[u] - m_t)
                num = num + su * vf[u:u + 1, :]
                den = den + su
            hh = num / jnp.maximum(jnp.abs(den), jnp.exp(-m_t))
            hm_ref[pl.ds(r0, ROWS_PAD), hs] = (hh * so_ref[pl.ds(r0, ROWS_PAD), hs]).astype(BF16)
            last = t_real - 1
            m_new = m_t[last:last + 1, :]
            g_c = g[last:last + 1, :]
            wk = jnp.where(real, jnp.exp(f_col[last:last + 1, :] - f_col + i_col - m_new), 0.0)
            kw = kf * wk
            c_ref[s, h] = g_c * c0 + _dot_tn(kw.astype(BF16), v)
            n_ref[s, h:h + 1, :] = g_c * n0 + jnp.sum(kw, axis=0, keepdims=True)
            m_ref[pl.ds(s, 1), h:h + 1] = m_new
        return carry

    lax.fori_loop(0, nseq, per_seq, 0)


def _mlstm_sample(mqkv, so, gates, c0, n0, m0, nseq_blk, t_real):
    nseq = c0.shape[0]
    rows = nseq_blk * ROWS_PAD
    rblk = lambda w: pl.BlockSpec((rows, w), lambda i: (i, 0))
    cblk = pl.BlockSpec((nseq_blk, M_HEADS, M_HEAD_DIM, M_HEAD_DIM), lambda i: (i, 0, 0, 0))
    nblk = pl.BlockSpec((nseq_blk, M_HEADS, M_HEAD_DIM), lambda i: (i, 0, 0))
    mblk = pl.BlockSpec((nseq_blk, M_HEADS), lambda i: (i, 0))
    return pl.pallas_call(
        functools.partial(_mlstm_sample_kernel, nseq=nseq_blk, t_real=t_real),
        grid=(nseq // nseq_blk,),
        in_specs=[rblk(3 * M_WIDTH), rblk(M_WIDTH), rblk(LANES), cblk, nblk, mblk],
        out_specs=[rblk(M_WIDTH), cblk, nblk, mblk],
        out_shape=[jax.ShapeDtypeStruct((nseq * ROWS_PAD, M_WIDTH), BF16),
                   jax.ShapeDtypeStruct(c0.shape, F32),
                   jax.ShapeDtypeStruct(n0.shape, F32),
                   jax.ShapeDtypeStruct(m0.shape, F32)],
        compiler_params=_cparams(("parallel",)),
        name="mlstm_sample",
    )(mqkv, so, gates, c0, n0, m0)


MAX_BISECT = 20
MASKED = -1e30
ROW_BLOCK = 128
SELECT_TK = 256


def _row_pass(sc_ref, nch, tk, init, fn, params=()):
    rows = sc_ref.shape[0]
    rb = min(ROW_BLOCK, rows)
    outs = []
    for r0 in range(0, rows, rb):
        rsl = lambda t: jax.tree.map(lambda a: a[r0:r0 + rb], t)
        prm = rsl(params)

        def body(c, acc, r0=r0, prm=prm):
            c0 = pl.multiple_of(c * tk, tk)
            blk = sc_ref[r0:r0 + rb, pl.ds(c0, tk)]
            for j in range(tk // LANES):
                acc = fn(acc, blk[:, j * LANES:(j + 1) * LANES], c0 + j * LANES, prm)
            return acc

        outs.append(lax.fori_loop(0, nch, body, rsl(init)))
    return jax.tree.map(lambda *a: jnp.concatenate(a, axis=0), *outs)


def _select_threshold(sc_ref, nch, tk, kr, smax, stats=None):
    rows = sc_ref.shape[0]
    zeros = jnp.zeros((rows, LANES), F32)
    lsum = lambda a: jnp.sum(a, axis=1, keepdims=True)
    full = lambda col: jnp.broadcast_to(col, (rows, LANES))

    def count_ge(thr):
        return lsum(_row_pass(sc_ref, nch, tk, zeros, lambda a, x, _, t: a + jnp.where(x >= t, 1.0, 0.0), full(thr)))

    def max_below(thr):
        acc = _row_pass(sc_ref, nch, tk, jnp.full((rows, LANES), NEG_INF, F32),
                        lambda a, x, _, t: jnp.maximum(a, jnp.where(x < t, x, NEG_INF)), full(thr))
        return jnp.max(acc, axis=1, keepdims=True)

    def range_fn(a, x, _, prm):
        fin = x > NEG_INF
        return (jnp.maximum(a[0], x), jnp.minimum(a[1], jnp.where(fin, x, jnp.inf)), a[2] + jnp.where(fin, 1.0, 0.0))

    if stats is None:
        mx, mn, nf = _row_pass(
            sc_ref, nch, tk, (jnp.full((rows, LANES), NEG_INF, F32), jnp.full((rows, LANES), jnp.inf, F32), zeros),
            range_fn)
        rmax, rmin, n_fin = jnp.max(mx, axis=1, keepdims=True), jnp.min(mn, axis=1, keepdims=True), lsum(nf)
    else:
        rmax, rmin, n_fin = stats

    above = rmax + jnp.maximum(rmax - rmin, jnp.maximum(jnp.abs(rmax) * (2.0 ** -10), 1.0))

    def bisect_cond(st):
        it, _, _, c_lo, _ = st
        return jnp.logical_and(it < MAX_BISECT, jnp.max(c_lo - kr) > 0.5)

    def bisect(st):
        it, lo, hi, c_lo, c_hi = st
        mid = 0.5 * (lo + hi)
        c = count_ge(mid)
        ge = c >= kr
        return (it + 1, jnp.where(ge, mid, lo), jnp.where(ge, hi, mid), jnp.where(ge, c, c_lo), jnp.where(ge, c_hi, c))

    _, lo, hi, c_lo, c_hi = lax.while_loop(bisect_cond, bisect, (jnp.int32(0), rmin, above, n_fin, jnp.zeros_like(rmax)))

    done0 = jnp.where(c_lo - kr > 0.5, 0.0, 1.0)

    def peel_cond(st):
        it, _, _, _, _, _, done = st
        return jnp.logical_and(it <= smax, jnp.min(done) < 0.5)

    def peel(st):
        it, hi, c_hi, tau, c_gt, c_ge, done = st
        t1 = max_below(hi)
        c1 = count_ge(t1)
        fin = jnp.logical_and(c1 >= kr, done < 0.5)
        tau = jnp.where(fin, t1, tau)
        c_gt = jnp.where(fin, c_hi, c_gt)
        c_ge = jnp.where(fin, c1, c_ge)
        done = jnp.where(fin, 1.0, done)
        live = done < 0.5
        return it + 1, jnp.where(live, t1, hi), jnp.where(live, c1, c_hi), tau, c_gt, c_ge, done

    st = (jnp.int32(0), hi, c_hi, lo, jnp.zeros_like(rmax), kr, done0)
    _, _, _, tau, c_gt, c_ge, _ = lax.while_loop(peel_cond, peel, st)

    need = kr - c_gt
    surplus = jnp.max((c_ge - c_gt) - need)

    @pl.when(surplus > 0.5)
    def _():
        lane = lax.broadcasted_iota(jnp.int32, (1, LANES), 1)
        tau_f = full(tau)

        def count_eq_le(idx):
            return lsum(_row_pass(
                sc_ref, nch, tk, zeros,
                lambda a, x, c0, p: a + jnp.where(jnp.logical_and(x == p[0], (lane + c0) <= p[1]), 1.0, 0.0),
                (tau_f, jnp.broadcast_to(idx, (rows, LANES)))))

        def isearch(_, st):
            lo_i, hi_i = st
            mid = (lo_i + hi_i) >> 1
            ok = count_eq_le(mid) >= need
            return jnp.where(ok, lo_i, mid), jnp.where(ok, mid, hi_i)

        n_it = int(math.ceil(math.log2(smax + 1))) + 1
        _, sigma = lax.fori_loop(0, n_it, isearch,
                                 (jnp.full((rows, 1), -1, jnp.int32), jnp.full((rows, 1), smax - 1, jnp.int32)))

        def rewrite(c, carry):
            c0 = pl.multiple_of(c * tk, tk)
            x = sc_ref[:, pl.ds(c0, tk)]
            col = lax.broadcasted_iota(jnp.int32, (1, tk), 1) + c0
            sc_ref[:, pl.ds(c0, tk)] = jnp.where(jnp.logical_and(x == tau, col > sigma), NEG_INF, x)
            return carry

        lax.fori_loop(0, nch, rewrite, 0)

    return tau


def _ones_rows(tk):
    return jnp.where(lax.broadcasted_iota(jnp.int32, (LANES, tk), 0) == 0, 1.0, 0.0).astype(BF16)


def _dsa_prompt_kernel(qi_ref, g_ref, q_ref, kit_ref, kt_ref, vt_ref, o_ref, sc_ref, tau_scr, m_scr, acc_scr,
                       *, tq, tk, topk, smax):
    i = pl.program_id(1)
    nch = ((i + 1) * tq + tk - 1) // tk
    row = lax.broadcasted_iota(jnp.int32, (tq, 1), 0) + i * tq

    m_scr[0] = jnp.full((tq, LANES), NEG_INF, F32)
    m_scr[1] = jnp.full((tq, LANES), jnp.inf, F32)

    def score_chunk(c, carry):
        c0 = pl.multiple_of(c * tk, tk)
        kblk = kit_ref[:, pl.ds(c0, tk)]
        acc = jnp.zeros((tq, tk), F32)
        for h in range(IDX_HEADS):
            x = _dot(qi_ref[:, h * LANES:(h + 1) * LANES], kblk)
            acc = acc + jnp.maximum(x, 0.0) * g_ref[:, L_WI + h:L_WI + h + 1]
        col = lax.broadcasted_iota(jnp.int32, (1, LANES), 1) + c0
        hi_p, lo_p = m_scr[0], m_scr[1]
        for j in range(tk // LANES):
            causal = col + j * LANES <= row
            a = acc[:, j * LANES:(j + 1) * LANES]
            masked = jnp.where(causal, a, NEG_INF)
            sc_ref[:, pl.ds(pl.multiple_of(c0 + j * LANES, LANES), LANES)] = masked
            hi_p = jnp.maximum(hi_p, masked)
            lo_p = jnp.minimum(lo_p, jnp.where(causal, a, jnp.inf))
        m_scr[0], m_scr[1] = hi_p, lo_p
        return carry

    lax.fori_loop(0, nch, score_chunk, 0)

    kr = jnp.minimum(row + 1, topk).astype(F32)
    stats = (jnp.max(m_scr[0], axis=1, keepdims=True), jnp.min(m_scr[1], axis=1, keepdims=True), (row + 1).astype(F32))
    tau = _select_threshold(sc_ref, nch * (tk // SELECT_TK), SELECT_TK, kr, smax, stats)

    nslab = tk // LANES
    tau_scr[...] = jnp.broadcast_to(tau, (tq, LANES))

    def masked_logits(c0, h):
        pr = slice((h // 2) * LANES, (h // 2 + 1) * LANES)
        s = _dot(q_ref[:, h * LANES:(h + 1) * LANES], kt_ref[pr, pl.ds(c0, tk)])
        x = sc_ref[:, pl.ds(c0, tk)]
        thr = tau_scr[...]
        return [jnp.where(x[:, j * LANES:(j + 1) * LANES] >= thr, s[:, j * LANES:(j + 1) * LANES], MASKED)
                for j in range(nslab)]

    m_scr[...] = jnp.full(m_scr.shape, MASKED, F32)

    def max_chunk(c, carry):
        c0 = pl.multiple_of(c * tk, tk)
        for h in range(A_HEADS):
            slabs = masked_logits(c0, h)
            part = slabs[0]
            for j in range(1, nslab):
                part = jnp.maximum(part, slabs[j])
            m_scr[h] = jnp.maximum(m_scr[h], part)
        return carry

    lax.fori_loop(0, nch, max_chunk, 0)
    for h in range(A_HEADS):
        m_scr[h] = jnp.broadcast_to(jnp.max(m_scr[h], axis=1, keepdims=True), (tq, LANES))

    acc_scr[...] = jnp.zeros(acc_scr.shape, F32)
    ones = _ones_rows(tk)

    def attend_chunk(c, carry):
        c0 = pl.multiple_of(c * tk, tk)
        for h in range(A_HEADS):
            pr = slice((h // 2) * LANES, (h // 2 + 1) * LANES)
            m_h = m_scr[h]
            pb = jnp.concatenate([jnp.exp2(sl - m_h).astype(BF16) for sl in masked_logits(c0, h)], axis=1)
            vt_aug = jnp.concatenate([vt_ref[pr, pl.ds(c0, tk)], ones], axis=0)
            acc_scr[h] = acc_scr[h] + _dot_nt(pb, vt_aug)
        return carry

    lax.fori_loop(0, nch, attend_chunk, 0)

    lo_half = lax.broadcasted_iota(jnp.int32, (1, LANES), 1) < A_HEAD_DIM
    for p in range(A_HEADS // 2):
        a_e, a_o = acc_scr[2 * p], acc_scr[2 * p + 1]
        even = a_e[:, 0:LANES] / a_e[:, LANES:LANES + 1]
        odd = a_o[:, 0:LANES] / a_o[:, LANES:LANES + 1]
        o_ref[:, p * LANES:(p + 1) * LANES] = jnp.where(lo_half, even, odd).astype(BF16)


def _dsa_prompt(qipad, gates, qpad, kit2, ktb, vtb, nb, seq, tq, tk):
    nq = seq // tq
    topk = min(TOPK_MAX, seq // 4)
    qrow = lambda w: pl.BlockSpec((tq, w), lambda b, i: (b * nq + i, 0))
    kfull = lambda r: pl.BlockSpec((None, r, seq), lambda b, i: (b, 0, 0), pipeline_mode=pl.Buffered(1))
    return pl.pallas_call(
        functools.partial(_dsa_prompt_kernel, tq=tq, tk=tk, topk=topk, smax=seq),
        grid=(nb, nq),
        in_specs=[qrow(2 * A_WIDTH), qrow(LANES), qrow(2 * A_WIDTH), kfull(2 * IDX_DIM), kfull(A_WIDTH), kfull(A_WIDTH)],
        out_specs=qrow(A_WIDTH),
        out_shape=jax.ShapeDtypeStruct((nb * seq, A_WIDTH), BF16),
        scratch_shapes=[pltpu.VMEM((tq, seq + LANES), F32),
                        pltpu.VMEM((tq, LANES), F32),
                        pltpu.VMEM((A_HEADS, tq, LANES), F32),
                        pltpu.VMEM((A_HEADS, tq, 2 * LANES), F32)],
        compiler_params=_cparams(("parallel", "arbitrary")),
        name="dsa_prompt",
    )(qipad, gates, qpad, kit2, ktb, vtb)


def _dsa_sample_scores_kernel(pt_ref, q_ref, w_ref, *refs, n_pages, t_real):
    page_refs, new_ref, o_ref = refs[:n_pages], refs[n_pages], refs[n_pages + 1]
    q = q_ref[0]
    w = w_ref[0]
    trow = lax.broadcasted_iota(jnp.int32, (t_real, 1), 0)
    for p in range(n_pages + 1):
        keys_t = page_refs[p][0].astype(BF16) if p < n_pages else new_ref[0]
        r = jnp.maximum(_dot(q, keys_t), 0.0) * w
        sc = jnp.sum(r.reshape(t_real, IDX_HEADS, PAGE_SIZE), axis=1)
        if p == n_pages:
            col = lax.broadcasted_iota(jnp.int32, (1, PAGE_SIZE), 1)
            sc = jnp.where(col <= trow, sc, NEG_INF)
        o_ref[0, :, p * PAGE_SIZE:(p + 1) * PAGE_SIZE] = sc


def _dsa_sample_scores(page_table, qi, wi, cache_kidx_t, ki_new_t, t_real):
    nseq, n_pages = page_table.shape
    ncol = (n_pages + 1) * PAGE_SIZE
    per_seq = lambda a: pl.BlockSpec((1,) + a.shape[1:], lambda b, pt: (b, 0, 0))
    page = lambda p: pl.BlockSpec((1, IDX_DIM, PAGE_SIZE), lambda b, pt, p=p: (pt[b * n_pages + p], 0, 0))
    return pl.pallas_call(
        functools.partial(_dsa_sample_scores_kernel, n_pages=n_pages, t_real=t_real),
        grid_spec=pltpu.PrefetchScalarGridSpec(
            num_scalar_prefetch=1,
            grid=(nseq,),
            in_specs=[per_seq(qi), per_seq(wi)] + [page(p) for p in range(n_pages)] + [per_seq(ki_new_t)],
            out_specs=pl.BlockSpec((1, t_real, ncol), lambda b, pt: (b, 0, 0)),
        ),
        out_shape=jax.ShapeDtypeStruct((nseq, t_real, ncol), F32),
        compiler_params=_cparams(("parallel",)),
        name="dsa_sample_scores",
    )(page_table.reshape(-1), qi, wi, *([cache_kidx_t] * n_pages), ki_new_t)


def _dsa_sample_select_kernel(sc_ref, adj_ref, tau_ref, *, tk, topk):
    adj_ref[...] = sc_ref[...]
    rows, ncol = adj_ref.shape
    kr = jnp.full((rows, 1), float(topk), F32)
    tau = _select_threshold(adj_ref, ncol // tk, tk, kr, ncol)
    tau_ref[...] = jnp.broadcast_to(tau, (rows, LANES))


def _dsa_sample_select(sc, topk, rblk):
    rows, ncol = sc.shape
    return pl.pallas_call(
        functools.partial(_dsa_sample_select_kernel, tk=LANES, topk=topk),
        grid=(rows // rblk,),
        in_specs=[pl.BlockSpec((rblk, ncol), lambda i: (i, 0))],
        out_specs=[pl.BlockSpec((rblk, ncol), lambda i: (i, 0)), pl.BlockSpec((rblk, LANES), lambda i: (i, 0))],
        out_shape=[jax.ShapeDtypeStruct((rows, ncol), F32), jax.ShapeDtypeStruct((rows, LANES), F32)],
        compiler_params=_cparams(("parallel",)),
        name="dsa_sample_select",
    )(sc)


def _dsa_sample_attend_kernel(pt_ref, q_ref, sc_ref, tau_ref, *refs, n_pages, t_real):
    kp, vp = refs[:n_pages], refs[n_pages:2 * n_pages]
    knew_ref, vnew_ref, o_ref, s_scr = refs[2 * n_pages:]
    q = q_ref[0]
    nrow = t_real * A_HEADS
    rows_of = lambda a: jnp.broadcast_to(a[:, None, :], (t_real, A_HEADS, a.shape[-1])).reshape(nrow, a.shape[-1])
    tau = rows_of(tau_ref[0][:, 0:1])
    for p in range(n_pages + 1):
        kt = kp[p][0].reshape(A_WIDTH, PAGE_SIZE).astype(BF16) if p < n_pages else knew_ref[0]
        x = rows_of(sc_ref[0, :, p * PAGE_SIZE:(p + 1) * PAGE_SIZE])
        s_scr[:, p * PAGE_SIZE:(p + 1) * PAGE_SIZE] = jnp.where(x >= tau, _dot(q, kt), MASKED)
    s_all = s_scr[...]
    pr = jnp.exp2(s_all - jnp.max(s_all, axis=1, keepdims=True))
    l = jnp.sum(pr, axis=1, keepdims=True)
    s_scr[...] = pr
    acc = jnp.zeros((nrow, A_WIDTH), F32)
    for p in range(n_pages + 1):
        vt = vp[p][0].reshape(A_WIDTH, PAGE_SIZE).astype(BF16) if p < n_pages else vnew_ref[0]
        acc = acc + _dot_nt(s_scr[:, p * PAGE_SIZE:(p + 1) * PAGE_SIZE].astype(BF16), vt)
    out = acc / l
    head_of_row = lax.broadcasted_iota(jnp.int32, (nrow, 1), 0) % A_HEADS
    head_of_lane = lax.broadcasted_iota(jnp.int32, (1, A_WIDTH), 1) // A_HEAD_DIM
    out = jnp.where(head_of_row == head_of_lane, out, 0.0)
    o_ref[0] = jnp.sum(out.reshape(t_real, A_HEADS, A_WIDTH), axis=1).astype(BF16)


def _dsa_sample_attend(page_table, qbd, sc_adj, tau, cache_k_t, cache_v_t, k_new_t, v_new_t, t_real):
    nseq, n_pages = page_table.shape
    ncol = (n_pages + 1) * PAGE_SIZE
    per_seq = lambda a: pl.BlockSpec((1,) + a.shape[1:], lambda b, pt: (b, 0, 0))
    page = lambda p: pl.BlockSpec((1, A_HEADS, A_HEAD_DIM, PAGE_SIZE), lambda b, pt, p=p: (pt[b * n_pages + p], 0, 0, 0))
    pages = lambda: [page(p) for p in range(n_pages)]
    return pl.pallas_call(
        functools.partial(_dsa_sample_attend_kernel, n_pages=n_pages, t_real=t_real),
        grid_spec=pltpu.PrefetchScalarGridSpec(
            num_scalar_prefetch=1,
            grid=(nseq,),
            in_specs=[per_seq(qbd), per_seq(sc_adj), per_seq(tau)] + pages() + pages() + [per_seq(k_new_t), per_seq(v_new_t)],
            out_specs=pl.BlockSpec((1, t_real, A_WIDTH), lambda b, pt: (b, 0, 0)),
            scratch_shapes=[pltpu.VMEM((t_real * A_HEADS, ncol), F32)],
        ),
        out_shape=jax.ShapeDtypeStruct((nseq, t_real, A_WIDTH), BF16),
        compiler_params=_cparams(("parallel",)),
        name="dsa_sample_attend",
    )(page_table.reshape(-1), qbd, sc_adj, tau, *([cache_k_t] * n_pages), *([cache_v_t] * n_pages), k_new_t, v_new_t)


PROJ_TM = 256
OUT_TM = 256
MLSTM_CHUNK = 128
DSA_TQ = 256
DSA_TK = 512
SAMPLE_SEQ_BLK = 8
SELECT_ROWS = 128


def _unpad_heads(xpad, n_heads):
    x = xpad.reshape(xpad.shape[:-1] + (n_heads, 2, LANES // 2))
    return x[..., 0, :] + x[..., 1, :]


def kernel(x_prompt, x_sample, cache_k, cache_v, cache_kidx, state_C, state_n, state_m, page_table,
           w_in, b_gate, w_out, ln1_g, ln1_b, w_gate, w_up, w_down, ln2_g, ln2_b):
    bp, sp, _ = x_prompt.shape
    bs, ts, _ = x_sample.shape
    n_pages = page_table.shape[1]
    past = n_pages * PAGE_SIZE
    assert DEPTH == 1 and w_in.shape[0] == 1

    w = _prep_w_in(w_in[0], b_gate[0])
    wo, wg, wu, wd = (a[0].astype(BF16) for a in (w_out, w_gate, w_up, w_down))
    g1, b1, g2, b2 = (v[0].astype(F32)[None, :] for v in (ln1_g, ln1_b, ln2_g, ln2_b))

    mp = bp * sp
    xp = x_prompt.reshape(mp, D_MODEL)
    tabs_p, tabs_pt = _rope_tables(jnp.arange(sp, dtype=jnp.int32))
    (mqkv, so, gates, gatest, qpad, qipad, kt_p, ktb, vt_p, vtb, kit_p, kit2) = _proj(
        xp, w, tabs_p, tabs_pt, PROJ_TM, bp, sp)
    hm_p, caug, m_p = _mlstm_prompt(mqkv.reshape(bp, sp, -1), so.reshape(bp, sp, -1), gates.reshape(bp, sp, -1),
                                    gatest, bp, sp, MLSTM_CHUNK)
    ha_p = _dsa_prompt(qipad, gates, qpad, kit2, ktb, vtb, bp, sp, DSA_TQ, DSA_TK)
    y_p = _block_out(xp, hm_p.reshape(mp, -1), ha_p, wo, g1, b1, wg, wu, wd, g2, b2, OUT_TM)

    ms = bs * ts
    xs = x_sample.reshape(ms, D_MODEL)
    tabs_s, tabs_st = _rope_tables(jnp.tile(past + jnp.arange(ts, dtype=jnp.int32), bs))
    (mqkv_s, so_s, gates_s, _, qpad_s, qipad_s, kt_s, ktb_s, vt_s, vtb_s, kit_s, _) = _proj(
        xs, w, tabs_s, tabs_st, ms, 1, ms)
    pad_rows = lambda a: jnp.pad(a.reshape(bs, ts, -1), ((0, 0), (0, ROWS_PAD - ts), (0, 0))).reshape(bs * ROWS_PAD, -1)
    hm_s, c_s, n_s, m_s = _mlstm_sample(pad_rows(mqkv_s), pad_rows(so_s), pad_rows(gates_s),
                                        state_C[0].astype(F32), state_n[0].astype(F32), state_m[0].astype(F32),
                                        SAMPLE_SEQ_BLK, ts)
    hm_s = hm_s.reshape(bs, ROWS_PAD, -1)[:, :ts].reshape(ms, -1)

    new_page = lambda a: jnp.pad(a.reshape(a.shape[0], bs, ts).transpose(1, 0, 2), ((0, 0), (0, 0), (0, PAGE_SIZE - ts)))
    qi_s = _unpad_heads(qipad_s, IDX_HEADS).reshape(bs, ts * IDX_HEADS, IDX_DIM)
    wi_s = gates_s[:, L_WI:L_WI + IDX_HEADS].reshape(bs, ts * IDX_HEADS, 1)
    sc_s = _dsa_sample_scores(page_table, qi_s, wi_s, cache_kidx[0].transpose(0, 2, 1), new_page(kit_s[0]).astype(BF16), ts)
    ncol = sc_s.shape[-1]
    topk_s = min(TOPK_MAX, (past + ts) // 4)
    sc_adj, tau_s = _dsa_sample_select(sc_s.reshape(ms, ncol), topk_s, SELECT_ROWS)
    q_s = _unpad_heads(qpad_s, A_HEADS).reshape(bs, ts, A_HEADS, A_HEAD_DIM)
    eye = jnp.eye(A_HEADS, dtype=q_s.dtype)
    qbd = (q_s[:, :, :, None, :] * eye[None, None, :, :, None]).reshape(bs, ts * A_HEADS, A_WIDTH)
    ha_s = _dsa_sample_attend(page_table, qbd, sc_adj.reshape(bs, ts, ncol), tau_s.reshape(bs, ts, LANES),
                              cache_k[0].transpose(0, 2, 3, 1), cache_v[0].transpose(0, 2, 3, 1),
                              new_page(ktb_s[0]), new_page(vtb_s[0]), ts)
    y_s = _block_out(xs, hm_s, ha_s.reshape(ms, -1), wo, g1, b1, wg, wu, wd, g2, b2, min(OUT_TM, ms))

    heads = lambda a, b, t: a.reshape(1, b, A_HEADS, A_HEAD_DIM, t).transpose(0, 1, 4, 2, 3)
    heads_s = lambda a: a[0].T.reshape(1, bs, ts, A_HEADS, A_HEAD_DIM)
    return (y_p.reshape(bp, sp, D_MODEL), y_s.reshape(bs, ts, D_MODEL),
            heads(kt_p, bp, sp), heads(vt_p, bp, sp), kit_p.transpose(0, 2, 1)[None],
            caug[None, :, :, :, 0:M_HEAD_DIM], caug[None, :, :, :, M_HEAD_DIM], m_p[:, 0].reshape(1, bp, M_HEADS),
            heads_s(kt_s), heads_s(vt_s), kit_s[0].T.reshape(1, bs, ts, IDX_DIM),
            c_s[None], n_s[None], m_s[None])
```

```python
import functools
import math

import jax
import jax.numpy as jnp
from jax import lax
from jax.experimental import pallas as pl
from jax.experimental.pallas import tpu as pltpu

D_MODEL = 1024
M_HEADS = 4
M_HEAD_DIM = 128
M_WIDTH = 512
A_HEADS = 8
A_HEAD_DIM = 64
A_WIDTH = 512
IDX_HEADS = 8
IDX_DIM = 64
TOPK_MAX = 256
PAGE_SIZE = 128
ROPE_THETA = 500000.0
ROT = A_HEAD_DIM // 4
HALF = ROT // 2
D_FF = 2816
DEPTH = 1
ALPHA = (2 * DEPTH) ** 0.25
LN_EPS = 1e-5
LOG2E = math.log2(math.e)

LANES = 128
SUBLANES = 8
VMEM_LIMIT = 56 * 1024 * 1024

NEG_INF = float("-inf")
BF16 = jnp.bfloat16
F32 = jnp.float32

C_MQ, C_MK, C_MV, C_MO = 0, 512, 1024, 1536
C_AQ, C_IQ, C_SM = 2048, 2560, 3072
N_PERM = 3200
L_WI, L_IG, L_FG = 64, 72, 76
R_AK, R_AV, R_IK, N_ROWS_T = 0, 512, 1024, 1152


def _cparams(sem):
    return pltpu.CompilerParams(dimension_semantics=sem, vmem_limit_bytes=VMEM_LIMIT)


def _dot(a, b):
    return jnp.dot(a, b, preferred_element_type=F32)


def _dot_nt(a, b):
    return lax.dot_general(a, b, (((1,), (1,)), ((), ())), preferred_element_type=F32)


def _dot_tn(a, b):
    return lax.dot_general(a, b, (((0,), (0,)), ((), ())), preferred_element_type=F32)


def _rope128(x, cos, sa, sb):
    return x * cos + pltpu.roll(x, LANES - HALF, 1) * sa + pltpu.roll(x, HALF, 1) * sb


def _rope_rows(z, cos_t, sin_t):
    a, b = z[0:HALF], z[HALF:ROT]
    return jnp.concatenate([a * cos_t - b * sin_t, b * cos_t + a * sin_t, z[ROT:]], axis=0)


def _proj_kernel(x_ref, w_ref, wt_ref, wgt_ref, bias_ref, biast_ref, cos_ref, sa_ref, sb_ref, cost_ref, sint_ref,
                 mqkv_ref, so_ref, gates_ref, gatest_ref, qpad_ref, qipad_ref,
                 kt_ref, ktb_ref, vt_ref, vtb_ref, kit_ref, kit2_ref):
    xb = x_ref[...].astype(BF16)
    cos, sa, sb = cos_ref[...], sa_ref[...], sb_ref[...]
    cos_t, sin_t = cost_ref[...], sint_ref[...]
    lane = lax.broadcasted_iota(jnp.int32, (1, LANES), 1)
    lo_half = lane < A_HEAD_DIM

    zm = _dot(xb, w_ref[:, C_MQ:C_MO])
    mqkv_ref[:, 0:M_WIDTH] = zm[:, 0:M_WIDTH].astype(BF16)
    mqkv_ref[:, M_WIDTH:2 * M_WIDTH] = (zm[:, M_WIDTH:2 * M_WIDTH] * (M_HEAD_DIM ** -0.5)).astype(BF16)
    mqkv_ref[:, 2 * M_WIDTH:3 * M_WIDTH] = zm[:, 2 * M_WIDTH:3 * M_WIDTH].astype(BF16)
    so_ref[...] = jax.nn.sigmoid(_dot(xb, w_ref[:, C_MO:C_AQ]))

    def padded_heads(z, scale, out_ref):
        for p in range(A_WIDTH // LANES):
            r = _rope128(z[:, p * LANES:(p + 1) * LANES], cos, sa, sb) * scale
            out_ref[:, (2 * p) * LANES:(2 * p + 1) * LANES] = jnp.where(lo_half, r, 0.0).astype(BF16)
            out_ref[:, (2 * p + 1) * LANES:(2 * p + 2) * LANES] = jnp.where(lo_half, 0.0, r).astype(BF16)

    padded_heads(_dot(xb, w_ref[:, C_AQ:C_IQ]), (A_HEAD_DIM ** -0.5) * LOG2E, qpad_ref)
    padded_heads(_dot(xb, w_ref[:, C_IQ:C_SM]), IDX_DIM ** -0.5, qipad_ref)

    zs = _dot(xb, w_ref[:, C_SM:N_PERM]) + bias_ref[...]
    is_wi = (lane >= L_WI) & (lane < L_IG)
    is_fg = (lane >= L_FG) & (lane < L_FG + M_HEADS)
    g = jnp.where(is_wi, zs * (IDX_HEADS ** -0.5), zs)
    gates_ref[...] = jnp.where(is_fg, jax.nn.log_sigmoid(zs), g)

    zt = _dot_nt(wgt_ref[...], xb) + biast_ref[...]
    row = lax.broadcasted_iota(jnp.int32, (2 * M_HEADS, 1), 0)
    gatest_ref[...] = jnp.where(row >= M_HEADS, jax.nn.log_sigmoid(zt), zt)

    zkt = _dot_nt(wt_ref[R_AK:R_AV, :], xb)
    for h in range(A_HEADS):
        r = _rope_rows(zkt[h * A_HEAD_DIM:(h + 1) * A_HEAD_DIM], cos_t, sin_t)
        kt_ref[h * A_HEAD_DIM:(h + 1) * A_HEAD_DIM, :] = r
        ktb_ref[h * A_HEAD_DIM:(h + 1) * A_HEAD_DIM, :] = r.astype(BF16)
    zvt = _dot_nt(wt_ref[R_AV:R_IK, :], xb)
    vt_ref[...] = zvt
    vtb_ref[...] = zvt.astype(BF16)
    zit = _dot_nt(wt_ref[R_IK:N_ROWS_T, :], xb)
    r = _rope_rows(zit[0:IDX_DIM], cos_t, sin_t)
    kit_ref[...] = r
    kit2_ref[...] = jnp.concatenate([r, r], axis=0).astype(BF16)


def _rope_tables(pos):
    inv = ROPE_THETA ** (-jnp.arange(HALF, dtype=F32) / HALF)
    ang = pos.astype(F32)[:, None] * inv[None, :]
    c, s = jnp.cos(ang), jnp.sin(ang)
    n = pos.shape[0]
    one = jnp.ones((n, A_HEAD_DIM - ROT), F32)
    zero = jnp.zeros((n, A_HEAD_DIM - ROT), F32)
    z8 = jnp.zeros((n, HALF), F32)
    cos64 = jnp.concatenate([c, c, one], axis=1)
    sa64 = jnp.concatenate([-s, z8, zero], axis=1)
    sb64 = jnp.concatenate([z8, s, zero], axis=1)
    two = lambda t: jnp.concatenate([t, t], axis=1)
    return (two(cos64), two(sa64), two(sb64)), (c.T, s.T)


def _proj(x, w, tabs, tabs_t, tm, nb, seq):
    wp, wt, wgt, bias, biast = w
    m = x.shape[0]
    nblk = seq // tm
    row = lambda wd: pl.BlockSpec((tm, wd), lambda i: (i, 0))
    full = lambda a: pl.BlockSpec(a.shape, lambda i: (0, 0))
    tab = pl.BlockSpec((tm, LANES), lambda i: (i % nblk, 0))
    tab_t = pl.BlockSpec((HALF, tm), lambda i: (0, i % nblk))
    fmaj = lambda r: pl.BlockSpec((None, r, tm), lambda i: (i // nblk, 0, i % nblk))
    outs = [
        ((m, 3 * M_WIDTH), BF16, row(3 * M_WIDTH)),
        ((m, M_WIDTH), F32, row(M_WIDTH)),
        ((m, LANES), F32, row(LANES)),
        ((2 * M_HEADS, m), F32, pl.BlockSpec((2 * M_HEADS, tm), lambda i: (0, i))),
        ((m, 2 * A_WIDTH), BF16, row(2 * A_WIDTH)),
        ((m, 2 * A_WIDTH), BF16, row(2 * A_WIDTH)),
        ((nb, A_WIDTH, seq), F32, fmaj(A_WIDTH)),
        ((nb, A_WIDTH, seq), BF16, fmaj(A_WIDTH)),
        ((nb, A_WIDTH, seq), F32, fmaj(A_WIDTH)),
        ((nb, A_WIDTH, seq), BF16, fmaj(A_WIDTH)),
        ((nb, IDX_DIM, seq), F32, fmaj(IDX_DIM)),
        ((nb, 2 * IDX_DIM, seq), BF16, fmaj(2 * IDX_DIM)),
    ]
    return pl.pallas_call(
        _proj_kernel,
        grid=(m // tm,),
        in_specs=[row(D_MODEL), full(wp), full(wt), full(wgt), full(bias), full(biast), tab, tab, tab, tab_t, tab_t],
        out_specs=[o[2] for o in outs],
        out_shape=[jax.ShapeDtypeStruct(o[0], o[1]) for o in outs],
        compiler_params=_cparams(("parallel",)),
        name="proj",
    )(x, wp, wt, wgt, bias, biast, *tabs, *tabs_t)


def _prep_w_in(w_in, b_gate):
    s = [0, 512, 1024, 1536, 2048, 2052, 2056, 2568, 3080, 3592, 4104, 4168, 4176]
    mq, mk, mv, mo, mi, mf, aq, ak, av, iq, ik, iw = [w_in[:, s[i]:s[i + 1]] for i in range(12)]
    z64 = jnp.zeros((D_MODEL, L_WI), w_in.dtype)
    pad = jnp.zeros((D_MODEL, N_PERM - C_SM - L_FG - M_HEADS), w_in.dtype)
    wp = jnp.concatenate([mq, mk, mv, mo, aq, iq, z64, iw, mi, mf, pad], axis=1).astype(BF16)
    wt = jnp.concatenate([ak, av, ik, ik], axis=1).T.astype(BF16)
    wgt = jnp.concatenate([mi, mf], axis=1).T.astype(BF16)
    bg = b_gate.astype(F32)
    bias = jnp.zeros((1, LANES), F32).at[0, L_IG:L_IG + 2 * M_HEADS].set(bg)
    return wp, wt, wgt, bias, bg[:, None]


FF_CHUNK = D_FF // 2


def _layer_norm(x, g, b):
    mu = jnp.mean(x, axis=-1, keepdims=True)
    xc = x - mu
    var = jnp.mean(xc * xc, axis=-1, keepdims=True)
    return xc * lax.rsqrt(var + LN_EPS) * g + b


def _block_out_kernel(x_ref, hm_ref, ha_ref, wo_ref, g1_ref, b1_ref, wg_ref, wu_ref, wd_ref, g2_ref, b2_ref, y_ref):
    mix = _dot(hm_ref[...], wo_ref[0:M_WIDTH, :]) + _dot(ha_ref[...], wo_ref[M_WIDTH:2 * M_WIDTH, :])
    x1 = _layer_norm(ALPHA * x_ref[...] + mix, g1_ref[...], b1_ref[...])
    xb = x1.astype(BF16)
    ff = jnp.zeros_like(x1)
    for c in range(D_FF // FF_CHUNK):
        sl = slice(c * FF_CHUNK, (c + 1) * FF_CHUNK)
        act = jax.nn.silu(_dot(xb, wg_ref[:, sl])) * _dot(xb, wu_ref[:, sl])
        ff = ff + _dot(act.astype(BF16), wd_ref[sl, :])
    y_ref[...] = _layer_norm(ALPHA * x1 + ff, g2_ref[...], b2_ref[...])


def _block_out(x, hm, ha, wo, g1, b1, wg, wu, wd, g2, b2, tm):
    m = x.shape[0]
    row = lambda w: pl.BlockSpec((tm, w), lambda i: (i, 0))
    full = lambda a: pl.BlockSpec(a.shape, lambda i: (0, 0), pipeline_mode=pl.Buffered(1))
    return pl.pallas_call(
        _block_out_kernel,
        grid=(m // tm,),
        in_specs=[row(D_MODEL), row(M_WIDTH), row(A_WIDTH), full(wo), full(g1), full(b1),
                  full(wg), full(wu), full(wd), full(g2), full(b2)],
        out_specs=row(D_MODEL),
        out_shape=jax.ShapeDtypeStruct((m, D_MODEL), F32),
        compiler_params=_cparams(("parallel",)),
        name="block_out",
    )(x, hm, ha, wo, g1, b1, wg, wu, wd, g2, b2)


def _split3(x):
    h1 = x.astype(BF16)
    r1 = x - h1.astype(F32)
    h2 = r1.astype(BF16)
    h3 = (r1 - h2.astype(F32)).astype(BF16)
    return h1, h2, h3


def _dot3(x, w):
    h1, h2, h3 = _split3(x)
    return _dot(h1, w) + _dot(h2, w) + _dot(h3, w)


def _mlstm_chunk(q, k, v, lf_row, i_row, i_col, c_aug, m_prev):
    L = q.shape[0]
    t_idx = lax.broadcasted_iota(jnp.int32, (L, L), 0)
    s_idx = lax.broadcasted_iota(jnp.int32, (L, L), 1)
    causal = s_idx <= t_idx
    ones_b = jnp.ones((L, LANES), BF16)
    G = _dot3(jnp.where(causal, lf_row, 0.0), ones_b)
    tri_u = jnp.where(t_idx <= s_idx, 1.0, 0.0).astype(BF16)
    f_row = _dot3(jnp.broadcast_to(lf_row, (SUBLANES, L)), tri_u)[0:1, :]
    f_col = G[:, 0:1]
    dlog = jnp.where(causal, f_col - f_row + i_row, NEG_INF)
    inter = m_prev + f_col
    m_t = jnp.maximum(jnp.max(dlog, axis=1, keepdims=True), inter)
    w = jnp.exp(dlog - m_t)
    g = jnp.exp(inter - m_t)
    s = _dot_nt(q, k) * w
    lane = lax.broadcasted_iota(jnp.int32, (L, LANES), 1)
    v_aug = jnp.concatenate([v, jnp.where(lane == 0, 1.0, 0.0).astype(BF16)], axis=1)
    nd = _dot(s.astype(BF16), v_aug) + g * _dot(q, c_aug.astype(BF16))
    den = nd[:, M_HEAD_DIM:M_HEAD_DIM + 1]
    h = nd[:, 0:M_HEAD_DIM] / jnp.maximum(jnp.abs(den), jnp.exp(-m_t))
    m_new = m_t[L - 1:L, :]
    g_c = g[L - 1:L, :]
    wk = jnp.exp(f_col[L - 1:L, :] - f_col + i_col - m_new)
    kw = (k.astype(F32) * wk).astype(BF16)
    c_new = g_c * c_aug + _dot_tn(kw, v_aug)
    return h, c_new, m_new


def _mlstm_prompt_kernel(*refs, nb):
    mqkv_ref, so_ref, g_ref = refs[0:3]
    gt_refs = refs[3:3 + nb]
    hm_ref, c_ref, m_ref = refs[3 + nb:]
    step = pl.program_id(0)

    @pl.when(step == 0)
    def _():
        c_ref[...] = jnp.zeros_like(c_ref)
        m_ref[...] = jnp.zeros_like(m_ref)

    for b in range(nb):
        for h in range(M_HEADS):
            hs = slice(h * M_HEAD_DIM, (h + 1) * M_HEAD_DIM)
            q = mqkv_ref[b, :, hs]
            k = mqkv_ref[b, :, M_WIDTH + h * M_HEAD_DIM:M_WIDTH + (h + 1) * M_HEAD_DIM]
            v = mqkv_ref[b, :, 2 * M_WIDTH + h * M_HEAD_DIM:2 * M_WIDTH + (h + 1) * M_HEAD_DIM]
            i_row = gt_refs[b][h:h + 1, :]
            lf_row = gt_refs[b][M_HEADS + h:M_HEADS + h + 1, :]
            i_col = g_ref[b, :, L_IG + h:L_IG + h + 1]
            r = b * M_HEADS + h
            m_prev = m_ref[r:r + 1, 0:1]
            hh, c_new, m_new = _mlstm_chunk(q, k, v, lf_row, i_row, i_col, c_ref[b, h], m_prev)
            c_ref[b, h] = c_new
            m_ref[r:r + 1, :] = jnp.broadcast_to(m_new, (1, LANES))
            hm_ref[b, :, hs] = (hh * so_ref[b, :, hs]).astype(BF16)


def _mlstm_prompt(mqkv, so, gates, gatest, nb, seq, chunk):
    nc = seq // chunk
    blk = lambda w: pl.BlockSpec((nb, chunk, w), lambda c: (0, c, 0))
    gt_specs = [pl.BlockSpec((2 * M_HEADS, chunk), lambda c, b=b: (0, b * nc + c)) for b in range(nb)]
    return pl.pallas_call(
        functools.partial(_mlstm_prompt_kernel, nb=nb),
        grid=(nc,),
        in_specs=[blk(3 * M_WIDTH), blk(M_WIDTH), blk(LANES)] + gt_specs,
        out_specs=[blk(M_WIDTH),
                   pl.BlockSpec((nb, M_HEADS, M_HEAD_DIM, 2 * M_HEAD_DIM), lambda c: (0, 0, 0, 0)),
                   pl.BlockSpec((nb * M_HEADS, LANES), lambda c: (0, 0))],
        out_shape=[jax.ShapeDtypeStruct((nb, seq, M_WIDTH), BF16),
                   jax.ShapeDtypeStruct((nb, M_HEADS, M_HEAD_DIM, 2 * M_HEAD_DIM), F32),
                   jax.ShapeDtypeStruct((nb * M_HEADS, LANES), F32)],
        compiler_params=_cparams(("arbitrary",)),
        name="mlstm_prompt",
    )(mqkv, so, gates, *([gatest] * nb))


ROWS_PAD = SUBLANES


def _mlstm_sample_kernel(mqkv_ref, so_ref, g_ref, c0_ref, n0_ref, m0_ref, hm_ref, c_ref, n_ref, m_ref, *, nseq, t_real):
    row = lax.broadcasted_iota(jnp.int32, (ROWS_PAD, 1), 0)
    real = row < t_real

    def per_seq(s, carry):
        r0 = pl.multiple_of(s * ROWS_PAD, ROWS_PAD)
        gt = g_ref[pl.ds(r0, ROWS_PAD), :]
        cum = gt
        for d in range(1, t_real):
            cum = cum + jnp.where(row >= d, pltpu.roll(gt, d, 0), 0.0)
        for h in range(M_HEADS):
            hs = slice(h * M_HEAD_DIM, (h + 1) * M_HEAD_DIM)
            q = mqkv_ref[pl.ds(r0, ROWS_PAD), hs]
            k = mqkv_ref[pl.ds(r0, ROWS_PAD), M_WIDTH + h * M_HEAD_DIM:M_WIDTH + (h + 1) * M_HEAD_DIM]
            v = mqkv_ref[pl.ds(r0, ROWS_PAD), 2 * M_WIDTH + h * M_HEAD_DIM:2 * M_WIDTH + (h + 1) * M_HEAD_DIM]
            qf, kf, vf = q.astype(F32), k.astype(F32), v.astype(F32)
            i_col = gt[:, L_IG + h:L_IG + h + 1]
            f_col = cum[:, L_FG + h:L_FG + h + 1]
            c0 = c0_ref[s, h]
            n0 = n0_ref[s, h:h + 1, :]
            m0 = m0_ref[pl.ds(s, 1), h:h + 1]
            inter = m0 + f_col
            dl = [jnp.where(real & (row >= u), f_col - f_col[u:u + 1, :] + i_col[u:u + 1, :], NEG_INF)
                  for u in range(t_real)]
            m_t = inter
            for u in range(t_real):
                m_t = jnp.maximum(m_t, dl[u])
            g = jnp.exp(inter - m_t)
            qc = _dot(q, c0.astype(BF16))
            num = g * qc
            den = g * jnp.sum(qf * n0, axis=1, keepdims=True)
            for u in range(t_real):
                su = jnp.sum(qf * kf[u:u + 1, :], axis=1, keepdims=True) * jnp.exp(dl[u] - m_t)
                num = num + su * vf[u:u + 1, :]
                den = den + su
            hh = num / jnp.maximum(jnp.abs(den), jnp.exp(-m_t))
            hm_ref[pl.ds(r0, ROWS_PAD), hs] = (hh * so_ref[pl.ds(r0, ROWS_PAD), hs]).astype(BF16)
            last = t_real - 1
            m_new = m_t[last:last + 1, :]
            g_c = g[last:last + 1, :]
            wk = jnp.where(real, jnp.exp(f_col[last:last + 1, :] - f_col + i_col - m_new), 0.0)
            kw = kf * wk
            c_ref[s, h] = g_c * c0 + _dot_tn(kw.astype(BF16), v)
            n_ref[s, h:h + 1, :] = g_c * n0 + jnp.sum(kw, axis=0, keepdims=True)
            m_ref[pl.ds(s, 1), h:h + 1] = m_new
        return carry

    lax.fori_loop(0, nseq, per_seq, 0)


def _mlstm_sample(mqkv, so, gates, c0, n0, m0, nseq_blk, t_real):
    nseq = c0.shape[0]
    rows = nseq_blk * ROWS_PAD
    rblk = lambda w: pl.BlockSpec((rows, w), lambda i: (i, 0))
    cblk = pl.BlockSpec((nseq_blk, M_HEADS, M_HEAD_DIM, M_HEAD_DIM), lambda i: (i, 0, 0, 0))
    nblk = pl.BlockSpec((nseq_blk, M_HEADS, M_HEAD_DIM), lambda i: (i, 0, 0))
    mblk = pl.BlockSpec((nseq_blk, M_HEADS), lambda i: (i, 0))
    return pl.pallas_call(
        functools.partial(_mlstm_sample_kernel, nseq=nseq_blk, t_real=t_real),
        grid=(nseq // nseq_blk,),
        in_specs=[rblk(3 * M_WIDTH), rblk(M_WIDTH), rblk(LANES), cblk, nblk, mblk],
        out_specs=[rblk(M_WIDTH), cblk, nblk, mblk],
        out_shape=[jax.ShapeDtypeStruct((nseq * ROWS_PAD, M_WIDTH), BF16),
                   jax.ShapeDtypeStruct(c0.shape, F32),
                   jax.ShapeDtypeStruct(n0.shape, F32),
                   jax.ShapeDtypeStruct(m0.shape, F32)],
        compiler_params=_cparams(("parallel",)),
        name="mlstm_sample",
    )(mqkv, so, gates, c0, n0, m0)


MAX_BISECT = 20
MASKED = -1e30
ROW_BLOCK = 128
SELECT_TK = 256
NORM_TK = 1024
MIN_ROW_SUM = 2.0 ** -100


def _row_pass(sc_ref, nch, tk, init, fn, params=()):
    rows = sc_ref.shape[0]
    rb = min(ROW_BLOCK, rows)
    outs = []
    for r0 in range(0, rows, rb):
        rsl = lambda t: jax.tree.map(lambda a: a[r0:r0 + rb], t)
        prm = rsl(params)

        def body(c, acc, r0=r0, prm=prm):
            c0 = pl.multiple_of(c * tk, tk)
            blk = sc_ref[r0:r0 + rb, pl.ds(c0, tk)]
            for j in range(tk // LANES):
                acc = fn(acc, blk[:, j * LANES:(j + 1) * LANES], c0 + j * LANES, prm)
            return acc

        outs.append(lax.fori_loop(0, nch, body, rsl(init)))
    return jax.tree.map(lambda *a: jnp.concatenate(a, axis=0), *outs)


def _select_threshold(sc_ref, nch, tk, kr, smax, stats=None):
    rows = sc_ref.shape[0]
    zeros = jnp.zeros((rows, LANES), F32)
    lsum = lambda a: jnp.sum(a, axis=1, keepdims=True)
    full = lambda col: jnp.broadcast_to(col, (rows, LANES))

    def count_ge(thr):
        return lsum(_row_pass(sc_ref, nch, tk, zeros, lambda a, x, _, t: a + jnp.where(x >= t, 1.0, 0.0), full(thr)))

    def max_below(thr):
        acc = _row_pass(sc_ref, nch, tk, jnp.full((rows, LANES), NEG_INF, F32),
                        lambda a, x, _, t: jnp.maximum(a, jnp.where(x < t, x, NEG_INF)), full(thr))
        return jnp.max(acc, axis=1, keepdims=True)

    def range_fn(a, x, _, prm):
        fin = x > NEG_INF
        return (jnp.maximum(a[0], x), jnp.minimum(a[1], jnp.where(fin, x, jnp.inf)), a[2] + jnp.where(fin, 1.0, 0.0))

    if stats is None:
        mx, mn, nf = _row_pass(
            sc_ref, nch, tk, (jnp.full((rows, LANES), NEG_INF, F32), jnp.full((rows, LANES), jnp.inf, F32), zeros),
            range_fn)
        rmax, rmin, n_fin = jnp.max(mx, axis=1, keepdims=True), jnp.min(mn, axis=1, keepdims=True), lsum(nf)
    else:
        rmax, rmin, n_fin = stats

    above = rmax + jnp.maximum(rmax - rmin, jnp.maximum(jnp.abs(rmax) * (2.0 ** -10), 1.0))

    def bisect_cond(st):
        it, _, _, c_lo, _ = st
        return jnp.logical_and(it < MAX_BISECT, jnp.max(c_lo - kr) > 0.5)

    def bisect(st):
        it, lo, hi, c_lo, c_hi = st
        mid = 0.5 * (lo + hi)
        c = count_ge(mid)
        ge = c >= kr
        return (it + 1, jnp.where(ge, mid, lo), jnp.where(ge, hi, mid), jnp.where(ge, c, c_lo), jnp.where(ge, c_hi, c))

    _, lo, hi, c_lo, c_hi = lax.while_loop(bisect_cond, bisect, (jnp.int32(0), rmin, above, n_fin, jnp.zeros_like(rmax)))

    done0 = jnp.where(c_lo - kr > 0.5, 0.0, 1.0)

    def peel_cond(st):
        it, _, _, _, _, _, done = st
        return jnp.logical_and(it <= smax, jnp.min(done) < 0.5)

    def peel(st):
        it, hi, c_hi, tau, c_gt, c_ge, done = st
        t1 = max_below(hi)
        c1 = count_ge(t1)
        fin = jnp.logical_and(c1 >= kr, done < 0.5)
        tau = jnp.where(fin, t1, tau)
        c_gt = jnp.where(fin, c_hi, c_gt)
        c_ge = jnp.where(fin, c1, c_ge)
        done = jnp.where(fin, 1.0, done)
        live = done < 0.5
        return it + 1, jnp.where(live, t1, hi), jnp.where(live, c1, c_hi), tau, c_gt, c_ge, done

    st = (jnp.int32(0), hi, c_hi, lo, jnp.zeros_like(rmax), kr, done0)
    _, _, _, tau, c_gt, c_ge, _ = lax.while_loop(peel_cond, peel, st)

    need = kr - c_gt
    surplus = jnp.max((c_ge - c_gt) - need)

    @pl.when(surplus > 0.5)
    def _():
        lane = lax.broadcasted_iota(jnp.int32, (1, LANES), 1)
        tau_f = full(tau)

        def count_eq_le(idx):
            return lsum(_row_pass(
                sc_ref, nch, tk, zeros,
                lambda a, x, c0, p: a + jnp.where(jnp.logical_and(x == p[0], (lane + c0) <= p[1]), 1.0, 0.0),
                (tau_f, jnp.broadcast_to(idx, (rows, LANES)))))

        def isearch(_, st):
            lo_i, hi_i = st
            mid = (lo_i + hi_i) >> 1
            ok = count_eq_le(mid) >= need
            return jnp.where(ok, lo_i, mid), jnp.where(ok, mid, hi_i)

        n_it = int(math.ceil(math.log2(smax + 1))) + 1
        _, sigma = lax.fori_loop(0, n_it, isearch,
                                 (jnp.full((rows, 1), -1, jnp.int32), jnp.full((rows, 1), smax - 1, jnp.int32)))

        def rewrite(c, carry):
            c0 = pl.multiple_of(c * tk, tk)
            x = sc_ref[:, pl.ds(c0, tk)]
            col = lax.broadcasted_iota(jnp.int32, (1, tk), 1) + c0
            sc_ref[:, pl.ds(c0, tk)] = jnp.where(jnp.logical_and(x == tau, col > sigma), NEG_INF, x)
            return carry

        lax.fori_loop(0, nch, rewrite, 0)

    return tau


def _ones_rows(tk):
    return jnp.where(lax.broadcasted_iota(jnp.int32, (LANES, tk), 0) == 0, 1.0, 0.0).astype(BF16)


def _dsa_prompt_kernel(qi_ref, g_ref, q_ref, kit_ref, kt_ref, vt_ref, o_ref, sc_ref, tau_scr, m_scr, acc_scr, kn_scr,
                       *, tq, tk, topk, smax):
    i = pl.program_id(1)
    nch = ((i + 1) * tq + tk - 1) // tk
    row = lax.broadcasted_iota(jnp.int32, (tq, 1), 0) + i * tq

    m_scr[0] = jnp.full((tq, LANES), NEG_INF, F32)
    m_scr[1] = jnp.full((tq, LANES), jnp.inf, F32)

    def score_chunk(c, carry):
        c0 = pl.multiple_of(c * tk, tk)
        kblk = kit_ref[:, pl.ds(c0, tk)]
        acc = jnp.zeros((tq, tk), F32)
        for h in range(IDX_HEADS):
            x = _dot(qi_ref[:, h * LANES:(h + 1) * LANES], kblk)
            acc = acc + jnp.maximum(x, 0.0) * g_ref[:, L_WI + h:L_WI + h + 1]
        col = lax.broadcasted_iota(jnp.int32, (1, LANES), 1) + c0
        hi_p, lo_p = m_scr[0], m_scr[1]
        for j in range(tk // LANES):
            causal = col + j * LANES <= row
            a = acc[:, j * LANES:(j + 1) * LANES]
            masked = jnp.where(causal, a, NEG_INF)
            sc_ref[:, pl.ds(pl.multiple_of(c0 + j * LANES, LANES), LANES)] = masked
            hi_p = jnp.maximum(hi_p, masked)
            lo_p = jnp.minimum(lo_p, jnp.where(causal, a, jnp.inf))
        m_scr[0], m_scr[1] = hi_p, lo_p
        return carry

    lax.fori_loop(0, nch, score_chunk, 0)

    kr = jnp.minimum(row + 1, topk).astype(F32)
    stats = (jnp.max(m_scr[0], axis=1, keepdims=True), jnp.min(m_scr[1], axis=1, keepdims=True), (row + 1).astype(F32))
    tau = _select_threshold(sc_ref, nch * (tk // SELECT_TK), SELECT_TK, kr, smax, stats)

    nslab = tk // LANES
    tau_scr[...] = jnp.broadcast_to(tau, (tq, LANES))

    def masked_logits(c0, h):
        pr = slice((h // 2) * LANES, (h // 2 + 1) * LANES)
        s = _dot(q_ref[:, h * LANES:(h + 1) * LANES], kt_ref[pr, pl.ds(c0, tk)])
        x = sc_ref[:, pl.ds(c0, tk)]
        thr = tau_scr[...]
        return [jnp.where(x[:, j * LANES:(j + 1) * LANES] >= thr, s[:, j * LANES:(j + 1) * LANES], MASKED)
                for j in range(nslab)]

    def max_sweep():
        m_scr[...] = jnp.full(m_scr.shape, MASKED, F32)

        def max_chunk(c, carry):
            c0 = pl.multiple_of(c * tk, tk)
            for h in range(A_HEADS):
                slabs = masked_logits(c0, h)
                part = slabs[0]
                for j in range(1, nslab):
                    part = jnp.maximum(part, slabs[j])
                m_scr[h] = jnp.maximum(m_scr[h], part)
            return carry

        lax.fori_loop(0, nch, max_chunk, 0)
        for h in range(A_HEADS):
            m_scr[h] = jnp.broadcast_to(jnp.max(m_scr[h], axis=1, keepdims=True), (tq, LANES))

    ones = _ones_rows(tk)

    def attend_sweep():
        acc_scr[...] = jnp.zeros(acc_scr.shape, F32)

        def attend_chunk(c, carry):
            c0 = pl.multiple_of(c * tk, tk)
            for h in range(A_HEADS):
                pr = slice((h // 2) * LANES, (h // 2 + 1) * LANES)
                m_h = m_scr[h]
                pb = jnp.concatenate([jnp.exp2(sl - m_h).astype(BF16) for sl in masked_logits(c0, h)], axis=1)
                vt_aug = jnp.concatenate([vt_ref[pr, pl.ds(c0, tk)], ones], axis=0)
                acc_scr[h] = acc_scr[h] + _dot_nt(pb, vt_aug)
            return carry

        lax.fori_loop(0, nch, attend_chunk, 0)

    @pl.when(i == 0)
    def _():
        for h in range(A_HEADS):
            def norm_chunk(c, best, h=h):
                c0 = pl.multiple_of(c * NORM_TK, NORM_TK)
                kk = kt_ref[h * A_HEAD_DIM:(h + 1) * A_HEAD_DIM, pl.ds(c0, NORM_TK)].astype(F32)
                return jnp.maximum(best, jnp.sum(kk * kk, axis=0, keepdims=True))
            best = lax.fori_loop(0, smax // NORM_TK, norm_chunk, jnp.zeros((1, NORM_TK), F32))
            kn_scr[h:h + 1, :] = jnp.broadcast_to(jnp.max(best, axis=1, keepdims=True), (1, LANES))

    for h in range(A_HEADS):
        qh = q_ref[:, h * LANES:(h + 1) * LANES].astype(F32)
        qn2 = jnp.sum(qh * qh, axis=1, keepdims=True)
        m_scr[h] = jnp.broadcast_to(jnp.sqrt(qn2 * kn_scr[h:h + 1, 0:1]), (tq, LANES))
    attend_sweep()
    l_min = acc_scr[0][:, LANES:LANES + 1]
    for h in range(1, A_HEADS):
        l_min = jnp.minimum(l_min, acc_scr[h][:, LANES:LANES + 1])

    @pl.when(jnp.logical_not(jnp.min(l_min) >= MIN_ROW_SUM))
    def _():
        max_sweep()
        attend_sweep()

    lo_half = lax.broadcasted_iota(jnp.int32, (1, LANES), 1) < A_HEAD_DIM
    for p in range(A_HEADS // 2):
        a_e, a_o = acc_scr[2 * p], acc_scr[2 * p + 1]
        even = a_e[:, 0:LANES] / a_e[:, LANES:LANES + 1]
        odd = a_o[:, 0:LANES] / a_o[:, LANES:LANES + 1]
        o_ref[:, p * LANES:(p + 1) * LANES] = jnp.where(lo_half, even, odd).astype(BF16)


def _dsa_prompt(qipad, gates, qpad, kit2, ktb, vtb, nb, seq, tq, tk):
    nq = seq // tq
    topk = min(TOPK_MAX, seq // 4)
    qrow = lambda w: pl.BlockSpec((tq, w), lambda b, i: (b * nq + i, 0))
    kfull = lambda r: pl.BlockSpec((None, r, seq), lambda b, i: (b, 0, 0), pipeline_mode=pl.Buffered(1))
    return pl.pallas_call(
        functools.partial(_dsa_prompt_kernel, tq=tq, tk=tk, topk=topk, smax=seq),
        grid=(nb, nq),
        in_specs=[qrow(2 * A_WIDTH), qrow(LANES), qrow(2 * A_WIDTH), kfull(2 * IDX_DIM), kfull(A_WIDTH), kfull(A_WIDTH)],
        out_specs=qrow(A_WIDTH),
        out_shape=jax.ShapeDtypeStruct((nb * seq, A_WIDTH), BF16),
        scratch_shapes=[pltpu.VMEM((tq, seq + LANES), F32),
                        pltpu.VMEM((tq, LANES), F32),
                        pltpu.VMEM((A_HEADS, tq, LANES), F32),
                        pltpu.VMEM((A_HEADS, tq, 2 * LANES), F32),
                        pltpu.VMEM((A_HEADS, LANES), F32)],
        compiler_params=_cparams(("parallel", "arbitrary")),
        name="dsa_prompt",
    )(qipad, gates, qpad, kit2, ktb, vtb)


def _dsa_sample_scores_kernel(pt_ref, q_ref, w_ref, *refs, n_pages, t_real):
    page_refs, new_ref, o_ref = refs[:n_pages], refs[n_pages], refs[n_pages + 1]
    q = q_ref[0]
    w = w_ref[0]
    trow = lax.broadcasted_iota(jnp.int32, (t_real, 1), 0)
    for p in range(n_pages + 1):
        keys_t = page_refs[p][0].astype(BF16) if p < n_pages else new_ref[0]
        r = jnp.maximum(_dot(q, keys_t), 0.0) * w
        sc = jnp.sum(r.reshape(t_real, IDX_HEADS, PAGE_SIZE), axis=1)
        if p == n_pages:
            col = lax.broadcasted_iota(jnp.int32, (1, PAGE_SIZE), 1)
            sc = jnp.where(col <= trow, sc, NEG_INF)
        o_ref[0, :, p * PAGE_SIZE:(p + 1) * PAGE_SIZE] = sc


def _dsa_sample_scores(page_table, qi, wi, cache_kidx_t, ki_new_t, t_real):
    nseq, n_pages = page_table.shape
    ncol = (n_pages + 1) * PAGE_SIZE
    per_seq = lambda a: pl.BlockSpec((1,) + a.shape[1:], lambda b, pt: (b, 0, 0))
    page = lambda p: pl.BlockSpec((1, IDX_DIM, PAGE_SIZE), lambda b, pt, p=p: (pt[b * n_pages + p], 0, 0))
    return pl.pallas_call(
        functools.partial(_dsa_sample_scores_kernel, n_pages=n_pages, t_real=t_real),
        grid_spec=pltpu.PrefetchScalarGridSpec(
            num_scalar_prefetch=1,
            grid=(nseq,),
            in_specs=[per_seq(qi), per_seq(wi)] + [page(p) for p in range(n_pages)] + [per_seq(ki_new_t)],
            out_specs=pl.BlockSpec((1, t_real, ncol), lambda b, pt: (b, 0, 0)),
        ),
        out_shape=jax.ShapeDtypeStruct((nseq, t_real, ncol), F32),
        compiler_params=_cparams(("parallel",)),
        name="dsa_sample_scores",
    )(page_table.reshape(-1), qi, wi, *([cache_kidx_t] * n_pages), ki_new_t)


def _dsa_sample_select_kernel(sc_ref, adj_ref, tau_ref, *, tk, topk):
    adj_ref[...] = sc_ref[...]
    rows, ncol = adj_ref.shape
    kr = jnp.full((rows, 1), float(topk), F32)
    tau = _select_threshold(adj_ref, ncol // tk, tk, kr, ncol)
    tau_ref[...] = jnp.broadcast_to(tau, (rows, LANES))


def _dsa_sample_select(sc, topk, rblk):
    rows, ncol = sc.shape
    return pl.pallas_call(
        functools.partial(_dsa_sample_select_kernel, tk=LANES, topk=topk),
        grid=(rows // rblk,),
        in_specs=[pl.BlockSpec((rblk, ncol), lambda i: (i, 0))],
        out_specs=[pl.BlockSpec((rblk, ncol), lambda i: (i, 0)), pl.BlockSpec((rblk, LANES), lambda i: (i, 0))],
        out_shape=[jax.ShapeDtypeStruct((rows, ncol), F32), jax.ShapeDtypeStruct((rows, LANES), F32)],
        compiler_params=_cparams(("parallel",)),
        name="dsa_sample_select",
    )(sc)


def _dsa_sample_attend_kernel(pt_ref, q_ref, sc_ref, tau_ref, *refs, n_pages, t_real):
    kp, vp = refs[:n_pages], refs[n_pages:2 * n_pages]
    knew_ref, vnew_ref, o_ref, s_scr = refs[2 * n_pages:]
    q = q_ref[0]
    nrow = t_real * A_HEADS
    rows_of = lambda a: jnp.broadcast_to(a[:, None, :], (t_real, A_HEADS, a.shape[-1])).reshape(nrow, a.shape[-1])
    tau = rows_of(tau_ref[0][:, 0:1])
    for p in range(n_pages + 1):
        kt = kp[p][0].reshape(A_WIDTH, PAGE_SIZE).astype(BF16) if p < n_pages else knew_ref[0]
        x = rows_of(sc_ref[0, :, p * PAGE_SIZE:(p + 1) * PAGE_SIZE])
        s_scr[:, p * PAGE_SIZE:(p + 1) * PAGE_SIZE] = jnp.where(x >= tau, _dot(q, kt), MASKED)
    s_all = s_scr[...]
    pr = jnp.exp2(s_all - jnp.max(s_all, axis=1, keepdims=True))
    l = jnp.sum(pr, axis=1, keepdims=True)
    s_scr[...] = pr
    acc = jnp.zeros((nrow, A_WIDTH), F32)
    for p in range(n_pages + 1):
        vt = vp[p][0].reshape(A_WIDTH, PAGE_SIZE).astype(BF16) if p < n_pages else vnew_ref[0]
        acc = acc + _dot_nt(s_scr[:, p * PAGE_SIZE:(p + 1) * PAGE_SIZE].astype(BF16), vt)
    out = acc / l
    head_of_row = lax.broadcasted_iota(jnp.int32, (nrow, 1), 0) % A_HEADS
    head_of_lane = lax.broadcasted_iota(jnp.int32, (1, A_WIDTH), 1) // A_HEAD_DIM
    out = jnp.where(head_of_row == head_of_lane, out, 0.0)
    o_ref[0] = jnp.sum(out.reshape(t_real, A_HEADS, A_WIDTH), axis=1).astype(BF16)


def _dsa_sample_attend(page_table, qbd, sc_adj, tau, cache_k_t, cache_v_t, k_new_t, v_new_t, t_real):
    nseq, n_pages = page_table.shape
    ncol = (n_pages + 1) * PAGE_SIZE
    per_seq = lambda a: pl.BlockSpec((1,) + a.shape[1:], lambda b, pt: (b, 0, 0))
    page = lambda p: pl.BlockSpec((1, A_HEADS, A_HEAD_DIM, PAGE_SIZE), lambda b, pt, p=p: (pt[b * n_pages + p], 0, 0, 0))
    pages = lambda: [page(p) for p in range(n_pages)]
    return pl.pallas_call(
        functools.partial(_dsa_sample_attend_kernel, n_pages=n_pages, t_real=t_real),
        grid_spec=pltpu.PrefetchScalarGridSpec(
            num_scalar_prefetch=1,
            grid=(nseq,),
            in_specs=[per_seq(qbd), per_seq(sc_adj), per_seq(tau)] + pages() + pages() + [per_seq(k_new_t), per_seq(v_new_t)],
            out_specs=pl.BlockSpec((1, t_real, A_WIDTH), lambda b, pt: (b, 0, 0)),
            scratch_shapes=[pltpu.VMEM((t_real * A_HEADS, ncol), F32)],
        ),
        out_shape=jax.ShapeDtypeStruct((nseq, t_real, A_WIDTH), BF16),
        compiler_params=_cparams(("parallel",)),
        name="dsa_sample_attend",
    )(page_table.reshape(-1), qbd, sc_adj, tau, *([cache_k_t] * n_pages), *([cache_v_t] * n_pages), k_new_t, v_new_t)


PROJ_TM = 256
OUT_TM = 256
MLSTM_CHUNK = 128
DSA_TQ = 256
DSA_TK = 512
SAMPLE_SEQ_BLK = 8
SELECT_ROWS = 128


def _unpad_heads(xpad, n_heads):
    x = xpad.reshape(xpad.shape[:-1] + (n_heads, 2, LANES // 2))
    return x[..., 0, :] + x[..., 1, :]


def kernel(x_prompt, x_sample, cache_k, cache_v, cache_kidx, state_C, state_n, state_m, page_table,
           w_in, b_gate, w_out, ln1_g, ln1_b, w_gate, w_up, w_down, ln2_g, ln2_b):
    bp, sp, _ = x_prompt.shape
    bs, ts, _ = x_sample.shape
    n_pages = page_table.shape[1]
    past = n_pages * PAGE_SIZE
    assert DEPTH == 1 and w_in.shape[0] == 1

    w = _prep_w_in(w_in[0], b_gate[0])
    wo, wg, wu, wd = (a[0].astype(BF16) for a in (w_out, w_gate, w_up, w_down))
    g1, b1, g2, b2 = (v[0].astype(F32)[None, :] for v in (ln1_g, ln1_b, ln2_g, ln2_b))

    mp = bp * sp
    xp = x_prompt.reshape(mp, D_MODEL)
    tabs_p, tabs_pt = _rope_tables(jnp.arange(sp, dtype=jnp.int32))
    (mqkv, so, gates, gatest, qpad, qipad, kt_p, ktb, vt_p, vtb, kit_p, kit2) = _proj(
        xp, w, tabs_p, tabs_pt, PROJ_TM, bp, sp)
    hm_p, caug, m_p = _mlstm_prompt(mqkv.reshape(bp, sp, -1), so.reshape(bp, sp, -1), gates.reshape(bp, sp, -1),
                                    gatest, bp, sp, MLSTM_CHUNK)
    ha_p = _dsa_prompt(qipad, gates, qpad, kit2, ktb, vtb, bp, sp, DSA_TQ, DSA_TK)
    y_p = _block_out(xp, hm_p.reshape(mp, -1), ha_p, wo, g1, b1, wg, wu, wd, g2, b2, OUT_TM)

    ms = bs * ts
    xs = x_sample.reshape(ms, D_MODEL)
    tabs_s, tabs_st = _rope_tables(jnp.tile(past + jnp.arange(ts, dtype=jnp.int32), bs))
    (mqkv_s, so_s, gates_s, _, qpad_s, qipad_s, kt_s, ktb_s, vt_s, vtb_s, kit_s, _) = _proj(
        xs, w, tabs_s, tabs_st, ms, 1, ms)
    pad_rows = lambda a: jnp.pad(a.reshape(bs, ts, -1), ((0, 0), (0, ROWS_PAD - ts), (0, 0))).reshape(bs * ROWS_PAD, -1)
    hm_s, c_s, n_s, m_s = _mlstm_sample(pad_rows(mqkv_s), pad_rows(so_s), pad_rows(gates_s),
                                        state_C[0].astype(F32), state_n[0].astype(F32), state_m[0].astype(F32),
                                        SAMPLE_SEQ_BLK, ts)
    hm_s = hm_s.reshape(bs, ROWS_PAD, -1)[:, :ts].reshape(ms, -1)

    new_page = lambda a: jnp.pad(a.reshape(a.shape[0], bs, ts).transpose(1, 0, 2), ((0, 0), (0, 0), (0, PAGE_SIZE - ts)))
    qi_s = _unpad_heads(qipad_s, IDX_HEADS).reshape(bs, ts * IDX_HEADS, IDX_DIM)
    wi_s = gates_s[:, L_WI:L_WI + IDX_HEADS].reshape(bs, ts * IDX_HEADS, 1)
    sc_s = _dsa_sample_scores(page_table, qi_s, wi_s, cache_kidx[0].transpose(0, 2, 1), new_page(kit_s[0]).astype(BF16), ts)
    ncol = sc_s.shape[-1]
    topk_s = min(TOPK_MAX, (past + ts) // 4)
    sc_adj, tau_s = _dsa_sample_select(sc_s.reshape(ms, ncol), topk_s, SELECT_ROWS)
    q_s = _unpad_heads(qpad_s, A_HEADS).reshape(bs, ts, A_HEADS, A_HEAD_DIM)
    eye = jnp.eye(A_HEADS, dtype=q_s.dtype)
    qbd = (q_s[:, :, :, None, :] * eye[None, None, :, :, None]).reshape(bs, ts * A_HEADS, A_WIDTH)
    ha_s = _dsa_sample_attend(page_table, qbd, sc_adj.reshape(bs, ts, ncol), tau_s.reshape(bs, ts, LANES),
                              cache_k[0].transpose(0, 2, 3, 1), cache_v[0].transpose(0, 2, 3, 1),
                              new_page(ktb_s[0]), new_page(vtb_s[0]), ts)
    y_s = _block_out(xs, hm_s, ha_s.reshape(ms, -1), wo, g1, b1, wg, wu, wd, g2, b2, min(OUT_TM, ms))

    heads = lambda a, b, t: a.reshape(1, b, A_HEADS, A_HEAD_DIM, t).transpose(0, 1, 4, 2, 3)
    heads_s = lambda a: a[0].T.reshape(1, bs, ts, A_HEADS, A_HEAD_DIM)
    return (y_p.reshape(bp, sp, D_MODEL), y_s.reshape(bs, ts, D_MODEL),
            heads(kt_p, bp, sp), heads(vt_p, bp, sp), kit_p.transpose(0, 2, 1)[None],
            caug[None, :, :, :, 0:M_HEAD_DIM], caug[None, :, :, :, M_HEAD_DIM], m_p[:, 0].reshape(1, bp, M_HEADS),
            heads_s(kt_s), heads_s(vt_s), kit_s[0].T.reshape(1, bs, ts, IDX_DIM),
            c_s[None], n_s[None], m_s[None])
```

```python
import functools
import math

import jax
import jax.numpy as jnp
from jax import lax
from jax.experimental import pallas as pl
from jax.experimental.pallas import tpu as pltpu

D_MODEL = 1024
M_HEADS = 4
M_HEAD_DIM = 128
M_WIDTH = 512
A_HEADS = 8
A_HEAD_DIM = 64
A_WIDTH = 512
IDX_HEADS = 8
IDX_DIM = 64
TOPK_MAX = 256
PAGE_SIZE = 128
ROPE_THETA = 500000.0
ROT = A_HEAD_DIM // 4
HALF = ROT // 2
D_FF = 2816
DEPTH = 1
ALPHA = (2 * DEPTH) ** 0.25
LN_EPS = 1e-5
LOG2E = math.log2(math.e)

LANES = 128
SUBLANES = 8
VMEM_LIMIT = 56 * 1024 * 1024

NEG_INF = float("-inf")
BF16 = jnp.bfloat16
F32 = jnp.float32

C_MQ, C_MK, C_MV, C_MO = 0, 512, 1024, 1536
C_AQ, C_IQ, C_SM = 2048, 2560, 3072
N_PERM = 3200
L_WI, L_IG, L_FG = 64, 72, 76
R_AK, R_AV, R_IK, N_ROWS_T = 0, 512, 1024, 1152


def _cparams(sem):
    return pltpu.CompilerParams(dimension_semantics=sem, vmem_limit_bytes=VMEM_LIMIT)


def _dot(a, b):
    return jnp.dot(a, b, preferred_element_type=F32)


def _dot_nt(a, b):
    return lax.dot_general(a, b, (((1,), (1,)), ((), ())), preferred_element_type=F32)


def _dot_tn(a, b):
    return lax.dot_general(a, b, (((0,), (0,)), ((), ())), preferred_element_type=F32)


def _rope128(x, cos, sa, sb):
    return x * cos + pltpu.roll(x, LANES - HALF, 1) * sa + pltpu.roll(x, HALF, 1) * sb


def _rope_rows(z, cos_t, sin_t):
    a, b = z[0:HALF], z[HALF:ROT]
    return jnp.concatenate([a * cos_t - b * sin_t, b * cos_t + a * sin_t, z[ROT:]], axis=0)


def _proj_kernel(x_ref, w_ref, wt_ref, wgt_ref, bias_ref, biast_ref, cos_ref, sa_ref, sb_ref, cost_ref, sint_ref,
                 mqkv_ref, so_ref, gates_ref, gatest_ref, qpad_ref, qipad_ref,
                 kt_ref, ktb_ref, vt_ref, vtb_ref, kit_ref, kit2_ref):
    xb = x_ref[...].astype(BF16)
    cos, sa, sb = cos_ref[...], sa_ref[...], sb_ref[...]
    cos_t, sin_t = cost_ref[...], sint_ref[...]
    lane = lax.broadcasted_iota(jnp.int32, (1, LANES), 1)
    lo_half = lane < A_HEAD_DIM

    zm = _dot(xb, w_ref[:, C_MQ:C_MO])
    mqkv_ref[:, 0:M_WIDTH] = zm[:, 0:M_WIDTH].astype(BF16)
    mqkv_ref[:, M_WIDTH:2 * M_WIDTH] = (zm[:, M_WIDTH:2 * M_WIDTH] * (M_HEAD_DIM ** -0.5)).astype(BF16)
    mqkv_ref[:, 2 * M_WIDTH:3 * M_WIDTH] = zm[:, 2 * M_WIDTH:3 * M_WIDTH].astype(BF16)
    so_ref[...] = jax.nn.sigmoid(_dot(xb, w_ref[:, C_MO:C_AQ]))

    def padded_heads(z, scale, out_ref):
        for p in range(A_WIDTH // LANES):
            r = _rope128(z[:, p * LANES:(p + 1) * LANES], cos, sa, sb) * scale
            out_ref[:, (2 * p) * LANES:(2 * p + 1) * LANES] = jnp.where(lo_half, r, 0.0).astype(BF16)
            out_ref[:, (2 * p + 1) * LANES:(2 * p + 2) * LANES] = jnp.where(lo_half, 0.0, r).astype(BF16)

    padded_heads(_dot(xb, w_ref[:, C_AQ:C_IQ]), (A_HEAD_DIM ** -0.5) * LOG2E, qpad_ref)
    padded_heads(_dot(xb, w_ref[:, C_IQ:C_SM]), IDX_DIM ** -0.5, qipad_ref)

    zs = _dot(xb, w_ref[:, C_SM:N_PERM]) + bias_ref[...]
    is_wi = (lane >= L_WI) & (lane < L_IG)
    is_fg = (lane >= L_FG) & (lane < L_FG + M_HEADS)
    g = jnp.where(is_wi, zs * (IDX_HEADS ** -0.5), zs)
    gates_ref[...] = jnp.where(is_fg, jax.nn.log_sigmoid(zs), g)

    zt = _dot_nt(wgt_ref[...], xb) + biast_ref[...]
    row = lax.broadcasted_iota(jnp.int32, (2 * M_HEADS, 1), 0)
    gatest_ref[...] = jnp.where(row >= M_HEADS, jax.nn.log_sigmoid(zt), zt)

    zkt = _dot_nt(wt_ref[R_AK:R_AV, :], xb)
    for h in range(A_HEADS):
        r = _rope_rows(zkt[h * A_HEAD_DIM:(h + 1) * A_HEAD_DIM], cos_t, sin_t)
        kt_ref[h * A_HEAD_DIM:(h + 1) * A_HEAD_DIM, :] = r
        ktb_ref[h * A_HEAD_DIM:(h + 1) * A_HEAD_DIM, :] = r.astype(BF16)
    zvt = _dot_nt(wt_ref[R_AV:R_IK, :], xb)
    vt_ref[...] = zvt
    vtb_ref[...] = zvt.astype(BF16)
    zit = _dot_nt(wt_ref[R_IK:N_ROWS_T, :], xb)
    r = _rope_rows(zit[0:IDX_DIM], cos_t, sin_t)
    kit_ref[...] = r
    kit2_ref[...] = jnp.concatenate([r, r], axis=0).astype(BF16)


def _rope_tables(pos):
    inv = ROPE_THETA ** (-jnp.arange(HALF, dtype=F32) / HALF)
    ang = pos.astype(F32)[:, None] * inv[None, :]
    c, s = jnp.cos(ang), jnp.sin(ang)
    n = pos.shape[0]
    one = jnp.ones((n, A_HEAD_DIM - ROT), F32)
    zero = jnp.zeros((n, A_HEAD_DIM - ROT), F32)
    z8 = jnp.zeros((n, HALF), F32)
    cos64 = jnp.concatenate([c, c, one], axis=1)
    sa64 = jnp.concatenate([-s, z8, zero], axis=1)
    sb64 = jnp.concatenate([z8, s, zero], axis=1)
    two = lambda t: jnp.concatenate([t, t], axis=1)
    return (two(cos64), two(sa64), two(sb64)), (c.T, s.T)


def _proj(x, w, tabs, tabs_t, tm, nb, seq):
    wp, wt, wgt, bias, biast = w
    m = x.shape[0]
    nblk = seq // tm
    row = lambda wd: pl.BlockSpec((tm, wd), lambda i: (i, 0))
    full = lambda a: pl.BlockSpec(a.shape, lambda i: (0, 0))
    tab = pl.BlockSpec((tm, LANES), lambda i: (i % nblk, 0))
    tab_t = pl.BlockSpec((HALF, tm), lambda i: (0, i % nblk))
    fmaj = lambda r: pl.BlockSpec((None, r, tm), lambda i: (i // nblk, 0, i % nblk))
    outs = [
        ((m, 3 * M_WIDTH), BF16, row(3 * M_WIDTH)),
        ((m, M_WIDTH), F32, row(M_WIDTH)),
        ((m, LANES), F32, row(LANES)),
        ((2 * M_HEADS, m), F32, pl.BlockSpec((2 * M_HEADS, tm), lambda i: (0, i))),
        ((m, 2 * A_WIDTH), BF16, row(2 * A_WIDTH)),
        ((m, 2 * A_WIDTH), BF16, row(2 * A_WIDTH)),
        ((nb, A_WIDTH, seq), F32, fmaj(A_WIDTH)),
        ((nb, A_WIDTH, seq), BF16, fmaj(A_WIDTH)),
        ((nb, A_WIDTH, seq), F32, fmaj(A_WIDTH)),
        ((nb, A_WIDTH, seq), BF16, fmaj(A_WIDTH)),
        ((nb, IDX_DIM, seq), F32, fmaj(IDX_DIM)),
        ((nb, 2 * IDX_DIM, seq), BF16, fmaj(2 * IDX_DIM)),
    ]
    return pl.pallas_call(
        _proj_kernel,
        grid=(m // tm,),
        in_specs=[row(D_MODEL), full(wp), full(wt), full(wgt), full(bias), full(biast), tab, tab, tab, tab_t, tab_t],
        out_specs=[o[2] for o in outs],
        out_shape=[jax.ShapeDtypeStruct(o[0], o[1]) for o in outs],
        compiler_params=_cparams(("parallel",)),
        name="proj",
    )(x, wp, wt, wgt, bias, biast, *tabs, *tabs_t)


def _prep_w_in(w_in, b_gate):
    s = [0, 512, 1024, 1536, 2048, 2052, 2056, 2568, 3080, 3592, 4104, 4168, 4176]
    mq, mk, mv, mo, mi, mf, aq, ak, av, iq, ik, iw = [w_in[:, s[i]:s[i + 1]] for i in range(12)]
    z64 = jnp.zeros((D_MODEL, L_WI), w_in.dtype)
    pad = jnp.zeros((D_MODEL, N_PERM - C_SM - L_FG - M_HEADS), w_in.dtype)
    wp = jnp.concatenate([mq, mk, mv, mo, aq, iq, z64, iw, mi, mf, pad], axis=1).astype(BF16)
    wt = jnp.concatenate([ak, av, ik, ik], axis=1).T.astype(BF16)
    wgt = jnp.concatenate([mi, mf], axis=1).T.astype(BF16)
    bg = b_gate.astype(F32)
    bias = jnp.zeros((1, LANES), F32).at[0, L_IG:L_IG + 2 * M_HEADS].set(bg)
    return wp, wt, wgt, bias, bg[:, None]


FF_CHUNK = D_FF // 2


def _layer_norm(x, g, b):
    mu = jnp.mean(x, axis=-1, keepdims=True)
    xc = x - mu
    var = jnp.mean(xc * xc, axis=-1, keepdims=True)
    return xc * lax.rsqrt(var + LN_EPS) * g + b


def _block_out_kernel(x_ref, hm_ref, ha_ref, wo_ref, g1_ref, b1_ref, wg_ref, wu_ref, wd_ref, g2_ref, b2_ref, y_ref):
    mix = _dot(hm_ref[...], wo_ref[0:M_WIDTH, :]) + _dot(ha_ref[...], wo_ref[M_WIDTH:2 * M_WIDTH, :])
    x1 = _layer_norm(ALPHA * x_ref[...] + mix, g1_ref[...], b1_ref[...])
    xb = x1.astype(BF16)
    ff = jnp.zeros_like(x1)
    for c in range(D_FF // FF_CHUNK):
        sl = slice(c * FF_CHUNK, (c + 1) * FF_CHUNK)
        act = jax.nn.silu(_dot(xb, wg_ref[:, sl])) * _dot(xb, wu_ref[:, sl])
        ff = ff + _dot(act.astype(BF16), wd_ref[sl, :])
    y_ref[...] = _layer_norm(ALPHA * x1 + ff, g2_ref[...], b2_ref[...])


def _block_out(x, hm, ha, wo, g1, b1, wg, wu, wd, g2, b2, tm):
    m = x.shape[0]
    row = lambda w: pl.BlockSpec((tm, w), lambda i: (i, 0))
    full = lambda a: pl.BlockSpec(a.shape, lambda i: (0, 0), pipeline_mode=pl.Buffered(1))
    return pl.pallas_call(
        _block_out_kernel,
        grid=(m // tm,),
        in_specs=[row(D_MODEL), row(M_WIDTH), row(A_WIDTH), full(wo), full(g1), full(b1),
                  full(wg), full(wu), full(wd), full(g2), full(b2)],
        out_specs=row(D_MODEL),
        out_shape=jax.ShapeDtypeStruct((m, D_MODEL), F32),
        compiler_params=_cparams(("parallel",)),
        name="block_out",
    )(x, hm, ha, wo, g1, b1, wg, wu, wd, g2, b2)


def _split3(x):
    h1 = x.astype(BF16)
    r1 = x - h1.astype(F32)
    h2 = r1.astype(BF16)
    h3 = (r1 - h2.astype(F32)).astype(BF16)
    return h1, h2, h3


def _dot3(x, w):
    h1, h2, h3 = _split3(x)
    return _dot(h1, w) + _dot(h2, w) + _dot(h3, w)


def _mlstm_chunk(q, k, v, lf_row, i_row, i_col, c_aug, m_prev):
    L = q.shape[0]
    t_idx = lax.broadcasted_iota(jnp.int32, (L, L), 0)
    s_idx = lax.broadcasted_iota(jnp.int32, (L, L), 1)
    causal = s_idx <= t_idx
    ones_b = jnp.ones((L, LANES), BF16)
    G = _dot3(jnp.where(causal, lf_row, 0.0), ones_b)
    tri_u = jnp.where(t_idx <= s_idx, 1.0, 0.0).astype(BF16)
    f_row = _dot3(jnp.broadcast_to(lf_row, (SUBLANES, L)), tri_u)[0:1, :]
    f_col = G[:, 0:1]
    dlog = jnp.where(causal, f_col - f_row + i_row, NEG_INF)
    inter = m_prev + f_col
    m_t = jnp.maximum(jnp.max(dlog, axis=1, keepdims=True), inter)
    w = jnp.exp(dlog - m_t)
    g = jnp.exp(inter - m_t)
    s = _dot_nt(q, k) * w
    lane = lax.broadcasted_iota(jnp.int32, (L, LANES), 1)
    v_aug = jnp.concatenate([v, jnp.where(lane == 0, 1.0, 0.0).astype(BF16)], axis=1)
    nd = _dot(s.astype(BF16), v_aug) + g * _dot(q, c_aug.astype(BF16))
    den = nd[:, M_HEAD_DIM:M_HEAD_DIM + 1]
    h = nd[:, 0:M_HEAD_DIM] / jnp.maximum(jnp.abs(den), jnp.exp(-m_t))
    m_new = m_t[L - 1:L, :]
    g_c = g[L - 1:L, :]
    wk = jnp.exp(f_col[L - 1:L, :] - f_col + i_col - m_new)
    kw = (k.astype(F32) * wk).astype(BF16)
    c_new = g_c * c_aug + _dot_tn(kw, v_aug)
    return h, c_new, m_new


def _mlstm_prompt_kernel(*refs, nb):
    mqkv_ref, so_ref, g_ref = refs[0:3]
    gt_refs = refs[3:3 + nb]
    hm_ref, c_ref, m_ref = refs[3 + nb:]
    step = pl.program_id(0)

    @pl.when(step == 0)
    def _():
        c_ref[...] = jnp.zeros_like(c_ref)
        m_ref[...] = jnp.zeros_like(m_ref)

    for b in range(nb):
        for h in range(M_HEADS):
            hs = slice(h * M_HEAD_DIM, (h + 1) * M_HEAD_DIM)
            q = mqkv_ref[b, :, hs]
            k = mqkv_ref[b, :, M_WIDTH + h * M_HEAD_DIM:M_WIDTH + (h + 1) * M_HEAD_DIM]
            v = mqkv_ref[b, :, 2 * M_WIDTH + h * M_HEAD_DIM:2 * M_WIDTH + (h + 1) * M_HEAD_DIM]
            i_row = gt_refs[b][h:h + 1, :]
            lf_row = gt_refs[b][M_HEADS + h:M_HEADS + h + 1, :]
            i_col = g_ref[b, :, L_IG + h:L_IG + h + 1]
            r = b * M_HEADS + h
            m_prev = m_ref[r:r + 1, 0:1]
            hh, c_new, m_new = _mlstm_chunk(q, k, v, lf_row, i_row, i_col, c_ref[b, h], m_prev)
            c_ref[b, h] = c_new
            m_ref[r:r + 1, :] = jnp.broadcast_to(m_new, (1, LANES))
            hm_ref[b, :, hs] = (hh * so_ref[b, :, hs]).astype(BF16)


def _mlstm_prompt(mqkv, so, gates, gatest, nb, seq, chunk):
    nc = seq // chunk
    blk = lambda w: pl.BlockSpec((nb, chunk, w), lambda c: (0, c, 0))
    gt_specs = [pl.BlockSpec((2 * M_HEADS, chunk), lambda c, b=b: (0, b * nc + c)) for b in range(nb)]
    return pl.pallas_call(
        functools.partial(_mlstm_prompt_kernel, nb=nb),
        grid=(nc,),
        in_specs=[blk(3 * M_WIDTH), blk(M_WIDTH), blk(LANES)] + gt_specs,
        out_specs=[blk(M_WIDTH),
                   pl.BlockSpec((nb, M_HEADS, M_HEAD_DIM, 2 * M_HEAD_DIM), lambda c: (0, 0, 0, 0)),
                   pl.BlockSpec((nb * M_HEADS, LANES), lambda c: (0, 0))],
        out_shape=[jax.ShapeDtypeStruct((nb, seq, M_WIDTH), BF16),
                   jax.ShapeDtypeStruct((nb, M_HEADS, M_HEAD_DIM, 2 * M_HEAD_DIM), F32),
                   jax.ShapeDtypeStruct((nb * M_HEADS, LANES), F32)],
        compiler_params=_cparams(("arbitrary",)),
        name="mlstm_prompt",
    )(mqkv, so, gates, *([gatest] * nb))


ROWS_PAD = SUBLANES


def _mlstm_sample_kernel(mqkv_ref, so_ref, g_ref, c0_ref, n0_ref, m0_ref, hm_ref, c_ref, n_ref, m_ref, *, nseq, t_real):
    row = lax.broadcasted_iota(jnp.int32, (ROWS_PAD, 1), 0)
    real = row < t_real

    def per_seq(s, carry):
        r0 = pl.multiple_of(s * ROWS_PAD, ROWS_PAD)
        gt = g_ref[pl.ds(r0, ROWS_PAD), :]
        cum = gt
        for d in range(1, t_real):
            cum = cum + jnp.where(row >= d, pltpu.roll(gt, d, 0), 0.0)
        for h in range(M_HEADS):
            hs = slice(h * M_HEAD_DIM, (h + 1) * M_HEAD_DIM)
            q = mqkv_ref[pl.ds(r0, ROWS_PAD), hs]
            k = mqkv_ref[pl.ds(r0, ROWS_PAD), M_WIDTH + h * M_HEAD_DIM:M_WIDTH + (h + 1) * M_HEAD_DIM]
            v = mqkv_ref[pl.ds(r0, ROWS_PAD), 2 * M_WIDTH + h * M_HEAD_DIM:2 * M_WIDTH + (h + 1) * M_HEAD_DIM]
            qf, kf, vf = q.astype(F32), k.astype(F32), v.astype(F32)
            i_col = gt[:, L_IG + h:L_IG + h + 1]
            f_col = cum[:, L_FG + h:L_FG + h + 1]
            c0 = c0_ref[s, h]
            n0 = n0_ref[s, h:h + 1, :]
            m0 = m0_ref[pl.ds(s, 1), h:h + 1]
            inter = m0 + f_col
            dl = [jnp.where(real & (row >= u), f_col - f_col[u:u + 1, :] + i_col[u:u + 1, :], NEG_INF)
                  for u in range(t_real)]
            m_t = inter
            for u in range(t_real):
                m_t = jnp.maximum(m_t, dl[u])
            g = jnp.exp(inter - m_t)
            qc = _dot(q, c0.astype(BF16))
            num = g * qc
            den = g * jnp.sum(qf * n0, axis=1, keepdims=True)
            for u in range(t_real):
                su = jnp.sum(qf * kf[u:u + 1, :], axis=1, keepdims=True) * jnp.exp(dl[u] - m_t)
                num = num + su * vf[u:u + 1, :]
                den = den + su
            hh = num / jnp.maximum(jnp.abs(den), jnp.exp(-m_t))
            hm_ref[pl.ds(r0, ROWS_PAD), hs] = (hh * so_ref[pl.ds(r0, ROWS_PAD), hs]).astype(BF16)
            last = t_real - 1
            m_new = m_t[last:last + 1, :]
            g_c = g[last:last + 1, :]
            wk = jnp.where(real, jnp.exp(f_col[last:last + 1, :] - f_col + i_col - m_new), 0.0)
            kw = kf * wk
            c_ref[s, h] = g_c * c0 + _dot_tn(kw.astype(BF16), v)
            n_ref[s, h:h + 1, :] = g_c * n0 + jnp.sum(kw, axis=0, keepdims=True)
            m_ref[pl.ds(s, 1), h:h + 1] = m_new
        return carry

    lax.fori_loop(0, nseq, per_seq, 0)


def _mlstm_sample(mqkv, so, gates, c0, n0, m0, nseq_blk, t_real):
    nseq = c0.shape[0]
    rows = nseq_blk * ROWS_PAD
    rblk = lambda w: pl.BlockSpec((rows, w), lambda i: (i, 0))
    cblk = pl.BlockSpec((nseq_blk, M_HEADS, M_HEAD_DIM, M_HEAD_DIM), lambda i: (i, 0, 0, 0))
    nblk = pl.BlockSpec((nseq_blk, M_HEADS, M_HEAD_DIM), lambda i: (i, 0, 0))
    mblk = pl.BlockSpec((nseq_blk, M_HEADS), lambda i: (i, 0))
    return pl.pallas_call(
        functools.partial(_mlstm_sample_kernel, nseq=nseq_blk, t_real=t_real),
        grid=(nseq // nseq_blk,),
        in_specs=[rblk(3 * M_WIDTH), rblk(M_WIDTH), rblk(LANES), cblk, nblk, mblk],
        out_specs=[rblk(M_WIDTH), cblk, nblk, mblk],
        out_shape=[jax.ShapeDtypeStruct((nseq * ROWS_PAD, M_WIDTH), BF16),
                   jax.ShapeDtypeStruct(c0.shape, F32),
                   jax.ShapeDtypeStruct(n0.shape, F32),
                   jax.ShapeDtypeStruct(m0.shape, F32)],
        compiler_params=_cparams(("parallel",)),
        name="mlstm_sample",
    )(mqkv, so, gates, c0, n0, m0)


MAX_BISECT = 20
MASKED = -1e30
ROW_BLOCK = 128
SELECT_TK = 256
NORM_TK = 1024
MIN_ROW_SUM = 2.0 ** -100


def _row_pass(sc_ref, nch, tk, init, fn, params=()):
    rows = sc_ref.shape[0]
    rb = min(ROW_BLOCK, rows)
    outs = []
    for r0 in range(0, rows, rb):
        rsl = lambda t: jax.tree.map(lambda a: a[r0:r0 + rb], t)
        prm = rsl(params)

        def body(c, acc, r0=r0, prm=prm):
            c0 = pl.multiple_of(c * tk, tk)
            blk = sc_ref[r0:r0 + rb, pl.ds(c0, tk)]
            for j in range(tk // LANES):
                acc = fn(acc, blk[:, j * LANES:(j + 1) * LANES], c0 + j * LANES, prm)
            return acc

        outs.append(lax.fori_loop(0, nch, body, rsl(init)))
    return jax.tree.map(lambda *a: jnp.concatenate(a, axis=0), *outs)


def _select_threshold(sc_ref, nch, tk, kr, smax, stats=None):
    rows = sc_ref.shape[0]
    zeros = jnp.zeros((rows, LANES), F32)
    lsum = lambda a: jnp.sum(a, axis=1, keepdims=True)
    full = lambda col: jnp.broadcast_to(col, (rows, LANES))

    def count_ge(thr):
        return lsum(_row_pass(sc_ref, nch, tk, zeros, lambda a, x, _, t: a + jnp.where(x >= t, 1.0, 0.0), full(thr)))

    def max_below(thr):
        acc = _row_pass(sc_ref, nch, tk, jnp.full((rows, LANES), NEG_INF, F32),
                        lambda a, x, _, t: jnp.maximum(a, jnp.where(x < t, x, NEG_INF)), full(thr))
        return jnp.max(acc, axis=1, keepdims=True)

    def range_fn(a, x, _, prm):
        fin = x > NEG_INF
        return (jnp.maximum(a[0], x), jnp.minimum(a[1], jnp.where(fin, x, jnp.inf)), a[2] + jnp.where(fin, 1.0, 0.0))

    if stats is None:
        mx, mn, nf = _row_pass(
            sc_ref, nch, tk, (jnp.full((rows, LANES), NEG_INF, F32), jnp.full((rows, LANES), jnp.inf, F32), zeros),
            range_fn)
        rmax, rmin, n_fin = jnp.max(mx, axis=1, keepdims=True), jnp.min(mn, axis=1, keepdims=True), lsum(nf)
    else:
        rmax, rmin, n_fin = stats

    above = rmax + jnp.maximum(rmax - rmin, jnp.maximum(jnp.abs(rmax) * (2.0 ** -10), 1.0))

    def bisect_cond(st):
        it, _, _, c_lo, _ = st
        return jnp.logical_and(it < MAX_BISECT, jnp.max(c_lo - kr) > 0.5)

    def bisect(st):
        it, lo, hi, c_lo, c_hi = st
        mid = 0.5 * (lo + hi)
        c = count_ge(mid)
        ge = c >= kr
        return (it + 1, jnp.where(ge, mid, lo), jnp.where(ge, hi, mid), jnp.where(ge, c, c_lo), jnp.where(ge, c_hi, c))

    _, lo, hi, c_lo, c_hi = lax.while_loop(bisect_cond, bisect, (jnp.int32(0), rmin, above, n_fin, jnp.zeros_like(rmax)))

    done0 = jnp.where(c_lo - kr > 0.5, 0.0, 1.0)

    def peel_cond(st):
        it, _, _, _, _, _, done = st
        return jnp.logical_and(it <= smax, jnp.min(done) < 0.5)

    def peel(st):
        it, hi, c_hi, tau, c_gt, c_ge, done = st
        t1 = max_below(hi)
        c1 = count_ge(t1)
        fin = jnp.logical_and(c1 >= kr, done < 0.5)
        tau = jnp.where(fin, t1, tau)
        c_gt = jnp.where(fin, c_hi, c_gt)
        c_ge = jnp.where(fin, c1, c_ge)
        done = jnp.where(fin, 1.0, done)
        live = done < 0.5
        return it + 1, jnp.where(live, t1, hi), jnp.where(live, c1, c_hi), tau, c_gt, c_ge, done

    st = (jnp.int32(0), hi, c_hi, lo, jnp.zeros_like(rmax), kr, done0)
    _, _, _, tau, c_gt, c_ge, _ = lax.while_loop(peel_cond, peel, st)

    need = kr - c_gt
    surplus = (c_ge - c_gt) - need
    rb = min(ROW_BLOCK, rows)
    k_idx = lax.broadcasted_iota(jnp.int32, (LANES, 2 * LANES), 0)
    j_idx = lax.broadcasted_iota(jnp.int32, (LANES, 2 * LANES), 1)
    tri_ones = jnp.where(jnp.logical_or(k_idx <= j_idx, j_idx >= LANES), 1.0, 0.0).astype(BF16)
    for r0 in range(0, rows, rb):
        @pl.when(jnp.max(surplus[r0:r0 + rb]) > 0.5)
        def _(r0=r0):
            tau_f = jnp.broadcast_to(tau[r0:r0 + rb], (rb, LANES))
            need_f = jnp.broadcast_to(need[r0:r0 + rb], (rb, LANES))

            def sweep(c, seen):
                for j in range(tk // LANES):
                    c0 = pl.multiple_of(c * tk + j * LANES, LANES)
                    x = sc_ref[r0:r0 + rb, pl.ds(c0, LANES)]
                    tied = x == tau_f
                    cnt = _dot(jnp.where(tied, 1.0, 0.0).astype(BF16), tri_ones)
                    rank = seen + cnt[:, 0:LANES]
                    sc_ref[r0:r0 + rb, pl.ds(c0, LANES)] = jnp.where(jnp.logical_and(tied, rank > need_f), NEG_INF, x)
                    seen = seen + cnt[:, LANES:2 * LANES]
                return seen

            lax.fori_loop(0, nch, sweep, jnp.zeros((rb, LANES), F32))

    return tau


def _ones_rows(tk):
    return jnp.where(lax.broadcasted_iota(jnp.int32, (LANES, tk), 0) == 0, 1.0, 0.0).astype(BF16)


def _dsa_prompt_kernel(qi_ref, g_ref, q_ref, kit_ref, kt_ref, vt_ref, o_ref, sc_ref, tau_scr, m_scr, acc_scr, kn_scr,
                       *, tq, tk, topk, smax):
    i = pl.program_id(1)
    nch = ((i + 1) * tq + tk - 1) // tk
    row = lax.broadcasted_iota(jnp.int32, (tq, 1), 0) + i * tq

    m_scr[0] = jnp.full((tq, LANES), NEG_INF, F32)
    m_scr[1] = jnp.full((tq, LANES), jnp.inf, F32)

    def score_chunk(c, carry):
        c0 = pl.multiple_of(c * tk, tk)
        kblk = kit_ref[:, pl.ds(c0, tk)]
        acc = jnp.zeros((tq, tk), F32)
        for h in range(IDX_HEADS):
            x = _dot(qi_ref[:, h * LANES:(h + 1) * LANES], kblk)
            acc = acc + jnp.maximum(x, 0.0) * g_ref[:, L_WI + h:L_WI + h + 1]
        col = lax.broadcasted_iota(jnp.int32, (1, LANES), 1) + c0
        hi_p, lo_p = m_scr[0], m_scr[1]
        for j in range(tk // LANES):
            causal = col + j * LANES <= row
            a = acc[:, j * LANES:(j + 1) * LANES]
            masked = jnp.where(causal, a, NEG_INF)
            sc_ref[:, pl.ds(pl.multiple_of(c0 + j * LANES, LANES), LANES)] = masked
            hi_p = jnp.maximum(hi_p, masked)
            lo_p = jnp.minimum(lo_p, jnp.where(causal, a, jnp.inf))
        m_scr[0], m_scr[1] = hi_p, lo_p
        return carry

    lax.fori_loop(0, nch, score_chunk, 0)

    kr = jnp.minimum(row + 1, topk).astype(F32)
    stats = (jnp.max(m_scr[0], axis=1, keepdims=True), jnp.min(m_scr[1], axis=1, keepdims=True), (row + 1).astype(F32))
    tau = _select_threshold(sc_ref, nch * (tk // SELECT_TK), SELECT_TK, kr, smax, stats)

    nslab = tk // LANES
    tau_scr[...] = jnp.broadcast_to(tau, (tq, LANES))

    def masked_logits(c0, h):
        pr = slice((h // 2) * LANES, (h // 2 + 1) * LANES)
        s = _dot(q_ref[:, h * LANES:(h + 1) * LANES], kt_ref[pr, pl.ds(c0, tk)])
        x = sc_ref[:, pl.ds(c0, tk)]
        thr = tau_scr[...]
        return [jnp.where(x[:, j * LANES:(j + 1) * LANES] >= thr, s[:, j * LANES:(j + 1) * LANES], MASKED)
                for j in range(nslab)]

    def max_sweep():
        m_scr[...] = jnp.full(m_scr.shape, MASKED, F32)

        def max_chunk(c, carry):
            c0 = pl.multiple_of(c * tk, tk)
            for h in range(A_HEADS):
                slabs = masked_logits(c0, h)
                part = slabs[0]
                for j in range(1, nslab):
                    part = jnp.maximum(part, slabs[j])
                m_scr[h] = jnp.maximum(m_scr[h], part)
            return carry

        lax.fori_loop(0, nch, max_chunk, 0)
        for h in range(A_HEADS):
            m_scr[h] = jnp.broadcast_to(jnp.max(m_scr[h], axis=1, keepdims=True), (tq, LANES))

    ones = _ones_rows(tk)

    def attend_sweep():
        acc_scr[...] = jnp.zeros(acc_scr.shape, F32)

        def attend_chunk(c, carry):
            c0 = pl.multiple_of(c * tk, tk)
            for h in range(A_HEADS):
                pr = slice((h // 2) * LANES, (h // 2 + 1) * LANES)
                m_h = m_scr[h]
                pb = jnp.concatenate([jnp.exp2(sl - m_h).astype(BF16) for sl in masked_logits(c0, h)], axis=1)
                vt_aug = jnp.concatenate([vt_ref[pr, pl.ds(c0, tk)], ones], axis=0)
                acc_scr[h] = acc_scr[h] + _dot_nt(pb, vt_aug)
            return carry

        lax.fori_loop(0, nch, attend_chunk, 0)

    @pl.when(i == 0)
    def _():
        for h in range(A_HEADS):
            def norm_chunk(c, best, h=h):
                c0 = pl.multiple_of(c * NORM_TK, NORM_TK)
                kk = kt_ref[h * A_HEAD_DIM:(h + 1) * A_HEAD_DIM, pl.ds(c0, NORM_TK)].astype(F32)
                return jnp.maximum(best, jnp.sum(kk * kk, axis=0, keepdims=True))
            best = lax.fori_loop(0, smax // NORM_TK, norm_chunk, jnp.zeros((1, NORM_TK), F32))
            kn_scr[h:h + 1, :] = jnp.broadcast_to(jnp.max(best, axis=1, keepdims=True), (1, LANES))

    for h in range(A_HEADS):
        qh = q_ref[:, h * LANES:(h + 1) * LANES].astype(F32)
        qn2 = jnp.sum(qh * qh, axis=1, keepdims=True)
        m_scr[h] = jnp.broadcast_to(jnp.sqrt(qn2 * kn_scr[h:h + 1, 0:1]), (tq, LANES))
    attend_sweep()
    l_min = acc_scr[0][:, LANES:LANES + 1]
    for h in range(1, A_HEADS):
        l_min = jnp.minimum(l_min, acc_scr[h][:, LANES:LANES + 1])

    @pl.when(jnp.logical_not(jnp.min(l_min) >= MIN_ROW_SUM))
    def _():
        max_sweep()
        attend_sweep()

    lo_half = lax.broadcasted_iota(jnp.int32, (1, LANES), 1) < A_HEAD_DIM
    for p in range(A_HEADS // 2):
        a_e, a_o = acc_scr[2 * p], acc_scr[2 * p + 1]
        even = a_e[:, 0:LANES] / a_e[:, LANES:LANES + 1]
        odd = a_o[:, 0:LANES] / a_o[:, LANES:LANES + 1]
        o_ref[:, p * LANES:(p + 1) * LANES] = jnp.where(lo_half, even, odd).astype(BF16)


def _dsa_prompt(qipad, gates, qpad, kit2, ktb, vtb, nb, seq, tq, tk):
    nq = seq // tq
    topk = min(TOPK_MAX, seq // 4)
    qrow = lambda w: pl.BlockSpec((tq, w), lambda b, i: (b * nq + i, 0))
    kfull = lambda r: pl.BlockSpec((None, r, seq), lambda b, i: (b, 0, 0), pipeline_mode=pl.Buffered(1))
    return pl.pallas_call(
        functools.partial(_dsa_prompt_kernel, tq=tq, tk=tk, topk=topk, smax=seq),
        grid=(nb, nq),
        in_specs=[qrow(2 * A_WIDTH), qrow(LANES), qrow(2 * A_WIDTH), kfull(2 * IDX_DIM), kfull(A_WIDTH), kfull(A_WIDTH)],
        out_specs=qrow(A_WIDTH),
        out_shape=jax.ShapeDtypeStruct((nb * seq, A_WIDTH), BF16),
        scratch_shapes=[pltpu.VMEM((tq, seq + LANES), F32),
                        pltpu.VMEM((tq, LANES), F32),
                        pltpu.VMEM((A_HEADS, tq, LANES), F32),
                        pltpu.VMEM((A_HEADS, tq, 2 * LANES), F32),
                        pltpu.VMEM((A_HEADS, LANES), F32)],
        compiler_params=_cparams(("parallel", "arbitrary")),
        name="dsa_prompt",
    )(qipad, gates, qpad, kit2, ktb, vtb)


def _dsa_sample_scores_kernel(pt_ref, q_ref, w_ref, *refs, n_pages, t_real):
    page_refs, new_ref, o_ref = refs[:n_pages], refs[n_pages], refs[n_pages + 1]
    q = q_ref[0]
    w = w_ref[0]
    trow = lax.broadcasted_iota(jnp.int32, (t_real, 1), 0)
    for p in range(n_pages + 1):
        keys_t = page_refs[p][0].astype(BF16) if p < n_pages else new_ref[0]
        r = jnp.maximum(_dot(q, keys_t), 0.0) * w
        sc = jnp.sum(r.reshape(t_real, IDX_HEADS, PAGE_SIZE), axis=1)
        if p == n_pages:
            col = lax.broadcasted_iota(jnp.int32, (1, PAGE_SIZE), 1)
            sc = jnp.where(col <= trow, sc, NEG_INF)
        o_ref[0, :, p * PAGE_SIZE:(p + 1) * PAGE_SIZE] = sc


def _dsa_sample_scores(page_table, qi, wi, cache_kidx_t, ki_new_t, t_real):
    nseq, n_pages = page_table.shape
    ncol = (n_pages + 1) * PAGE_SIZE
    per_seq = lambda a: pl.BlockSpec((1,) + a.shape[1:], lambda b, pt: (b, 0, 0))
    page = lambda p: pl.BlockSpec((1, IDX_DIM, PAGE_SIZE), lambda b, pt, p=p: (pt[b * n_pages + p], 0, 0))
    return pl.pallas_call(
        functools.partial(_dsa_sample_scores_kernel, n_pages=n_pages, t_real=t_real),
        grid_spec=pltpu.PrefetchScalarGridSpec(
            num_scalar_prefetch=1,
            grid=(nseq,),
            in_specs=[per_seq(qi), per_seq(wi)] + [page(p) for p in range(n_pages)] + [per_seq(ki_new_t)],
            out_specs=pl.BlockSpec((1, t_real, ncol), lambda b, pt: (b, 0, 0)),
        ),
        out_shape=jax.ShapeDtypeStruct((nseq, t_real, ncol), F32),
        compiler_params=_cparams(("parallel",)),
        name="dsa_sample_scores",
    )(page_table.reshape(-1), qi, wi, *([cache_kidx_t] * n_pages), ki_new_t)


def _dsa_sample_select_kernel(sc_ref, adj_ref, tau_ref, *, tk, topk):
    adj_ref[...] = sc_ref[...]
    rows, ncol = adj_ref.shape
    kr = jnp.full((rows, 1), float(topk), F32)
    tau = _select_threshold(adj_ref, ncol // tk, tk, kr, ncol)
    tau_ref[...] = jnp.broadcast_to(tau, (rows, LANES))


def _dsa_sample_select(sc, topk, rblk):
    rows, ncol = sc.shape
    return pl.pallas_call(
        functools.partial(_dsa_sample_select_kernel, tk=LANES, topk=topk),
        grid=(rows // rblk,),
        in_specs=[pl.BlockSpec((rblk, ncol), lambda i: (i, 0))],
        out_specs=[pl.BlockSpec((rblk, ncol), lambda i: (i, 0)), pl.BlockSpec((rblk, LANES), lambda i: (i, 0))],
        out_shape=[jax.ShapeDtypeStruct((rows, ncol), F32), jax.ShapeDtypeStruct((rows, LANES), F32)],
        compiler_params=_cparams(("parallel",)),
        name="dsa_sample_select",
    )(sc)


def _dsa_sample_attend_kernel(pt_ref, q_ref, sc_ref, tau_ref, *refs, n_pages, t_real):
    kp, vp = refs[:n_pages], refs[n_pages:2 * n_pages]
    knew_ref, vnew_ref, o_ref, s_scr = refs[2 * n_pages:]
    q = q_ref[0]
    nrow = t_real * A_HEADS
    rows_of = lambda a: jnp.broadcast_to(a[:, None, :], (t_real, A_HEADS, a.shape[-1])).reshape(nrow, a.shape[-1])
    tau = rows_of(tau_ref[0][:, 0:1])
    for p in range(n_pages + 1):
        kt = kp[p][0].reshape(A_WIDTH, PAGE_SIZE).astype(BF16) if p < n_pages else knew_ref[0]
        x = rows_of(sc_ref[0, :, p * PAGE_SIZE:(p + 1) * PAGE_SIZE])
        s_scr[:, p * PAGE_SIZE:(p + 1) * PAGE_SIZE] = jnp.where(x >= tau, _dot(q, kt), MASKED)
    s_all = s_scr[...]
    pr = jnp.exp2(s_all - jnp.max(s_all, axis=1, keepdims=True))
    l = jnp.sum(pr, axis=1, keepdims=True)
    s_scr[...] = pr
    acc = jnp.zeros((nrow, A_WIDTH), F32)
    for p in range(n_pages + 1):
        vt = vp[p][0].reshape(A_WIDTH, PAGE_SIZE).astype(BF16) if p < n_pages else vnew_ref[0]
        acc = acc + _dot_nt(s_scr[:, p * PAGE_SIZE:(p + 1) * PAGE_SIZE].astype(BF16), vt)
    out = acc / l
    head_of_row = lax.broadcasted_iota(jnp.int32, (nrow, 1), 0) % A_HEADS
    head_of_lane = lax.broadcasted_iota(jnp.int32, (1, A_WIDTH), 1) // A_HEAD_DIM
    out = jnp.where(head_of_row == head_of_lane, out, 0.0)
    o_ref[0] = jnp.sum(out.reshape(t_real, A_HEADS, A_WIDTH), axis=1).astype(BF16)


def _dsa_sample_attend(page_table, qbd, sc_adj, tau, cache_k_t, cache_v_t, k_new_t, v_new_t, t_real):
    nseq, n_pages = page_table.shape
    ncol = (n_pages + 1) * PAGE_SIZE
    per_seq = lambda a: pl.BlockSpec((1,) + a.shape[1:], lambda b, pt: (b, 0, 0))
    page = lambda p: pl.BlockSpec((1, A_HEADS, A_HEAD_DIM, PAGE_SIZE), lambda b, pt, p=p: (pt[b * n_pages + p], 0, 0, 0))
    pages = lambda: [page(p) for p in range(n_pages)]
    return pl.pallas_call(
        functools.partial(_dsa_sample_attend_kernel, n_pages=n_pages, t_real=t_real),
        grid_spec=pltpu.PrefetchScalarGridSpec(
            num_scalar_prefetch=1,
            grid=(nseq,),
            in_specs=[per_seq(qbd), per_seq(sc_adj), per_seq(tau)] + pages() + pages() + [per_seq(k_new_t), per_seq(v_new_t)],
            out_specs=pl.BlockSpec((1, t_real, A_WIDTH), lambda b, pt: (b, 0, 0)),
            scratch_shapes=[pltpu.VMEM((t_real * A_HEADS, ncol), F32)],
        ),
        out_shape=jax.ShapeDtypeStruct((nseq, t_real, A_WIDTH), BF16),
        compiler_params=_cparams(("parallel",)),
        name="dsa_sample_attend",
    )(page_table.reshape(-1), qbd, sc_adj, tau, *([cache_k_t] * n_pages), *([cache_v_t] * n_pages), k_new_t, v_new_t)


PROJ_TM = 256
OUT_TM = 256
MLSTM_CHUNK = 128
DSA_TQ = 256
DSA_TK = 512
SAMPLE_SEQ_BLK = 8
SELECT_ROWS = 128


def _unpad_heads(xpad, n_heads):
    x = xpad.reshape(xpad.shape[:-1] + (n_heads, 2, LANES // 2))
    return x[..., 0, :] + x[..., 1, :]


def kernel(x_prompt, x_sample, cache_k, cache_v, cache_kidx, state_C, state_n, state_m, page_table,
           w_in, b_gate, w_out, ln1_g, ln1_b, w_gate, w_up, w_down, ln2_g, ln2_b):
    bp, sp, _ = x_prompt.shape
    bs, ts, _ = x_sample.shape
    n_pages = page_table.shape[1]
    past = n_pages * PAGE_SIZE
    assert DEPTH == 1 and w_in.shape[0] == 1

    w = _prep_w_in(w_in[0], b_gate[0])
    wo, wg, wu, wd = (a[0].astype(BF16) for a in (w_out, w_gate, w_up, w_down))
    g1, b1, g2, b2 = (v[0].astype(F32)[None, :] for v in (ln1_g, ln1_b, ln2_g, ln2_b))

    mp = bp * sp
    xp = x_prompt.reshape(mp, D_MODEL)
    tabs_p, tabs_pt = _rope_tables(jnp.arange(sp, dtype=jnp.int32))
    (mqkv, so, gates, gatest, qpad, qipad, kt_p, ktb, vt_p, vtb, kit_p, kit2) = _proj(
        xp, w, tabs_p, tabs_pt, PROJ_TM, bp, sp)
    hm_p, caug, m_p = _mlstm_prompt(mqkv.reshape(bp, sp, -1), so.reshape(bp, sp, -1), gates.reshape(bp, sp, -1),
                                    gatest, bp, sp, MLSTM_CHUNK)
    ha_p = _dsa_prompt(qipad, gates, qpad, kit2, ktb, vtb, bp, sp, DSA_TQ, DSA_TK)
    y_p = _block_out(xp, hm_p.reshape(mp, -1), ha_p, wo, g1, b1, wg, wu, wd, g2, b2, OUT_TM)

    ms = bs * ts
    xs = x_sample.reshape(ms, D_MODEL)
    tabs_s, tabs_st = _rope_tables(jnp.tile(past + jnp.arange(ts, dtype=jnp.int32), bs))
    (mqkv_s, so_s, gates_s, _, qpad_s, qipad_s, kt_s, ktb_s, vt_s, vtb_s, kit_s, _) = _proj(
        xs, w, tabs_s, tabs_st, ms, 1, ms)
    pad_rows = lambda a: jnp.pad(a.reshape(bs, ts, -1), ((0, 0), (0, ROWS_PAD - ts), (0, 0))).reshape(bs * ROWS_PAD, -1)
    hm_s, c_s, n_s, m_s = _mlstm_sample(pad_rows(mqkv_s), pad_rows(so_s), pad_rows(gates_s),
                                        state_C[0].astype(F32), state_n[0].astype(F32), state_m[0].astype(F32),
                                        SAMPLE_SEQ_BLK, ts)
    hm_s = hm_s.reshape(bs, ROWS_PAD, -1)[:, :ts].reshape(ms, -1)

    new_page = lambda a: jnp.pad(a.reshape(a.shape[0], bs, ts).transpose(1, 0, 2), ((0, 0), (0, 0), (0, PAGE_SIZE - ts)))
    qi_s = _unpad_heads(qipad_s, IDX_HEADS).reshape(bs, ts * IDX_HEADS, IDX_DIM)
    wi_s = gates_s[:, L_WI:L_WI + IDX_HEADS].reshape(bs, ts * IDX_HEADS, 1)
    sc_s = _dsa_sample_scores(page_table, qi_s, wi_s, cache_kidx[0].transpose(0, 2, 1), new_page(kit_s[0]).astype(BF16), ts)
    ncol = sc_s.shape[-1]
    topk_s = min(TOPK_MAX, (past + ts) // 4)
    sc_adj, tau_s = _dsa_sample_select(sc_s.reshape(ms, ncol), topk_s, SELECT_ROWS)
    q_s = _unpad_heads(qpad_s, A_HEADS).reshape(bs, ts, A_HEADS, A_HEAD_DIM)
    eye = jnp.eye(A_HEADS, dtype=q_s.dtype)
    qbd = (q_s[:, :, :, None, :] * eye[None, None, :, :, None]).reshape(bs, ts * A_HEADS, A_WIDTH)
    ha_s = _dsa_sample_attend(page_table, qbd, sc_adj.reshape(bs, ts, ncol), tau_s.reshape(bs, ts, LANES),
                              cache_k[0].transpose(0, 2, 3, 1), cache_v[0].transpose(0, 2, 3, 1),
                              new_page(ktb_s[0]), new_page(vtb_s[0]), ts)
    y_s = _block_out(xs, hm_s, ha_s.reshape(ms, -1), wo, g1, b1, wg, wu, wd, g2, b2, min(OUT_TM, ms))

    heads = lambda a, b, t: a.reshape(1, b, A_HEADS, A_HEAD_DIM, t).transpose(0, 1, 4, 2, 3)
    heads_s = lambda a: a[0].T.reshape(1, bs, ts, A_HEADS, A_HEAD_DIM)
    return (y_p.reshape(bp, sp, D_MODEL), y_s.reshape(bs, ts, D_MODEL),
            heads(kt_p, bp, sp), heads(vt_p, bp, sp), kit_p.transpose(0, 2, 1)[None],
            caug[None, :, :, :, 0:M_HEAD_DIM], caug[None, :, :, :, M_HEAD_DIM], m_p[:, 0].reshape(1, bp, M_HEADS),
            heads_s(kt_s), heads_s(vt_s), kit_s[0].T.reshape(1, bs, ts, IDX_DIM),
            c_s[None], n_s[None], m_s[None])
```

```python
import functools
import math

import jax
import jax.numpy as jnp
from jax import lax
from jax.experimental import pallas as pl
from jax.experimental.pallas import tpu as pltpu

D_MODEL = 1024
M_HEADS = 4
M_HEAD_DIM = 128
M_WIDTH = 512
A_HEADS = 8
A_HEAD_DIM = 64
A_WIDTH = 512
IDX_HEADS = 8
IDX_DIM = 64
TOPK_MAX = 256
PAGE_SIZE = 128
ROPE_THETA = 500000.0
ROT = A_HEAD_DIM // 4
HALF = ROT // 2
D_FF = 2816
DEPTH = 1
ALPHA = (2 * DEPTH) ** 0.25
LN_EPS = 1e-5
LOG2E = math.log2(math.e)

LANES = 128
SUBLANES = 8
VMEM_LIMIT = 56 * 1024 * 1024

NEG_INF = float("-inf")
BF16 = jnp.bfloat16
F32 = jnp.float32

C_MQ, C_MK, C_MV, C_MO = 0, 512, 1024, 1536
C_AQ, C_IQ, C_SM = 2048, 2560, 3072
N_PERM = 3200
L_WI, L_IG, L_FG = 64, 72, 76
R_AK, R_AV, R_IK, N_ROWS_T = 0, 512, 1024, 1152


def _cparams(sem):
    return pltpu.CompilerParams(dimension_semantics=sem, vmem_limit_bytes=VMEM_LIMIT)


def _dot(a, b):
    return jnp.dot(a, b, preferred_element_type=F32)


def _dot_nt(a, b):
    return lax.dot_general(a, b, (((1,), (1,)), ((), ())), preferred_element_type=F32)


def _dot_tn(a, b):
    return lax.dot_general(a, b, (((0,), (0,)), ((), ())), preferred_element_type=F32)


def _rope128(x, cos, sa, sb):
    return x * cos + pltpu.roll(x, LANES - HALF, 1) * sa + pltpu.roll(x, HALF, 1) * sb


def _rope_rows(z, cos_t, sin_t):
    a, b = z[0:HALF], z[HALF:ROT]
    return jnp.concatenate([a * cos_t - b * sin_t, b * cos_t + a * sin_t, z[ROT:]], axis=0)


def _proj_kernel(x_ref, w_ref, wt_ref, wgt_ref, bias_ref, biast_ref, cos_ref, sa_ref, sb_ref, cost_ref, sint_ref,
                 mqkv_ref, so_ref, gates_ref, gatest_ref, qpad_ref, qipad_ref,
                 kt_ref, ktb_ref, vt_ref, vtb_ref, kit_ref, kit2_ref):
    xb = x_ref[...].astype(BF16)
    cos, sa, sb = cos_ref[...], sa_ref[...], sb_ref[...]
    cos_t, sin_t = cost_ref[...], sint_ref[...]
    lane = lax.broadcasted_iota(jnp.int32, (1, LANES), 1)
    lo_half = lane < A_HEAD_DIM

    zm = _dot(xb, w_ref[:, C_MQ:C_MO])
    mqkv_ref[:, 0:M_WIDTH] = zm[:, 0:M_WIDTH].astype(BF16)
    mqkv_ref[:, M_WIDTH:2 * M_WIDTH] = (zm[:, M_WIDTH:2 * M_WIDTH] * (M_HEAD_DIM ** -0.5)).astype(BF16)
    mqkv_ref[:, 2 * M_WIDTH:3 * M_WIDTH] = zm[:, 2 * M_WIDTH:3 * M_WIDTH].astype(BF16)
    so_ref[...] = jax.nn.sigmoid(_dot(xb, w_ref[:, C_MO:C_AQ]))

    def padded_heads(z, scale, out_ref):
        for p in range(A_WIDTH // LANES):
            r = _rope128(z[:, p * LANES:(p + 1) * LANES], cos, sa, sb) * scale
            out_ref[:, (2 * p) * LANES:(2 * p + 1) * LANES] = jnp.where(lo_half, r, 0.0).astype(BF16)
            out_ref[:, (2 * p + 1) * LANES:(2 * p + 2) * LANES] = jnp.where(lo_half, 0.0, r).astype(BF16)

    padded_heads(_dot(xb, w_ref[:, C_AQ:C_IQ]), (A_HEAD_DIM ** -0.5) * LOG2E, qpad_ref)
    padded_heads(_dot(xb, w_ref[:, C_IQ:C_SM]), IDX_DIM ** -0.5, qipad_ref)

    zs = _dot(xb, w_ref[:, C_SM:N_PERM]) + bias_ref[...]
    is_wi = (lane >= L_WI) & (lane < L_IG)
    is_fg = (lane >= L_FG) & (lane < L_FG + M_HEADS)
    g = jnp.where(is_wi, zs * (IDX_HEADS ** -0.5), zs)
    gates_ref[...] = jnp.where(is_fg, jax.nn.log_sigmoid(zs), g)

    zt = _dot_nt(wgt_ref[...], xb) + biast_ref[...]
    row = lax.broadcasted_iota(jnp.int32, (2 * M_HEADS, 1), 0)
    gatest_ref[...] = jnp.where(row >= M_HEADS, jax.nn.log_sigmoid(zt), zt)

    zkt = _dot_nt(wt_ref[R_AK:R_AV, :], xb)
    for h in range(A_HEADS):
        r = _rope_rows(zkt[h * A_HEAD_DIM:(h + 1) * A_HEAD_DIM], cos_t, sin_t)
        kt_ref[h * A_HEAD_DIM:(h + 1) * A_HEAD_DIM, :] = r
        ktb_ref[h * A_HEAD_DIM:(h + 1) * A_HEAD_DIM, :] = r.astype(BF16)
    zvt = _dot_nt(wt_ref[R_AV:R_IK, :], xb)
    vt_ref[...] = zvt
    vtb_ref[...] = zvt.astype(BF16)
    zit = _dot_nt(wt_ref[R_IK:N_ROWS_T, :], xb)
    r = _rope_rows(zit[0:IDX_DIM], cos_t, sin_t)
    kit_ref[...] = r
    kit2_ref[...] = jnp.concatenate([r, r], axis=0).astype(BF16)


def _rope_tables(pos):
    inv = ROPE_THETA ** (-jnp.arange(HALF, dtype=F32) / HALF)
    ang = pos.astype(F32)[:, None] * inv[None, :]
    c, s = jnp.cos(ang), jnp.sin(ang)
    n = pos.shape[0]
    one = jnp.ones((n, A_HEAD_DIM - ROT), F32)
    zero = jnp.zeros((n, A_HEAD_DIM - ROT), F32)
    z8 = jnp.zeros((n, HALF), F32)
    cos64 = jnp.concatenate([c, c, one], axis=1)
    sa64 = jnp.concatenate([-s, z8, zero], axis=1)
    sb64 = jnp.concatenate([z8, s, zero], axis=1)
    two = lambda t: jnp.concatenate([t, t], axis=1)
    return (two(cos64), two(sa64), two(sb64)), (c.T, s.T)


def _proj(x, w, tabs, tabs_t, tm, nb, seq):
    wp, wt, wgt, bias, biast = w
    m = x.shape[0]
    nblk = seq // tm
    row = lambda wd: pl.BlockSpec((tm, wd), lambda i: (i, 0))
    full = lambda a: pl.BlockSpec(a.shape, lambda i: (0, 0))
    tab = pl.BlockSpec((tm, LANES), lambda i: (i % nblk, 0))
    tab_t = pl.BlockSpec((HALF, tm), lambda i: (0, i % nblk))
    fmaj = lambda r: pl.BlockSpec((None, r, tm), lambda i: (i // nblk, 0, i % nblk))
    outs = [
        ((m, 3 * M_WIDTH), BF16, row(3 * M_WIDTH)),
        ((m, M_WIDTH), F32, row(M_WIDTH)),
        ((m, LANES), F32, row(LANES)),
        ((2 * M_HEADS, m), F32, pl.BlockSpec((2 * M_HEADS, tm), lambda i: (0, i))),
        ((m, 2 * A_WIDTH), BF16, row(2 * A_WIDTH)),
        ((m, 2 * A_WIDTH), BF16, row(2 * A_WIDTH)),
        ((nb, A_WIDTH, seq), F32, fmaj(A_WIDTH)),
        ((nb, A_WIDTH, seq), BF16, fmaj(A_WIDTH)),
        ((nb, A_WIDTH, seq), F32, fmaj(A_WIDTH)),
        ((nb, A_WIDTH, seq), BF16, fmaj(A_WIDTH)),
        ((nb, IDX_DIM, seq), F32, fmaj(IDX_DIM)),
        ((nb, 2 * IDX_DIM, seq), BF16, fmaj(2 * IDX_DIM)),
    ]
    return pl.pallas_call(
        _proj_kernel,
        grid=(m // tm,),
        in_specs=[row(D_MODEL), full(wp), full(wt), full(wgt), full(bias), full(biast), tab, tab, tab, tab_t, tab_t],
        out_specs=[o[2] for o in outs],
        out_shape=[jax.ShapeDtypeStruct(o[0], o[1]) for o in outs],
        compiler_params=_cparams(("parallel",)),
        name="proj",
    )(x, wp, wt, wgt, bias, biast, *tabs, *tabs_t)


def _prep_w_in(w_in, b_gate):
    s = [0, 512, 1024, 1536, 2048, 2052, 2056, 2568, 3080, 3592, 4104, 4168, 4176]
    w16 = w_in.astype(BF16)
    mq, mk, mv, mo, mi, mf, aq, ak, av, iq, ik, iw = [w16[:, s[i]:s[i + 1]] for i in range(12)]
    z64 = jnp.zeros((D_MODEL, L_WI), BF16)
    pad = jnp.zeros((D_MODEL, N_PERM - C_SM - L_FG - M_HEADS), BF16)
    wp = jnp.concatenate([mq, mk, mv, mo, aq, iq, z64, iw, mi, mf, pad], axis=1)
    wt = jnp.concatenate([ak, av, ik, ik], axis=1).T
    wgt = jnp.concatenate([mi, mf], axis=1).T
    bg = b_gate.astype(F32)
    bias = jnp.zeros((1, LANES), F32).at[0, L_IG:L_IG + 2 * M_HEADS].set(bg)
    return wp, wt, wgt, bias, bg[:, None]


FF_CHUNK = D_FF // 2


def _layer_norm(x, g, b):
    mu = jnp.mean(x, axis=-1, keepdims=True)
    xc = x - mu
    var = jnp.mean(xc * xc, axis=-1, keepdims=True)
    return xc * lax.rsqrt(var + LN_EPS) * g + b


def _block_out_kernel(x_ref, hm_ref, ha_ref, wo_ref, g1_ref, b1_ref, wg_ref, wu_ref, wd_ref, g2_ref, b2_ref, y_ref):
    mix = _dot(hm_ref[...], wo_ref[0:M_WIDTH, :]) + _dot(ha_ref[...], wo_ref[M_WIDTH:2 * M_WIDTH, :])
    x1 = _layer_norm(ALPHA * x_ref[...] + mix, g1_ref[...], b1_ref[...])
    xb = x1.astype(BF16)
    ff = jnp.zeros_like(x1)
    for c in range(D_FF // FF_CHUNK):
        sl = slice(c * FF_CHUNK, (c + 1) * FF_CHUNK)
        act = jax.nn.silu(_dot(xb, wg_ref[:, sl])) * _dot(xb, wu_ref[:, sl])
        ff = ff + _dot(act.astype(BF16), wd_ref[sl, :])
    y_ref[...] = _layer_norm(ALPHA * x1 + ff, g2_ref[...], b2_ref[...])


def _block_out(x, hm, ha, wo, g1, b1, wg, wu, wd, g2, b2, tm):
    m = x.shape[0]
    row = lambda w: pl.BlockSpec((tm, w), lambda i: (i, 0))
    full = lambda a: pl.BlockSpec(a.shape, lambda i: (0, 0), pipeline_mode=pl.Buffered(1))
    return pl.pallas_call(
        _block_out_kernel,
        grid=(m // tm,),
        in_specs=[row(D_MODEL), row(M_WIDTH), row(A_WIDTH), full(wo), full(g1), full(b1),
                  full(wg), full(wu), full(wd), full(g2), full(b2)],
        out_specs=row(D_MODEL),
        out_shape=jax.ShapeDtypeStruct((m, D_MODEL), F32),
        compiler_params=_cparams(("parallel",)),
        name="block_out",
    )(x, hm, ha, wo, g1, b1, wg, wu, wd, g2, b2)


def _split3(x):
    h1 = x.astype(BF16)
    r1 = x - h1.astype(F32)
    h2 = r1.astype(BF16)
    h3 = (r1 - h2.astype(F32)).astype(BF16)
    return h1, h2, h3


def _dot3(x, w):
    h1, h2, h3 = _split3(x)
    return _dot(h1, w) + _dot(h2, w) + _dot(h3, w)


def _mlstm_chunk(q, k, v, lf_row, i_row, i_col, c_aug, m_prev):
    L = q.shape[0]
    t_idx = lax.broadcasted_iota(jnp.int32, (L, L), 0)
    s_idx = lax.broadcasted_iota(jnp.int32, (L, L), 1)
    causal = s_idx <= t_idx
    ones_b = jnp.ones((L, LANES), BF16)
    G = _dot3(jnp.where(causal, lf_row, 0.0), ones_b)
    tri_u = jnp.where(t_idx <= s_idx, 1.0, 0.0).astype(BF16)
    f_row = _dot3(jnp.broadcast_to(lf_row, (SUBLANES, L)), tri_u)[0:1, :]
    f_col = G[:, 0:1]
    dlog = jnp.where(causal, f_col - f_row + i_row, NEG_INF)
    inter = m_prev + f_col
    m_t = jnp.maximum(jnp.max(dlog, axis=1, keepdims=True), inter)
    w = jnp.exp(dlog - m_t)
    g = jnp.exp(inter - m_t)
    s = _dot_nt(q, k) * w
    lane = lax.broadcasted_iota(jnp.int32, (L, LANES), 1)
    v_aug = jnp.concatenate([v, jnp.where(lane == 0, 1.0, 0.0).astype(BF16)], axis=1)
    nd = _dot(s.astype(BF16), v_aug) + g * _dot(q, c_aug.astype(BF16))
    den = nd[:, M_HEAD_DIM:M_HEAD_DIM + 1]
    h = nd[:, 0:M_HEAD_DIM] / jnp.maximum(jnp.abs(den), jnp.exp(-m_t))
    m_new = m_t[L - 1:L, :]
    g_c = g[L - 1:L, :]
    wk = jnp.exp(f_col[L - 1:L, :] - f_col + i_col - m_new)
    kw = (k.astype(F32) * wk).astype(BF16)
    c_new = g_c * c_aug + _dot_tn(kw, v_aug)
    return h, c_new, m_new


def _mlstm_prompt_kernel(*refs, nb):
    mqkv_ref, so_ref, g_ref = refs[0:3]
    gt_refs = refs[3:3 + nb]
    hm_ref, c_ref, m_ref = refs[3 + nb:]
    step = pl.program_id(0)

    @pl.when(step == 0)
    def _():
        c_ref[...] = jnp.zeros_like(c_ref)
        m_ref[...] = jnp.zeros_like(m_ref)

    for b in range(nb):
        for h in range(M_HEADS):
            hs = slice(h * M_HEAD_DIM, (h + 1) * M_HEAD_DIM)
            q = mqkv_ref[b, :, hs]
            k = mqkv_ref[b, :, M_WIDTH + h * M_HEAD_DIM:M_WIDTH + (h + 1) * M_HEAD_DIM]
            v = mqkv_ref[b, :, 2 * M_WIDTH + h * M_HEAD_DIM:2 * M_WIDTH + (h + 1) * M_HEAD_DIM]
            i_row = gt_refs[b][h:h + 1, :]
            lf_row = gt_refs[b][M_HEADS + h:M_HEADS + h + 1, :]
            i_col = g_ref[b, :, L_IG + h:L_IG + h + 1]
            r = b * M_HEADS + h
            m_prev = m_ref[r:r + 1, 0:1]
            hh, c_new, m_new = _mlstm_chunk(q, k, v, lf_row, i_row, i_col, c_ref[b, h], m_prev)
            c_ref[b, h] = c_new
            m_ref[r:r + 1, :] = jnp.broadcast_to(m_new, (1, LANES))
            hm_ref[b, :, hs] = (hh * so_ref[b, :, hs]).astype(BF16)


def _mlstm_prompt(mqkv, so, gates, gatest, nb, seq, chunk):
    nc = seq // chunk
    blk = lambda w: pl.BlockSpec((nb, chunk, w), lambda c: (0, c, 0))
    gt_specs = [pl.BlockSpec((2 * M_HEADS, chunk), lambda c, b=b: (0, b * nc + c)) for b in range(nb)]
    return pl.pallas_call(
        functools.partial(_mlstm_prompt_kernel, nb=nb),
        grid=(nc,),
        in_specs=[blk(3 * M_WIDTH), blk(M_WIDTH), blk(LANES)] + gt_specs,
        out_specs=[blk(M_WIDTH),
                   pl.BlockSpec((nb, M_HEADS, M_HEAD_DIM, 2 * M_HEAD_DIM), lambda c: (0, 0, 0, 0)),
                   pl.BlockSpec((nb * M_HEADS, LANES), lambda c: (0, 0))],
        out_shape=[jax.ShapeDtypeStruct((nb, seq, M_WIDTH), BF16),
                   jax.ShapeDtypeStruct((nb, M_HEADS, M_HEAD_DIM, 2 * M_HEAD_DIM), F32),
                   jax.ShapeDtypeStruct((nb * M_HEADS, LANES), F32)],
        compiler_params=_cparams(("arbitrary",)),
        name="mlstm_prompt",
    )(mqkv, so, gates, *([gatest] * nb))


ROWS_PAD = SUBLANES


def _mlstm_sample_kernel(mqkv_ref, so_ref, g_ref, c0_ref, n0_ref, m0_ref, hm_ref, c_ref, n_ref, m_ref, *, nseq, t_real):
    row = lax.broadcasted_iota(jnp.int32, (ROWS_PAD, 1), 0)
    real = row < t_real

    def per_seq(s, carry):
        r0 = pl.multiple_of(s * ROWS_PAD, ROWS_PAD)
        gt = g_ref[pl.ds(r0, ROWS_PAD), :]
        cum = gt
        for d in range(1, t_real):
            cum = cum + jnp.where(row >= d, pltpu.roll(gt, d, 0), 0.0)
        for h in range(M_HEADS):
            hs = slice(h * M_HEAD_DIM, (h + 1) * M_HEAD_DIM)
            q = mqkv_ref[pl.ds(r0, ROWS_PAD), hs]
            k = mqkv_ref[pl.ds(r0, ROWS_PAD), M_WIDTH + h * M_HEAD_DIM:M_WIDTH + (h + 1) * M_HEAD_DIM]
            v = mqkv_ref[pl.ds(r0, ROWS_PAD), 2 * M_WIDTH + h * M_HEAD_DIM:2 * M_WIDTH + (h + 1) * M_HEAD_DIM]
            qf, kf, vf = q.astype(F32), k.astype(F32), v.astype(F32)
            i_col = gt[:, L_IG + h:L_IG + h + 1]
            f_col = cum[:, L_FG + h:L_FG + h + 1]
            c0 = c0_ref[s, h]
            n0 = n0_ref[s, h:h + 1, :]
            m0 = m0_ref[pl.ds(s, 1), h:h + 1]
            inter = m0 + f_col
            dl = [jnp.where(real & (row >= u), f_col - f_col[u:u + 1, :] + i_col[u:u + 1, :], NEG_INF)
                  for u in range(t_real)]
            m_t = inter
            for u in range(t_real):
                m_t = jnp.maximum(m_t, dl[u])
            g = jnp.exp(inter - m_t)
            qc = _dot(q, c0.astype(BF16))
            num = g * qc
            den = g * jnp.sum(qf * n0, axis=1, keepdims=True)
            for u in range(t_real):
                su = jnp.sum(qf * kf[u:u + 1, :], axis=1, keepdims=True) * jnp.exp(dl[u] - m_t)
                num = num + su * vf[u:u + 1, :]
                den = den + su
            hh = num / jnp.maximum(jnp.abs(den), jnp.exp(-m_t))
            hm_ref[pl.ds(r0, ROWS_PAD), hs] = (hh * so_ref[pl.ds(r0, ROWS_PAD), hs]).astype(BF16)
            last = t_real - 1
            m_new = m_t[last:last + 1, :]
            g_c = g[last:last + 1, :]
            wk = jnp.where(real, jnp.exp(f_col[last:last + 1, :] - f_col + i_col - m_new), 0.0)
            kw = kf * wk
            c_ref[s, h] = g_c * c0 + _dot_tn(kw.astype(BF16), v)
            n_ref[s, h:h + 1, :] = g_c * n0 + jnp.sum(kw, axis=0, keepdims=True)
            m_ref[pl.ds(s, 1), h:h + 1] = m_new
        return carry

    lax.fori_loop(0, nseq, per_seq, 0)


def _mlstm_sample(mqkv, so, gates, c0, n0, m0, nseq_blk, t_real):
    nseq = c0.shape[0]
    rows = nseq_blk * ROWS_PAD
    rblk = lambda w: pl.BlockSpec((rows, w), lambda i: (i, 0))
    cblk = pl.BlockSpec((nseq_blk, M_HEADS, M_HEAD_DIM, M_HEAD_DIM), lambda i: (i, 0, 0, 0))
    nblk = pl.BlockSpec((nseq_blk, M_HEADS, M_HEAD_DIM), lambda i: (i, 0, 0))
    mblk = pl.BlockSpec((nseq_blk, M_HEADS), lambda i: (i, 0))
    return pl.pallas_call(
        functools.partial(_mlstm_sample_kernel, nseq=nseq_blk, t_real=t_real),
        grid=(nseq // nseq_blk,),
        in_specs=[rblk(3 * M_WIDTH), rblk(M_WIDTH), rblk(LANES), cblk, nblk, mblk],
        out_specs=[rblk(M_WIDTH), cblk, nblk, mblk],
        out_shape=[jax.ShapeDtypeStruct((nseq * ROWS_PAD, M_WIDTH), BF16),
                   jax.ShapeDtypeStruct(c0.shape, F32),
                   jax.ShapeDtypeStruct(n0.shape, F32),
                   jax.ShapeDtypeStruct(m0.shape, F32)],
        compiler_params=_cparams(("parallel",)),
        name="mlstm_sample",
    )(mqkv, so, gates, c0, n0, m0)


N_BISECT = 19
MASKED = -1e30
ROW_BLOCK = 128
SELECT_TK = 256
NORM_TK = 1024
MIN_ROW_SUM = 2.0 ** -100


def _row_pass(sc_ref, nch, tk, init, fn, params=()):
    rows = sc_ref.shape[0]
    rb = min(ROW_BLOCK, rows)
    outs = []
    for r0 in range(0, rows, rb):
        rsl = lambda t: jax.tree.map(lambda a: a[r0:r0 + rb], t)
        prm = rsl(params)

        def body(c, acc, r0=r0, prm=prm):
            c0 = pl.multiple_of(c * tk, tk)
            blk = sc_ref[r0:r0 + rb, pl.ds(c0, tk)]
            for j in range(tk // LANES):
                acc = fn(acc, blk[:, j * LANES:(j + 1) * LANES], c0 + j * LANES, prm)
            return acc

        outs.append(lax.fori_loop(0, nch, body, rsl(init)))
    return jax.tree.map(lambda *a: jnp.concatenate(a, axis=0), *outs)


def _select_threshold(sc_ref, nch, tk, kr, smax, stats=None):
    rows = sc_ref.shape[0]
    zeros = jnp.zeros((rows, LANES), F32)
    lsum = lambda a: jnp.sum(a, axis=1, keepdims=True)
    full = lambda col: jnp.broadcast_to(col, (rows, LANES))

    def count_ge(thr):
        return lsum(_row_pass(sc_ref, nch, tk, zeros, lambda a, x, _, t: a + jnp.where(x >= t, 1.0, 0.0), full(thr)))

    def max_below(thr):
        acc = _row_pass(sc_ref, nch, tk, jnp.full((rows, LANES), NEG_INF, F32),
                        lambda a, x, _, t: jnp.maximum(a, jnp.where(x < t, x, NEG_INF)), full(thr))
        return jnp.max(acc, axis=1, keepdims=True)

    def range_fn(a, x, _, prm):
        fin = x > NEG_INF
        return (jnp.maximum(a[0], x), jnp.minimum(a[1], jnp.where(fin, x, jnp.inf)), a[2] + jnp.where(fin, 1.0, 0.0))

    if stats is None:
        mx, mn, nf = _row_pass(
            sc_ref, nch, tk, (jnp.full((rows, LANES), NEG_INF, F32), jnp.full((rows, LANES), jnp.inf, F32), zeros),
            range_fn)
        rmax, rmin, n_fin = jnp.max(mx, axis=1, keepdims=True), jnp.min(mn, axis=1, keepdims=True), lsum(nf)
    else:
        rmax, rmin, n_fin = stats

    above = rmax + jnp.maximum(jnp.abs(rmax) * (2.0 ** -20), 2.0 ** -100)

    def bisect(_, st):
        lo, hi, c_lo, c_hi = st
        mid = 0.5 * (lo + hi)
        c = count_ge(mid)
        ge = c >= kr
        return jnp.where(ge, mid, lo), jnp.where(ge, hi, mid), jnp.where(ge, c, c_lo), jnp.where(ge, c_hi, c)

    lo, hi, c_lo, c_hi = lax.fori_loop(0, N_BISECT, bisect, (rmin, above, n_fin, jnp.zeros_like(rmax)))

    done0 = jnp.where(c_lo - kr > 0.5, 0.0, 1.0)

    def peel_cond(st):
        it, _, _, _, _, _, done = st
        return jnp.logical_and(it <= smax, jnp.min(done) < 0.5)

    def peel(st):
        it, hi, c_hi, tau, c_gt, c_ge, done = st
        t1 = max_below(hi)
        c1 = count_ge(t1)
        fin = jnp.logical_and(c1 >= kr, done < 0.5)
        tau = jnp.where(fin, t1, tau)
        c_gt = jnp.where(fin, c_hi, c_gt)
        c_ge = jnp.where(fin, c1, c_ge)
        done = jnp.where(fin, 1.0, done)
        live = done < 0.5
        return it + 1, jnp.where(live, t1, hi), jnp.where(live, c1, c_hi), tau, c_gt, c_ge, done

    st = (jnp.int32(0), hi, c_hi, lo, jnp.zeros_like(rmax), kr, done0)
    _, _, _, tau, c_gt, c_ge, _ = lax.while_loop(peel_cond, peel, st)

    need = kr - c_gt
    surplus = (c_ge - c_gt) - need
    rb = min(ROW_BLOCK, rows)
    k_idx = lax.broadcasted_iota(jnp.int32, (LANES, 2 * LANES), 0)
    j_idx = lax.broadcasted_iota(jnp.int32, (LANES, 2 * LANES), 1)
    tri_ones = jnp.where(jnp.logical_or(k_idx <= j_idx, j_idx >= LANES), 1.0, 0.0).astype(BF16)
    for r0 in range(0, rows, rb):
        @pl.when(jnp.max(surplus[r0:r0 + rb]) > 0.5)
        def _(r0=r0):
            tau_f = jnp.broadcast_to(tau[r0:r0 + rb], (rb, LANES))
            need_f = jnp.broadcast_to(need[r0:r0 + rb], (rb, LANES))

            def sweep(c, seen):
                for j in range(tk // LANES):
                    c0 = pl.multiple_of(c * tk + j * LANES, LANES)
                    x = sc_ref[r0:r0 + rb, pl.ds(c0, LANES)]
                    tied = x == tau_f
                    cnt = _dot(jnp.where(tied, 1.0, 0.0).astype(BF16), tri_ones)
                    rank = seen + cnt[:, 0:LANES]
                    sc_ref[r0:r0 + rb, pl.ds(c0, LANES)] = jnp.where(jnp.logical_and(tied, rank > need_f), NEG_INF, x)
                    seen = seen + cnt[:, LANES:2 * LANES]
                return seen

            lax.fori_loop(0, nch, sweep, jnp.zeros((rb, LANES), F32))

    return tau


def _ones_rows(tk):
    return jnp.where(lax.broadcasted_iota(jnp.int32, (LANES, tk), 0) == 0, 1.0, 0.0).astype(BF16)


def _dsa_prompt_kernel(qi_ref, g_ref, q_ref, kit_ref, kt_ref, vt_ref, o_ref, sc_ref, tau_scr, m_scr, acc_scr, kn_scr,
                       *, tq, tk, topk, smax):
    i = pl.program_id(1)
    nch = ((i + 1) * tq + tk - 1) // tk
    row = lax.broadcasted_iota(jnp.int32, (tq, 1), 0) + i * tq

    m_scr[0] = jnp.full((tq, LANES), NEG_INF, F32)
    m_scr[1] = jnp.full((tq, LANES), jnp.inf, F32)

    def score_chunk(c, carry):
        c0 = pl.multiple_of(c * tk, tk)
        kblk = kit_ref[:, pl.ds(c0, tk)]
        acc = jnp.zeros((tq, tk), F32)
        for h in range(IDX_HEADS):
            x = _dot(qi_ref[:, h * LANES:(h + 1) * LANES], kblk)
            acc = acc + jnp.maximum(x, 0.0) * g_ref[:, L_WI + h:L_WI + h + 1]
        col = lax.broadcasted_iota(jnp.int32, (1, LANES), 1) + c0
        hi_p, lo_p = m_scr[0], m_scr[1]
        for j in range(tk // LANES):
            causal = col + j * LANES <= row
            a = acc[:, j * LANES:(j + 1) * LANES]
            masked = jnp.where(causal, a, NEG_INF)
            sc_ref[:, pl.ds(pl.multiple_of(c0 + j * LANES, LANES), LANES)] = masked
            hi_p = jnp.maximum(hi_p, masked)
            lo_p = jnp.minimum(lo_p, jnp.where(causal, a, jnp.inf))
        m_scr[0], m_scr[1] = hi_p, lo_p
        return carry

    lax.fori_loop(0, nch, score_chunk, 0)

    kr = jnp.minimum(row + 1, topk).astype(F32)
    stats = (jnp.max(m_scr[0], axis=1, keepdims=True), jnp.min(m_scr[1], axis=1, keepdims=True), (row + 1).astype(F32))
    tau = _select_threshold(sc_ref, nch * (tk // SELECT_TK), SELECT_TK, kr, smax, stats)

    nslab = tk // LANES
    tau_scr[...] = jnp.broadcast_to(tau, (tq, LANES))

    def masked_logits(c0, h):
        pr = slice((h // 2) * LANES, (h // 2 + 1) * LANES)
        s = _dot(q_ref[:, h * LANES:(h + 1) * LANES], kt_ref[pr, pl.ds(c0, tk)])
        x = sc_ref[:, pl.ds(c0, tk)]
        thr = tau_scr[...]
        return [jnp.where(x[:, j * LANES:(j + 1) * LANES] >= thr, s[:, j * LANES:(j + 1) * LANES], MASKED)
                for j in range(nslab)]

    def max_sweep():
        m_scr[...] = jnp.full(m_scr.shape, MASKED, F32)

        def max_chunk(c, carry):
            c0 = pl.multiple_of(c * tk, tk)
            for h in range(A_HEADS):
                slabs = masked_logits(c0, h)
                part = slabs[0]
                for j in range(1, nslab):
                    part = jnp.maximum(part, slabs[j])
                m_scr[h] = jnp.maximum(m_scr[h], part)
            return carry

        lax.fori_loop(0, nch, max_chunk, 0)
        for h in range(A_HEADS):
            m_scr[h] = jnp.broadcast_to(jnp.max(m_scr[h], axis=1, keepdims=True), (tq, LANES))

    ones = _ones_rows(tk)

    def attend_sweep():
        acc_scr[...] = jnp.zeros(acc_scr.shape, F32)

        def attend_chunk(c, carry):
            c0 = pl.multiple_of(c * tk, tk)
            for h in range(A_HEADS):
                pr = slice((h // 2) * LANES, (h // 2 + 1) * LANES)
                m_h = m_scr[h]
                pb = jnp.concatenate([jnp.exp2(sl - m_h).astype(BF16) for sl in masked_logits(c0, h)], axis=1)
                vt_aug = jnp.concatenate([vt_ref[pr, pl.ds(c0, tk)], ones], axis=0)
                acc_scr[h] = acc_scr[h] + _dot_nt(pb, vt_aug)
            return carry

        lax.fori_loop(0, nch, attend_chunk, 0)

    @pl.when(i == 0)
    def _():
        for h in range(A_HEADS):
            def norm_chunk(c, best, h=h):
                c0 = pl.multiple_of(c * NORM_TK, NORM_TK)
                kk = kt_ref[h * A_HEAD_DIM:(h + 1) * A_HEAD_DIM, pl.ds(c0, NORM_TK)].astype(F32)
                return jnp.maximum(best, jnp.sum(kk * kk, axis=0, keepdims=True))
            best = lax.fori_loop(0, smax // NORM_TK, norm_chunk, jnp.zeros((1, NORM_TK), F32))
            kn_scr[h:h + 1, :] = jnp.broadcast_to(jnp.max(best, axis=1, keepdims=True), (1, LANES))

    for h in range(A_HEADS):
        qh = q_ref[:, h * LANES:(h + 1) * LANES].astype(F32)
        qn2 = jnp.sum(qh * qh, axis=1, keepdims=True)
        m_scr[h] = jnp.broadcast_to(jnp.sqrt(qn2 * kn_scr[h:h + 1, 0:1]), (tq, LANES))
    attend_sweep()
    l_min = acc_scr[0][:, LANES:LANES + 1]
    for h in range(1, A_HEADS):
        l_min = jnp.minimum(l_min, acc_scr[h][:, LANES:LANES + 1])

    @pl.when(jnp.logical_not(jnp.min(l_min) >= MIN_ROW_SUM))
    def _():
        max_sweep()
        attend_sweep()

    lo_half = lax.broadcasted_iota(jnp.int32, (1, LANES), 1) < A_HEAD_DIM
    for p in range(A_HEADS // 2):
        a_e, a_o = acc_scr[2 * p], acc_scr[2 * p + 1]
        even = a_e[:, 0:LANES] / a_e[:, LANES:LANES + 1]
        odd = a_o[:, 0:LANES] / a_o[:, LANES:LANES + 1]
        o_ref[:, p * LANES:(p + 1) * LANES] = jnp.where(lo_half, even, odd).astype(BF16)


def _dsa_prompt(qipad, gates, qpad, kit2, ktb, vtb, nb, seq, tq, tk):
    nq = seq // tq
    topk = min(TOPK_MAX, seq // 4)
    qrow = lambda w: pl.BlockSpec((tq, w), lambda b, i: (b * nq + i, 0))
    kfull = lambda r: pl.BlockSpec((None, r, seq), lambda b, i: (b, 0, 0), pipeline_mode=pl.Buffered(1))
    return pl.pallas_call(
        functools.partial(_dsa_prompt_kernel, tq=tq, tk=tk, topk=topk, smax=seq),
        grid=(nb, nq),
        in_specs=[qrow(2 * A_WIDTH), qrow(LANES), qrow(2 * A_WIDTH), kfull(2 * IDX_DIM), kfull(A_WIDTH), kfull(A_WIDTH)],
        out_specs=qrow(A_WIDTH),
        out_shape=jax.ShapeDtypeStruct((nb * seq, A_WIDTH), BF16),
        scratch_shapes=[pltpu.VMEM((tq, seq + LANES), F32),
                        pltpu.VMEM((tq, LANES), F32),
                        pltpu.VMEM((A_HEADS, tq, LANES), F32),
                        pltpu.VMEM((A_HEADS, tq, 2 * LANES), F32),
                        pltpu.VMEM((A_HEADS, LANES), F32)],
        compiler_params=_cparams(("parallel", "arbitrary")),
        name="dsa_prompt",
    )(qipad, gates, qpad, kit2, ktb, vtb)


def _dsa_sample_scores_kernel(pt_ref, q_ref, w_ref, *refs, n_pages, t_real):
    page_refs, new_ref, o_ref = refs[:n_pages], refs[n_pages], refs[n_pages + 1]
    q = q_ref[0]
    w = w_ref[0]
    trow = lax.broadcasted_iota(jnp.int32, (t_real, 1), 0)
    for p in range(n_pages + 1):
        keys_t = page_refs[p][0].astype(BF16) if p < n_pages else new_ref[0]
        r = jnp.maximum(_dot(q, keys_t), 0.0) * w
        sc = jnp.sum(r.reshape(t_real, IDX_HEADS, PAGE_SIZE), axis=1)
        if p == n_pages:
            col = lax.broadcasted_iota(jnp.int32, (1, PAGE_SIZE), 1)
            sc = jnp.where(col <= trow, sc, NEG_INF)
        o_ref[0, :, p * PAGE_SIZE:(p + 1) * PAGE_SIZE] = sc


def _dsa_sample_scores(page_table, qi, wi, cache_kidx_t, ki_new_t, t_real):
    nseq, n_pages = page_table.shape
    ncol = (n_pages + 1) * PAGE_SIZE
    per_seq = lambda a: pl.BlockSpec((1,) + a.shape[1:], lambda b, pt: (b, 0, 0))
    page = lambda p: pl.BlockSpec((1, IDX_DIM, PAGE_SIZE), lambda b, pt, p=p: (pt[b * n_pages + p], 0, 0))
    return pl.pallas_call(
        functools.partial(_dsa_sample_scores_kernel, n_pages=n_pages, t_real=t_real),
        grid_spec=pltpu.PrefetchScalarGridSpec(
            num_scalar_prefetch=1,
            grid=(nseq,),
            in_specs=[per_seq(qi), per_seq(wi)] + [page(p) for p in range(n_pages)] + [per_seq(ki_new_t)],
            out_specs=pl.BlockSpec((1, t_real, ncol), lambda b, pt: (b, 0, 0)),
        ),
        out_shape=jax.ShapeDtypeStruct((nseq, t_real, ncol), F32),
        compiler_params=_cparams(("parallel",)),
        name="dsa_sample_scores",
    )(page_table.reshape(-1), qi, wi, *([cache_kidx_t] * n_pages), ki_new_t)


def _dsa_sample_select_kernel(sc_ref, adj_ref, tau_ref, *, tk, topk):
    adj_ref[...] = sc_ref[...]
    rows, ncol = adj_ref.shape
    kr = jnp.full((rows, 1), float(topk), F32)
    tau = _select_threshold(adj_ref, ncol // tk, tk, kr, ncol)
    tau_ref[...] = jnp.broadcast_to(tau, (rows, LANES))


def _dsa_sample_select(sc, topk, rblk):
    rows, ncol = sc.shape
    return pl.pallas_call(
        functools.partial(_dsa_sample_select_kernel, tk=LANES, topk=topk),
        grid=(rows // rblk,),
        in_specs=[pl.BlockSpec((rblk, ncol), lambda i: (i, 0))],
        out_specs=[pl.BlockSpec((rblk, ncol), lambda i: (i, 0)), pl.BlockSpec((rblk, LANES), lambda i: (i, 0))],
        out_shape=[jax.ShapeDtypeStruct((rows, ncol), F32), jax.ShapeDtypeStruct((rows, LANES), F32)],
        compiler_params=_cparams(("parallel",)),
        name="dsa_sample_select",
    )(sc)


def _dsa_sample_attend_kernel(pt_ref, q_ref, sc_ref, tau_ref, *refs, n_pages, t_real):
    kp, vp = refs[:n_pages], refs[n_pages:2 * n_pages]
    knew_ref, vnew_ref, o_ref, s_scr = refs[2 * n_pages:]
    q = q_ref[0]
    nrow = t_real * A_HEADS
    rows_of = lambda a: jnp.broadcast_to(a[:, None, :], (t_real, A_HEADS, a.shape[-1])).reshape(nrow, a.shape[-1])
    tau = rows_of(tau_ref[0][:, 0:1])
    for p in range(n_pages + 1):
        kt = kp[p][0].reshape(A_WIDTH, PAGE_SIZE).astype(BF16) if p < n_pages else knew_ref[0]
        x = rows_of(sc_ref[0, :, p * PAGE_SIZE:(p + 1) * PAGE_SIZE])
        s_scr[:, p * PAGE_SIZE:(p + 1) * PAGE_SIZE] = jnp.where(x >= tau, _dot(q, kt), MASKED)
    s_all = s_scr[...]
    pr = jnp.exp2(s_all - jnp.max(s_all, axis=1, keepdims=True))
    l = jnp.sum(pr, axis=1, keepdims=True)
    s_scr[...] = pr
    acc = jnp.zeros((nrow, A_WIDTH), F32)
    for p in range(n_pages + 1):
        vt = vp[p][0].reshape(A_WIDTH, PAGE_SIZE).astype(BF16) if p < n_pages else vnew_ref[0]
        acc = acc + _dot_nt(s_scr[:, p * PAGE_SIZE:(p + 1) * PAGE_SIZE].astype(BF16), vt)
    out = acc / l
    head_of_row = lax.broadcasted_iota(jnp.int32, (nrow, 1), 0) % A_HEADS
    head_of_lane = lax.broadcasted_iota(jnp.int32, (1, A_WIDTH), 1) // A_HEAD_DIM
    out = jnp.where(head_of_row == head_of_lane, out, 0.0)
    o_ref[0] = jnp.sum(out.reshape(t_real, A_HEADS, A_WIDTH), axis=1).astype(BF16)


def _dsa_sample_attend(page_table, qbd, sc_adj, tau, cache_k_t, cache_v_t, k_new_t, v_new_t, t_real):
    nseq, n_pages = page_table.shape
    ncol = (n_pages + 1) * PAGE_SIZE
    per_seq = lambda a: pl.BlockSpec((1,) + a.shape[1:], lambda b, pt: (b, 0, 0))
    page = lambda p: pl.BlockSpec((1, A_HEADS, A_HEAD_DIM, PAGE_SIZE), lambda b, pt, p=p: (pt[b * n_pages + p], 0, 0, 0))
    pages = lambda: [page(p) for p in range(n_pages)]
    return pl.pallas_call(
        functools.partial(_dsa_sample_attend_kernel, n_pages=n_pages, t_real=t_real),
        grid_spec=pltpu.PrefetchScalarGridSpec(
            num_scalar_prefetch=1,
            grid=(nseq,),
            in_specs=[per_seq(qbd), per_seq(sc_adj), per_seq(tau)] + pages() + pages() + [per_seq(k_new_t), per_seq(v_new_t)],
            out_specs=pl.BlockSpec((1, t_real, A_WIDTH), lambda b, pt: (b, 0, 0)),
            scratch_shapes=[pltpu.VMEM((t_real * A_HEADS, ncol), F32)],
        ),
        out_shape=jax.ShapeDtypeStruct((nseq, t_real, A_WIDTH), BF16),
        compiler_params=_cparams(("parallel",)),
        name="dsa_sample_attend",
    )(page_table.reshape(-1), qbd, sc_adj, tau, *([cache_k_t] * n_pages), *([cache_v_t] * n_pages), k_new_t, v_new_t)


PROJ_TM = 256
OUT_TM = 256
MLSTM_CHUNK = 256
DSA_TQ = 512
DSA_TK = 512
SAMPLE_SEQ_BLK = 8
SELECT_ROWS = 128


def _unpad_heads(xpad, n_heads):
    x = xpad.reshape(xpad.shape[:-1] + (n_heads, 2, LANES // 2))
    return x[..., 0, :] + x[..., 1, :]


def kernel(x_prompt, x_sample, cache_k, cache_v, cache_kidx, state_C, state_n, state_m, page_table,
           w_in, b_gate, w_out, ln1_g, ln1_b, w_gate, w_up, w_down, ln2_g, ln2_b):
    bp, sp, _ = x_prompt.shape
    bs, ts, _ = x_sample.shape
    n_pages = page_table.shape[1]
    past = n_pages * PAGE_SIZE
    assert DEPTH == 1 and w_in.shape[0] == 1

    w = _prep_w_in(w_in[0], b_gate[0])
    wo, wg, wu, wd = (a[0].astype(BF16) for a in (w_out, w_gate, w_up, w_down))
    g1, b1, g2, b2 = (v[0].astype(F32)[None, :] for v in (ln1_g, ln1_b, ln2_g, ln2_b))

    mp = bp * sp
    xp = x_prompt.reshape(mp, D_MODEL)
    tabs_p, tabs_pt = _rope_tables(jnp.arange(sp, dtype=jnp.int32))
    (mqkv, so, gates, gatest, qpad, qipad, kt_p, ktb, vt_p, vtb, kit_p, kit2) = _proj(
        xp, w, tabs_p, tabs_pt, PROJ_TM, bp, sp)
    hm_p, caug, m_p = _mlstm_prompt(mqkv.reshape(bp, sp, -1), so.reshape(bp, sp, -1), gates.reshape(bp, sp, -1),
                                    gatest, bp, sp, MLSTM_CHUNK)
    ha_p = _dsa_prompt(qipad, gates, qpad, kit2, ktb, vtb, bp, sp, DSA_TQ, DSA_TK)
    y_p = _block_out(xp, hm_p.reshape(mp, -1), ha_p, wo, g1, b1, wg, wu, wd, g2, b2, OUT_TM)

    ms = bs * ts
    xs = x_sample.reshape(ms, D_MODEL)
    tabs_s, tabs_st = _rope_tables(jnp.tile(past + jnp.arange(ts, dtype=jnp.int32), bs))
    (mqkv_s, so_s, gates_s, _, qpad_s, qipad_s, kt_s, ktb_s, vt_s, vtb_s, kit_s, _) = _proj(
        xs, w, tabs_s, tabs_st, ms, 1, ms)
    pad_rows = lambda a: jnp.pad(a.reshape(bs, ts, -1), ((0, 0), (0, ROWS_PAD - ts), (0, 0))).reshape(bs * ROWS_PAD, -1)
    hm_s, c_s, n_s, m_s = _mlstm_sample(pad_rows(mqkv_s), pad_rows(so_s), pad_rows(gates_s),
                                        state_C[0].astype(F32), state_n[0].astype(F32), state_m[0].astype(F32),
                                        SAMPLE_SEQ_BLK, ts)
    hm_s = hm_s.reshape(bs, ROWS_PAD, -1)[:, :ts].reshape(ms, -1)

    new_page = lambda a: jnp.pad(a.reshape(a.shape[0], bs, ts).transpose(1, 0, 2), ((0, 0), (0, 0), (0, PAGE_SIZE - ts)))
    qi_s = _unpad_heads(qipad_s, IDX_HEADS).reshape(bs, ts * IDX_HEADS, IDX_DIM)
    wi_s = gates_s[:, L_WI:L_WI + IDX_HEADS].reshape(bs, ts * IDX_HEADS, 1)
    sc_s = _dsa_sample_scores(page_table, qi_s, wi_s, cache_kidx[0].transpose(0, 2, 1), new_page(kit_s[0]).astype(BF16), ts)
    ncol = sc_s.shape[-1]
    topk_s = min(TOPK_MAX, (past + ts) // 4)
    sc_adj, tau_s = _dsa_sample_select(sc_s.reshape(ms, ncol), topk_s, SELECT_ROWS)
    q_s = _unpad_heads(qpad_s, A_HEADS).reshape(bs, ts, A_HEADS, A_HEAD_DIM)
    eye = jnp.eye(A_HEADS, dtype=q_s.dtype)
    qbd = (q_s[:, :, :, None, :] * eye[None, None, :, :, None]).reshape(bs, ts * A_HEADS, A_WIDTH)
    ha_s = _dsa_sample_attend(page_table, qbd, sc_adj.reshape(bs, ts, ncol), tau_s.reshape(bs, ts, LANES),
                              cache_k[0].transpose(0, 2, 3, 1), cache_v[0].transpose(0, 2, 3, 1),
                              new_page(ktb_s[0]), new_page(vtb_s[0]), ts)
    y_s = _block_out(xs, hm_s, ha_s.reshape(ms, -1), wo, g1, b1, wg, wu, wd, g2, b2, min(OUT_TM, ms))

    heads = lambda a, b, t: a.reshape(1, b, A_HEADS, A_HEAD_DIM, t).transpose(0, 1, 4, 2, 3)
    heads_s = lambda a: a[0].T.reshape(1, bs, ts, A_HEADS, A_HEAD_DIM)
    return (y_p.reshape(bp, sp, D_MODEL), y_s.reshape(bs, ts, D_MODEL),
            heads(kt_p, bp, sp), heads(vt_p, bp, sp), kit_p.transpose(0, 2, 1)[None],
            caug[None, :, :, :, 0:M_HEAD_DIM], caug[None, :, :, :, M_HEAD_DIM], m_p[:, 0].reshape(1, bp, M_HEADS),
            heads_s(kt_s), heads_s(vt_s), kit_s[0].T.reshape(1, bs, ts, IDX_DIM),
            c_s[None], n_s[None], m_s[None])
```

```python
import functools
import math

import jax
import jax.numpy as jnp
from jax import lax
from jax.experimental import pallas as pl
from jax.experimental.pallas import tpu as pltpu

D_MODEL = 1024
M_HEADS = 4
M_HEAD_DIM = 128
M_WIDTH = 512
A_HEADS = 8
A_HEAD_DIM = 64
A_WIDTH = 512
IDX_HEADS = 8
IDX_DIM = 64
TOPK_MAX = 256
PAGE_SIZE = 128
ROPE_THETA = 500000.0
ROT = A_HEAD_DIM // 4
HALF = ROT // 2
D_FF = 2816
DEPTH = 1
ALPHA = (2 * DEPTH) ** 0.25
LN_EPS = 1e-5
LOG2E = math.log2(math.e)

LANES = 128
SUBLANES = 8
VMEM_LIMIT = 56 * 1024 * 1024

NEG_INF = float("-inf")
BF16 = jnp.bfloat16
F32 = jnp.float32

C_MQ, C_MK, C_MV, C_MO = 0, 512, 1024, 1536
C_AQ, C_IQ, C_SM = 2048, 2560, 3072
N_PERM = 3200
L_WI, L_IG, L_FG = 64, 72, 76
R_AK, R_AV, R_IK, N_ROWS_T = 0, 512, 1024, 1152


def _cparams(sem):
    return pltpu.CompilerParams(dimension_semantics=sem, vmem_limit_bytes=VMEM_LIMIT)


def _dot(a, b):
    return jnp.dot(a, b, preferred_element_type=F32)


def _dot_nt(a, b):
    return lax.dot_general(a, b, (((1,), (1,)), ((), ())), preferred_element_type=F32)


def _dot_tn(a, b):
    return lax.dot_general(a, b, (((0,), (0,)), ((), ())), preferred_element_type=F32)


def _rope128(x, cos, sa, sb):
    return x * cos + pltpu.roll(x, LANES - HALF, 1) * sa + pltpu.roll(x, HALF, 1) * sb


def _rope_rows(z, cos_t, sin_t):
    a, b = z[0:HALF], z[HALF:ROT]
    return jnp.concatenate([a * cos_t - b * sin_t, b * cos_t + a * sin_t, z[ROT:]], axis=0)


def _proj_kernel(x_ref, w_ref, wt_ref, wgt_ref, bias_ref, biast_ref, cos_ref, sa_ref, sb_ref, cost_ref, sint_ref,
                 mqkv_ref, so_ref, gates_ref, gatest_ref, qpad_ref, qipad_ref,
                 kt_ref, ktb_ref, vt_ref, vtb_ref, kit_ref, kit2_ref):
    xb = x_ref[...].astype(BF16)
    cos, sa, sb = cos_ref[...], sa_ref[...], sb_ref[...]
    cos_t, sin_t = cost_ref[...], sint_ref[...]
    lane = lax.broadcasted_iota(jnp.int32, (1, LANES), 1)
    lo_half = lane < A_HEAD_DIM

    zm = _dot(xb, w_ref[:, C_MQ:C_MO])
    mqkv_ref[:, 0:M_WIDTH] = zm[:, 0:M_WIDTH].astype(BF16)
    mqkv_ref[:, M_WIDTH:2 * M_WIDTH] = (zm[:, M_WIDTH:2 * M_WIDTH] * (M_HEAD_DIM ** -0.5)).astype(BF16)
    mqkv_ref[:, 2 * M_WIDTH:3 * M_WIDTH] = zm[:, 2 * M_WIDTH:3 * M_WIDTH].astype(BF16)
    so_ref[...] = jax.nn.sigmoid(_dot(xb, w_ref[:, C_MO:C_AQ]))

    def padded_heads(z, scale, out_ref):
        for p in range(A_WIDTH // LANES):
            r = _rope128(z[:, p * LANES:(p + 1) * LANES], cos, sa, sb) * scale
            out_ref[:, (2 * p) * LANES:(2 * p + 1) * LANES] = jnp.where(lo_half, r, 0.0).astype(BF16)
            out_ref[:, (2 * p + 1) * LANES:(2 * p + 2) * LANES] = jnp.where(lo_half, 0.0, r).astype(BF16)

    padded_heads(_dot(xb, w_ref[:, C_AQ:C_IQ]), (A_HEAD_DIM ** -0.5) * LOG2E, qpad_ref)
    padded_heads(_dot(xb, w_ref[:, C_IQ:C_SM]), IDX_DIM ** -0.5, qipad_ref)

    zs = _dot(xb, w_ref[:, C_SM:N_PERM]) + bias_ref[...]
    is_wi = (lane >= L_WI) & (lane < L_IG)
    is_fg = (lane >= L_FG) & (lane < L_FG + M_HEADS)
    g = jnp.where(is_wi, zs * (IDX_HEADS ** -0.5), zs)
    gates_ref[...] = jnp.where(is_fg, jax.nn.log_sigmoid(zs), g)

    zt = _dot_nt(wgt_ref[...], xb) + biast_ref[...]
    row = lax.broadcasted_iota(jnp.int32, (2 * M_HEADS, 1), 0)
    gatest_ref[...] = jnp.where(row >= M_HEADS, jax.nn.log_sigmoid(zt), zt)

    zkt = _dot_nt(wt_ref[R_AK:R_AV, :], xb)
    for h in range(A_HEADS):
        r = _rope_rows(zkt[h * A_HEAD_DIM:(h + 1) * A_HEAD_DIM], cos_t, sin_t)
        kt_ref[h * A_HEAD_DIM:(h + 1) * A_HEAD_DIM, :] = r
        ktb_ref[h * A_HEAD_DIM:(h + 1) * A_HEAD_DIM, :] = r.astype(BF16)
    zvt = _dot_nt(wt_ref[R_AV:R_IK, :], xb)
    vt_ref[...] = zvt
    vtb_ref[...] = zvt.astype(BF16)
    zit = _dot_nt(wt_ref[R_IK:N_ROWS_T, :], xb)
    r = _rope_rows(zit[0:IDX_DIM], cos_t, sin_t)
    kit_ref[...] = r
    kit2_ref[...] = jnp.concatenate([r, r], axis=0).astype(BF16)


def _rope_tables(pos):
    inv = ROPE_THETA ** (-jnp.arange(HALF, dtype=F32) / HALF)
    ang = pos.astype(F32)[:, None] * inv[None, :]
    c, s = jnp.cos(ang), jnp.sin(ang)
    n = pos.shape[0]
    one = jnp.ones((n, A_HEAD_DIM - ROT), F32)
    zero = jnp.zeros((n, A_HEAD_DIM - ROT), F32)
    z8 = jnp.zeros((n, HALF), F32)
    cos64 = jnp.concatenate([c, c, one], axis=1)
    sa64 = jnp.concatenate([-s, z8, zero], axis=1)
    sb64 = jnp.concatenate([z8, s, zero], axis=1)
    two = lambda t: jnp.concatenate([t, t], axis=1)
    return (two(cos64), two(sa64), two(sb64)), (c.T, s.T)


def _proj(x, w, tabs, tabs_t, tm, nb, seq):
    wp, wt, wgt, bias, biast = w
    m = x.shape[0]
    nblk = seq // tm
    row = lambda wd: pl.BlockSpec((tm, wd), lambda i: (i, 0))
    full = lambda a: pl.BlockSpec(a.shape, lambda i: (0, 0))
    tab = pl.BlockSpec((tm, LANES), lambda i: (i % nblk, 0))
    tab_t = pl.BlockSpec((HALF, tm), lambda i: (0, i % nblk))
    fmaj = lambda r: pl.BlockSpec((None, r, tm), lambda i: (i // nblk, 0, i % nblk))
    outs = [
        ((m, 3 * M_WIDTH), BF16, row(3 * M_WIDTH)),
        ((m, M_WIDTH), F32, row(M_WIDTH)),
        ((m, LANES), F32, row(LANES)),
        ((2 * M_HEADS, m), F32, pl.BlockSpec((2 * M_HEADS, tm), lambda i: (0, i))),
        ((m, 2 * A_WIDTH), BF16, row(2 * A_WIDTH)),
        ((m, 2 * A_WIDTH), BF16, row(2 * A_WIDTH)),
        ((nb, A_WIDTH, seq), F32, fmaj(A_WIDTH)),
        ((nb, A_WIDTH, seq), BF16, fmaj(A_WIDTH)),
        ((nb, A_WIDTH, seq), F32, fmaj(A_WIDTH)),
        ((nb, A_WIDTH, seq), BF16, fmaj(A_WIDTH)),
        ((nb, IDX_DIM, seq), F32, fmaj(IDX_DIM)),
        ((nb, 2 * IDX_DIM, seq), BF16, fmaj(2 * IDX_DIM)),
    ]
    return pl.pallas_call(
        _proj_kernel,
        grid=(m // tm,),
        in_specs=[row(D_MODEL), full(wp), full(wt), full(wgt), full(bias), full(biast), tab, tab, tab, tab_t, tab_t],
        out_specs=[o[2] for o in outs],
        out_shape=[jax.ShapeDtypeStruct(o[0], o[1]) for o in outs],
        compiler_params=_cparams(("parallel",)),
        name="proj",
    )(x, wp, wt, wgt, bias, biast, *tabs, *tabs_t)


def _prep_w_in(w_in, b_gate):
    s = [0, 512, 1024, 1536, 2048, 2052, 2056, 2568, 3080, 3592, 4104, 4168, 4176]
    w16 = w_in.astype(BF16)
    mq, mk, mv, mo, mi, mf, aq, ak, av, iq, ik, iw = [w16[:, s[i]:s[i + 1]] for i in range(12)]
    z64 = jnp.zeros((D_MODEL, L_WI), BF16)
    pad = jnp.zeros((D_MODEL, N_PERM - C_SM - L_FG - M_HEADS), BF16)
    wp = jnp.concatenate([mq, mk, mv, mo, aq, iq, z64, iw, mi, mf, pad], axis=1)
    wt = jnp.concatenate([ak, av, ik, ik], axis=1).T
    wgt = jnp.concatenate([mi, mf], axis=1).T
    bg = b_gate.astype(F32)
    bias = jnp.zeros((1, LANES), F32).at[0, L_IG:L_IG + 2 * M_HEADS].set(bg)
    return wp, wt, wgt, bias, bg[:, None]


FF_CHUNK = D_FF // 2


def _layer_norm(x, g, b):
    mu = jnp.mean(x, axis=-1, keepdims=True)
    xc = x - mu
    var = jnp.mean(xc * xc, axis=-1, keepdims=True)
    return xc * lax.rsqrt(var + LN_EPS) * g + b


def _block_out_kernel(x_ref, hm_ref, ha_ref, wo_ref, g1_ref, b1_ref, wg_ref, wu_ref, wd_ref, g2_ref, b2_ref, y_ref):
    mix = _dot(hm_ref[...], wo_ref[0:M_WIDTH, :]) + _dot(ha_ref[...], wo_ref[M_WIDTH:2 * M_WIDTH, :])
    x1 = _layer_norm(ALPHA * x_ref[...] + mix, g1_ref[...], b1_ref[...])
    xb = x1.astype(BF16)
    ff = jnp.zeros_like(x1)
    for c in range(D_FF // FF_CHUNK):
        sl = slice(c * FF_CHUNK, (c + 1) * FF_CHUNK)
        act = jax.nn.silu(_dot(xb, wg_ref[:, sl])) * _dot(xb, wu_ref[:, sl])
        ff = ff + _dot(act.astype(BF16), wd_ref[sl, :])
    y_ref[...] = _layer_norm(ALPHA * x1 + ff, g2_ref[...], b2_ref[...])


def _block_out(x, hm, ha, wo, g1, b1, wg, wu, wd, g2, b2, tm):
    m = x.shape[0]
    row = lambda w: pl.BlockSpec((tm, w), lambda i: (i, 0))
    full = lambda a: pl.BlockSpec(a.shape, lambda i: (0, 0), pipeline_mode=pl.Buffered(1))
    return pl.pallas_call(
        _block_out_kernel,
        grid=(m // tm,),
        in_specs=[row(D_MODEL), row(M_WIDTH), row(A_WIDTH), full(wo), full(g1), full(b1),
                  full(wg), full(wu), full(wd), full(g2), full(b2)],
        out_specs=row(D_MODEL),
        out_shape=jax.ShapeDtypeStruct((m, D_MODEL), F32),
        compiler_params=_cparams(("parallel",)),
        name="block_out",
    )(x, hm, ha, wo, g1, b1, wg, wu, wd, g2, b2)


def _split3(x):
    h1 = x.astype(BF16)
    r1 = x - h1.astype(F32)
    h2 = r1.astype(BF16)
    h3 = (r1 - h2.astype(F32)).astype(BF16)
    return h1, h2, h3


def _dot3(x, w):
    h1, h2, h3 = _split3(x)
    return _dot(h1, w) + _dot(h2, w) + _dot(h3, w)


def _mlstm_chunk(q, k, v, lf_row, i_row, i_col, c_aug, m_prev):
    L = q.shape[0]
    t_idx = lax.broadcasted_iota(jnp.int32, (L, L), 0)
    s_idx = lax.broadcasted_iota(jnp.int32, (L, L), 1)
    causal = s_idx <= t_idx
    ones_b = jnp.ones((L, LANES), BF16)
    G = _dot3(jnp.where(causal, lf_row, 0.0), ones_b)
    tri_u = jnp.where(t_idx <= s_idx, 1.0, 0.0).astype(BF16)
    f_row = _dot3(jnp.broadcast_to(lf_row, (SUBLANES, L)), tri_u)[0:1, :]
    f_col = G[:, 0:1]
    dlog = jnp.where(causal, f_col - f_row + i_row, NEG_INF)
    inter = m_prev + f_col
    m_t = jnp.maximum(jnp.max(dlog, axis=1, keepdims=True), inter)
    w = jnp.exp(dlog - m_t)
    g = jnp.exp(inter - m_t)
    s = _dot_nt(q, k) * w
    lane = lax.broadcasted_iota(jnp.int32, (L, LANES), 1)
    v_aug = jnp.concatenate([v, jnp.where(lane == 0, 1.0, 0.0).astype(BF16)], axis=1)
    nd = _dot(s.astype(BF16), v_aug) + g * _dot(q, c_aug.astype(BF16))
    den = nd[:, M_HEAD_DIM:M_HEAD_DIM + 1]
    h = nd[:, 0:M_HEAD_DIM] / jnp.maximum(jnp.abs(den), jnp.exp(-m_t))
    m_new = m_t[L - 1:L, :]
    g_c = g[L - 1:L, :]
    wk = jnp.exp(f_col[L - 1:L, :] - f_col + i_col - m_new)
    kw = (k.astype(F32) * wk).astype(BF16)
    c_new = g_c * c_aug + _dot_tn(kw, v_aug)
    return h, c_new, m_new


def _mlstm_prompt_kernel(*refs, nb):
    mqkv_ref, so_ref, g_ref = refs[0:3]
    gt_refs = refs[3:3 + nb]
    hm_ref, c_ref, m_ref = refs[3 + nb:]
    step = pl.program_id(0)

    @pl.when(step == 0)
    def _():
        c_ref[...] = jnp.zeros_like(c_ref)
        m_ref[...] = jnp.zeros_like(m_ref)

    for b in range(nb):
        for h in range(M_HEADS):
            hs = slice(h * M_HEAD_DIM, (h + 1) * M_HEAD_DIM)
            q = mqkv_ref[b, :, hs]
            k = mqkv_ref[b, :, M_WIDTH + h * M_HEAD_DIM:M_WIDTH + (h + 1) * M_HEAD_DIM]
            v = mqkv_ref[b, :, 2 * M_WIDTH + h * M_HEAD_DIM:2 * M_WIDTH + (h + 1) * M_HEAD_DIM]
            i_row = gt_refs[b][h:h + 1, :]
            lf_row = gt_refs[b][M_HEADS + h:M_HEADS + h + 1, :]
            i_col = g_ref[b, :, L_IG + h:L_IG + h + 1]
            r = b * M_HEADS + h
            m_prev = m_ref[r:r + 1, 0:1]
            hh, c_new, m_new = _mlstm_chunk(q, k, v, lf_row, i_row, i_col, c_ref[b, h], m_prev)
            c_ref[b, h] = c_new
            m_ref[r:r + 1, :] = jnp.broadcast_to(m_new, (1, LANES))
            hm_ref[b, :, hs] = (hh * so_ref[b, :, hs]).astype(BF16)


def _mlstm_prompt(mqkv, so, gates, gatest, nb, seq, chunk):
    nc = seq // chunk
    blk = lambda w: pl.BlockSpec((nb, chunk, w), lambda c: (0, c, 0))
    gt_specs = [pl.BlockSpec((2 * M_HEADS, chunk), lambda c, b=b: (0, b * nc + c)) for b in range(nb)]
    return pl.pallas_call(
        functools.partial(_mlstm_prompt_kernel, nb=nb),
        grid=(nc,),
        in_specs=[blk(3 * M_WIDTH), blk(M_WIDTH), blk(LANES)] + gt_specs,
        out_specs=[blk(M_WIDTH),
                   pl.BlockSpec((nb, M_HEADS, M_HEAD_DIM, 2 * M_HEAD_DIM), lambda c: (0, 0, 0, 0)),
                   pl.BlockSpec((nb * M_HEADS, LANES), lambda c: (0, 0))],
        out_shape=[jax.ShapeDtypeStruct((nb, seq, M_WIDTH), BF16),
                   jax.ShapeDtypeStruct((nb, M_HEADS, M_HEAD_DIM, 2 * M_HEAD_DIM), F32),
                   jax.ShapeDtypeStruct((nb * M_HEADS, LANES), F32)],
        compiler_params=_cparams(("arbitrary",)),
        name="mlstm_prompt",
    )(mqkv, so, gates, *([gatest] * nb))


ROWS_PAD = SUBLANES


def _mlstm_sample_kernel(mqkv_ref, so_ref, g_ref, c0_ref, n0_ref, m0_ref, hm_ref, c_ref, n_ref, m_ref, *, nseq, t_real):
    row = lax.broadcasted_iota(jnp.int32, (ROWS_PAD, 1), 0)
    real = row < t_real

    def per_seq(s, carry):
        r0 = pl.multiple_of(s * ROWS_PAD, ROWS_PAD)
        gt = g_ref[pl.ds(r0, ROWS_PAD), :]
        cum = gt
        for d in range(1, t_real):
            cum = cum + jnp.where(row >= d, pltpu.roll(gt, d, 0), 0.0)
        for h in range(M_HEADS):
            hs = slice(h * M_HEAD_DIM, (h + 1) * M_HEAD_DIM)
            q = mqkv_ref[pl.ds(r0, ROWS_PAD), hs]
            k = mqkv_ref[pl.ds(r0, ROWS_PAD), M_WIDTH + h * M_HEAD_DIM:M_WIDTH + (h + 1) * M_HEAD_DIM]
            v = mqkv_ref[pl.ds(r0, ROWS_PAD), 2 * M_WIDTH + h * M_HEAD_DIM:2 * M_WIDTH + (h + 1) * M_HEAD_DIM]
            qf, kf, vf = q.astype(F32), k.astype(F32), v.astype(F32)
            i_col = gt[:, L_IG + h:L_IG + h + 1]
            f_col = cum[:, L_FG + h:L_FG + h + 1]
            c0 = c0_ref[s, h]
            n0 = n0_ref[s, h:h + 1, :]
            m0 = m0_ref[pl.ds(s, 1), h:h + 1]
            inter = m0 + f_col
            dl = [jnp.where(real & (row >= u), f_col - f_col[u:u + 1, :] + i_col[u:u + 1, :], NEG_INF)
                  for u in range(t_real)]
            m_t = inter
            for u in range(t_real):
                m_t = jnp.maximum(m_t, dl[u])
            g = jnp.exp(inter - m_t)
            qc = _dot(q, c0.astype(BF16))
            num = g * qc
            den = g * jnp.sum(qf * n0, axis=1, keepdims=True)
            for u in range(t_real):
                su = jnp.sum(qf * kf[u:u + 1, :], axis=1, keepdims=True) * jnp.exp(dl[u] - m_t)
                num = num + su * vf[u:u + 1, :]
                den = den + su
            hh = num / jnp.maximum(jnp.abs(den), jnp.exp(-m_t))
            hm_ref[pl.ds(r0, ROWS_PAD), hs] = (hh * so_ref[pl.ds(r0, ROWS_PAD), hs]).astype(BF16)
            last = t_real - 1
            m_new = m_t[last:last + 1, :]
            g_c = g[last:last + 1, :]
            wk = jnp.where(real, jnp.exp(f_col[last:last + 1, :] - f_col + i_col - m_new), 0.0)
            kw = kf * wk
            c_ref[s, h] = g_c * c0 + _dot_tn(kw.astype(BF16), v)
            n_ref[s, h:h + 1, :] = g_c * n0 + jnp.sum(kw, axis=0, keepdims=True)
            m_ref[pl.ds(s, 1), h:h + 1] = m_new
        return carry

    lax.fori_loop(0, nseq, per_seq, 0)


def _mlstm_sample(mqkv, so, gates, c0, n0, m0, nseq_blk, t_real):
    nseq = c0.shape[0]
    rows = nseq_blk * ROWS_PAD
    rblk = lambda w: pl.BlockSpec((rows, w), lambda i: (i, 0))
    cblk = pl.BlockSpec((nseq_blk, M_HEADS, M_HEAD_DIM, M_HEAD_DIM), lambda i: (i, 0, 0, 0))
    nblk = pl.BlockSpec((nseq_blk, M_HEADS, M_HEAD_DIM), lambda i: (i, 0, 0))
    mblk = pl.BlockSpec((nseq_blk, M_HEADS), lambda i: (i, 0))
    return pl.pallas_call(
        functools.partial(_mlstm_sample_kernel, nseq=nseq_blk, t_real=t_real),
        grid=(nseq // nseq_blk,),
        in_specs=[rblk(3 * M_WIDTH), rblk(M_WIDTH), rblk(LANES), cblk, nblk, mblk],
        out_specs=[rblk(M_WIDTH), cblk, nblk, mblk],
        out_shape=[jax.ShapeDtypeStruct((nseq * ROWS_PAD, M_WIDTH), BF16),
                   jax.ShapeDtypeStruct(c0.shape, F32),
                   jax.ShapeDtypeStruct(n0.shape, F32),
                   jax.ShapeDtypeStruct(m0.shape, F32)],
        compiler_params=_cparams(("parallel",)),
        name="mlstm_sample",
    )(mqkv, so, gates, c0, n0, m0)


N_BISECT = 19
MASKED = -1e30
ROW_BLOCK = 128
SELECT_TK = 512
NORM_TK = 1024
MIN_ROW_SUM = 2.0 ** -100


def _block_pass(sc_ref, r0, rb, nch, tk, init, fn, params=()):
    def body(c, acc):
        c0 = pl.multiple_of(c * tk, tk)
        blk = sc_ref[r0:r0 + rb, pl.ds(c0, tk)]
        for j in range(tk // LANES):
            acc = fn(acc, blk[:, j * LANES:(j + 1) * LANES], c0 + j * LANES, params)
        return acc

    return lax.fori_loop(0, nch, body, init)


def _row_pass(sc_ref, nch, tk, init, fn, params=()):
    rows = sc_ref.shape[0]
    rb = min(ROW_BLOCK, rows)
    rsl = lambda t, r0: jax.tree.map(lambda a: a[r0:r0 + rb], t)
    outs = [_block_pass(sc_ref, r0, rb, nch, tk, rsl(init, r0), fn, rsl(params, r0)) for r0 in range(0, rows, rb)]
    return jax.tree.map(lambda *a: jnp.concatenate(a, axis=0), *outs)


def _select_threshold(sc_ref, nch, tk, kr, smax, stats=None):
    rows = sc_ref.shape[0]
    zeros = jnp.zeros((rows, LANES), F32)
    lsum = lambda a: jnp.sum(a, axis=1, keepdims=True)
    full = lambda col: jnp.broadcast_to(col, (rows, LANES))

    def count_ge(thr):
        return lsum(_row_pass(sc_ref, nch, tk, zeros, lambda a, x, _, t: a + jnp.where(x >= t, 1.0, 0.0), full(thr)))

    def range_fn(a, x, _, prm):
        fin = x > NEG_INF
        return (jnp.maximum(a[0], x), jnp.minimum(a[1], jnp.where(fin, x, jnp.inf)), a[2] + jnp.where(fin, 1.0, 0.0))

    if stats is None:
        mx, mn, nf = _row_pass(
            sc_ref, nch, tk, (jnp.full((rows, LANES), NEG_INF, F32), jnp.full((rows, LANES), jnp.inf, F32), zeros),
            range_fn)
        rmax, rmin, n_fin = jnp.max(mx, axis=1, keepdims=True), jnp.min(mn, axis=1, keepdims=True), lsum(nf)
    else:
        rmax, rmin, n_fin = stats

    above = rmax + jnp.maximum(jnp.abs(rmax) * (2.0 ** -20), 2.0 ** -100)

    def bisect(_, st):
        lo, hi, c_lo, c_hi = st
        mid = 0.5 * (lo + hi)
        c = count_ge(mid)
        ge = c >= kr
        return jnp.where(ge, mid, lo), jnp.where(ge, hi, mid), jnp.where(ge, c, c_lo), jnp.where(ge, c_hi, c)

    lo, hi, c_lo, c_hi = lax.fori_loop(0, N_BISECT, bisect, (rmin, above, n_fin, jnp.zeros_like(rmax)))

    rb = min(ROW_BLOCK, rows)
    tau_b, c_gt_b, c_ge_b = [], [], []
    for r0 in range(0, rows, rb):
        blk = lambda a, r0=r0: a[r0:r0 + rb]
        kr_b = blk(kr)
        bfull = lambda col: jnp.broadcast_to(col, (rb, LANES))

        def count_ge_b(thr, r0=r0):
            return lsum(_block_pass(sc_ref, r0, rb, nch, tk, jnp.zeros((rb, LANES), F32),
                                    lambda a, x, _, t: a + jnp.where(x >= t, 1.0, 0.0), bfull(thr)))

        def max_below_b(thr, r0=r0):
            acc = _block_pass(sc_ref, r0, rb, nch, tk, jnp.full((rb, LANES), NEG_INF, F32),
                              lambda a, x, _, t: jnp.maximum(a, jnp.where(x < t, x, NEG_INF)), bfull(thr))
            return jnp.max(acc, axis=1, keepdims=True)

        def peel_cond(st):
            it, _, _, _, _, _, done = st
            return jnp.logical_and(it <= smax, jnp.min(done) < 0.5)

        def peel(st, kr_b=kr_b, count_ge_b=count_ge_b, max_below_b=max_below_b):
            it, hi_b, c_hi_b, tau, c_gt, c_ge, done = st
            t1 = max_below_b(hi_b)
            c1 = count_ge_b(t1)
            fin = jnp.logical_and(c1 >= kr_b, done < 0.5)
            tau = jnp.where(fin, t1, tau)
            c_gt = jnp.where(fin, c_hi_b, c_gt)
            c_ge = jnp.where(fin, c1, c_ge)
            done = jnp.where(fin, 1.0, done)
            live = done < 0.5
            return it + 1, jnp.where(live, t1, hi_b), jnp.where(live, c1, c_hi_b), tau, c_gt, c_ge, done

        done0 = jnp.where(blk(c_lo) - kr_b > 0.5, 0.0, 1.0)
        st = (jnp.int32(0), blk(hi), blk(c_hi), blk(lo), jnp.zeros((rb, 1), F32), kr_b, done0)
        _, _, _, t_b, g_b, e_b, _ = lax.while_loop(peel_cond, peel, st)
        tau_b.append(t_b)
        c_gt_b.append(g_b)
        c_ge_b.append(e_b)
    tau, c_gt, c_ge = (jnp.concatenate(v, axis=0) for v in (tau_b, c_gt_b, c_ge_b))

    need = kr - c_gt
    surplus = (c_ge - c_gt) - need
    k_idx = lax.broadcasted_iota(jnp.int32, (LANES, 2 * LANES), 0)
    j_idx = lax.broadcasted_iota(jnp.int32, (LANES, 2 * LANES), 1)
    tri_ones = jnp.where(jnp.logical_or(k_idx <= j_idx, j_idx >= LANES), 1.0, 0.0).astype(BF16)
    for r0 in range(0, rows, rb):
        @pl.when(jnp.max(surplus[r0:r0 + rb]) > 0.5)
        def _(r0=r0):
            tau_f = jnp.broadcast_to(tau[r0:r0 + rb], (rb, LANES))
            need_f = jnp.broadcast_to(need[r0:r0 + rb], (rb, LANES))

            def sweep(c, seen):
                c0 = pl.multiple_of(c * tk, tk)
                x = sc_ref[r0:r0 + rb, pl.ds(c0, tk)]
                tiles = [x[:, j * LANES:(j + 1) * LANES] for j in range(tk // LANES)]
                cnts = [_dot(jnp.where(t == tau_f, 1.0, 0.0).astype(BF16), tri_ones) for t in tiles]
                out = []
                for t, cnt in zip(tiles, cnts):
                    rank = seen + cnt[:, 0:LANES]
                    out.append(jnp.where(jnp.logical_and(t == tau_f, rank > need_f), NEG_INF, t))
                    seen = seen + cnt[:, LANES:2 * LANES]
                sc_ref[r0:r0 + rb, pl.ds(c0, tk)] = jnp.concatenate(out, axis=1)
                return seen

            lax.fori_loop(0, nch, sweep, jnp.zeros((rb, LANES), F32))

    return tau


def _ones_rows(tk):
    return jnp.where(lax.broadcasted_iota(jnp.int32, (LANES, tk), 0) == 0, 1.0, 0.0).astype(BF16)


def _dsa_prompt_kernel(qi_ref, g_ref, q_ref, kit_ref, kt_ref, vt_ref, o_ref, sc_ref, tau_scr, m_scr, acc_scr, kn_scr,
                       *, tq, tk, topk, smax):
    i = pl.program_id(1)
    nch = ((i + 1) * tq + tk - 1) // tk
    row = lax.broadcasted_iota(jnp.int32, (tq, 1), 0) + i * tq

    m_scr[0] = jnp.full((tq, LANES), NEG_INF, F32)
    m_scr[1] = jnp.full((tq, LANES), jnp.inf, F32)

    def score_chunk(c, carry):
        c0 = pl.multiple_of(c * tk, tk)
        kblk = kit_ref[:, pl.ds(c0, tk)]
        acc = jnp.zeros((tq, tk), F32)
        for h in range(IDX_HEADS):
            x = _dot(qi_ref[:, h * LANES:(h + 1) * LANES], kblk)
            acc = acc + jnp.maximum(x, 0.0) * g_ref[:, L_WI + h:L_WI + h + 1]
        col = lax.broadcasted_iota(jnp.int32, (1, LANES), 1) + c0
        hi_p, lo_p = m_scr[0], m_scr[1]
        for j in range(tk // LANES):
            causal = col + j * LANES <= row
            a = acc[:, j * LANES:(j + 1) * LANES]
            masked = jnp.where(causal, a, NEG_INF)
            sc_ref[:, pl.ds(pl.multiple_of(c0 + j * LANES, LANES), LANES)] = masked
            hi_p = jnp.maximum(hi_p, masked)
            lo_p = jnp.minimum(lo_p, jnp.where(causal, a, jnp.inf))
        m_scr[0], m_scr[1] = hi_p, lo_p
        return carry

    lax.fori_loop(0, nch, score_chunk, 0)

    kr = jnp.minimum(row + 1, topk).astype(F32)
    stats = (jnp.max(m_scr[0], axis=1, keepdims=True), jnp.min(m_scr[1], axis=1, keepdims=True), (row + 1).astype(F32))
    tau = _select_threshold(sc_ref, nch * (tk // SELECT_TK), SELECT_TK, kr, smax, stats)

    nslab = tk // LANES
    tau_scr[...] = jnp.broadcast_to(tau, (tq, LANES))

    def masked_logits(c0, h):
        pr = slice((h // 2) * LANES, (h // 2 + 1) * LANES)
        s = _dot(q_ref[:, h * LANES:(h + 1) * LANES], kt_ref[pr, pl.ds(c0, tk)])
        x = sc_ref[:, pl.ds(c0, tk)]
        thr = tau_scr[...]
        return [jnp.where(x[:, j * LANES:(j + 1) * LANES] >= thr, s[:, j * LANES:(j + 1) * LANES], MASKED)
                for j in range(nslab)]

    def max_sweep():
        m_scr[...] = jnp.full(m_scr.shape, MASKED, F32)

        def max_chunk(c, carry):
            c0 = pl.multiple_of(c * tk, tk)
            for h in range(A_HEADS):
                slabs = masked_logits(c0, h)
                part = slabs[0]
                for j in range(1, nslab):
                    part = jnp.maximum(part, slabs[j])
                m_scr[h] = jnp.maximum(m_scr[h], part)
            return carry

        lax.fori_loop(0, nch, max_chunk, 0)
        for h in range(A_HEADS):
            m_scr[h] = jnp.broadcast_to(jnp.max(m_scr[h], axis=1, keepdims=True), (tq, LANES))

    ones = _ones_rows(tk)

    def attend_sweep():
        acc_scr[...] = jnp.zeros(acc_scr.shape, F32)

        def attend_chunk(c, carry):
            c0 = pl.multiple_of(c * tk, tk)
            for h in range(A_HEADS):
                pr = slice((h // 2) * LANES, (h // 2 + 1) * LANES)
                m_h = m_scr[h]
                pb = jnp.concatenate([jnp.exp2(sl - m_h).astype(BF16) for sl in masked_logits(c0, h)], axis=1)
                vt_aug = jnp.concatenate([vt_ref[pr, pl.ds(c0, tk)], ones], axis=0)
                acc_scr[h] = acc_scr[h] + _dot_nt(pb, vt_aug)
            return carry

        lax.fori_loop(0, nch, attend_chunk, 0)

    @pl.when(i == 0)
    def _():
        for h in range(A_HEADS):
            def norm_chunk(c, best, h=h):
                c0 = pl.multiple_of(c * NORM_TK, NORM_TK)
                kk = kt_ref[h * A_HEAD_DIM:(h + 1) * A_HEAD_DIM, pl.ds(c0, NORM_TK)].astype(F32)
                return jnp.maximum(best, jnp.sum(kk * kk, axis=0, keepdims=True))
            best = lax.fori_loop(0, smax // NORM_TK, norm_chunk, jnp.zeros((1, NORM_TK), F32))
            kn_scr[h:h + 1, :] = jnp.broadcast_to(jnp.max(best, axis=1, keepdims=True), (1, LANES))

    for h in range(A_HEADS):
        qh = q_ref[:, h * LANES:(h + 1) * LANES].astype(F32)
        n2 = jnp.sum(qh * qh, axis=1, keepdims=True) * kn_scr[h:h + 1, 0:1]
        m_scr[h] = jnp.broadcast_to(jnp.where(n2 > 0.0, n2 * lax.rsqrt(n2), 0.0), (tq, LANES))
    attend_sweep()
    l_min = acc_scr[0][:, LANES:LANES + 1]
    for h in range(1, A_HEADS):
        l_min = jnp.minimum(l_min, acc_scr[h][:, LANES:LANES + 1])

    @pl.when(jnp.logical_not(jnp.min(l_min) >= MIN_ROW_SUM))
    def _():
        max_sweep()
        attend_sweep()

    lo_half = lax.broadcasted_iota(jnp.int32, (1, LANES), 1) < A_HEAD_DIM
    for p in range(A_HEADS // 2):
        a_e, a_o = acc_scr[2 * p], acc_scr[2 * p + 1]
        even = a_e[:, 0:LANES] * (1.0 / a_e[:, LANES:LANES + 1])
        odd = a_o[:, 0:LANES] * (1.0 / a_o[:, LANES:LANES + 1])
        o_ref[:, p * LANES:(p + 1) * LANES] = jnp.where(lo_half, even, odd).astype(BF16)


def _dsa_prompt(qipad, gates, qpad, kit2, ktb, vtb, nb, seq, tq, tk):
    nq = seq // tq
    topk = min(TOPK_MAX, seq // 4)
    qrow = lambda w: pl.BlockSpec((tq, w), lambda b, i: (b * nq + i, 0))
    kfull = lambda r: pl.BlockSpec((None, r, seq), lambda b, i: (b, 0, 0), pipeline_mode=pl.Buffered(1))
    return pl.pallas_call(
        functools.partial(_dsa_prompt_kernel, tq=tq, tk=tk, topk=topk, smax=seq),
        grid=(nb, nq),
        in_specs=[qrow(2 * A_WIDTH), qrow(LANES), qrow(2 * A_WIDTH), kfull(2 * IDX_DIM), kfull(A_WIDTH), kfull(A_WIDTH)],
        out_specs=qrow(A_WIDTH),
        out_shape=jax.ShapeDtypeStruct((nb * seq, A_WIDTH), BF16),
        scratch_shapes=[pltpu.VMEM((tq, seq + LANES), F32),
                        pltpu.VMEM((tq, LANES), F32),
                        pltpu.VMEM((A_HEADS, tq, LANES), F32),
                        pltpu.VMEM((A_HEADS, tq, 2 * LANES), F32),
                        pltpu.VMEM((A_HEADS, LANES), F32)],
        compiler_params=_cparams(("parallel", "arbitrary")),
        name="dsa_prompt",
    )(qipad, gates, qpad, kit2, ktb, vtb)


def _dsa_sample_scores_kernel(pt_ref, q_ref, w_ref, *refs, n_pages, t_real):
    page_refs, new_ref, o_ref = refs[:n_pages], refs[n_pages], refs[n_pages + 1]
    q = q_ref[0]
    w = w_ref[0]
    ncol = (n_pages + 1) * PAGE_SIZE
    keys_t = jnp.concatenate([pr[0].astype(BF16) for pr in page_refs] + [new_ref[0]], axis=1)
    r = jnp.maximum(_dot(q, keys_t), 0.0) * w
    sc = jnp.sum(r.reshape(t_real, IDX_HEADS, ncol), axis=1)
    col = lax.broadcasted_iota(jnp.int32, (1, ncol), 1) - n_pages * PAGE_SIZE
    trow = lax.broadcasted_iota(jnp.int32, (t_real, 1), 0)
    o_ref[0] = jnp.where(col <= trow, sc, NEG_INF)


def _dsa_sample_scores(page_table, qi, wi, cache_kidx_t, ki_new_t, t_real):
    nseq, n_pages = page_table.shape
    ncol = (n_pages + 1) * PAGE_SIZE
    per_seq = lambda a: pl.BlockSpec((1,) + a.shape[1:], lambda b, pt: (b, 0, 0))
    page = lambda p: pl.BlockSpec((1, IDX_DIM, PAGE_SIZE), lambda b, pt, p=p: (pt[b * n_pages + p], 0, 0))
    return pl.pallas_call(
        functools.partial(_dsa_sample_scores_kernel, n_pages=n_pages, t_real=t_real),
        grid_spec=pltpu.PrefetchScalarGridSpec(
            num_scalar_prefetch=1,
            grid=(nseq,),
            in_specs=[per_seq(qi), per_seq(wi)] + [page(p) for p in range(n_pages)] + [per_seq(ki_new_t)],
            out_specs=pl.BlockSpec((1, t_real, ncol), lambda b, pt: (b, 0, 0)),
        ),
        out_shape=jax.ShapeDtypeStruct((nseq, t_real, ncol), F32),
        compiler_params=_cparams(("parallel",)),
        name="dsa_sample_scores",
    )(page_table.reshape(-1), qi, wi, *([cache_kidx_t] * n_pages), ki_new_t)


def _dsa_sample_select_kernel(sc_ref, adj_ref, tau_ref, *, tk, topk):
    adj_ref[...] = sc_ref[...]
    rows, ncol = adj_ref.shape
    kr = jnp.full((rows, 1), float(topk), F32)
    tau = _select_threshold(adj_ref, ncol // tk, tk, kr, ncol)
    tau_ref[...] = jnp.broadcast_to(tau, (rows, LANES))


def _dsa_sample_select(sc, topk, rblk):
    rows, ncol = sc.shape
    return pl.pallas_call(
        functools.partial(_dsa_sample_select_kernel, tk=LANES, topk=topk),
        grid=(rows // rblk,),
        in_specs=[pl.BlockSpec((rblk, ncol), lambda i: (i, 0))],
        out_specs=[pl.BlockSpec((rblk, ncol), lambda i: (i, 0)), pl.BlockSpec((rblk, LANES), lambda i: (i, 0))],
        out_shape=[jax.ShapeDtypeStruct((rows, ncol), F32), jax.ShapeDtypeStruct((rows, LANES), F32)],
        compiler_params=_cparams(("parallel",)),
        name="dsa_sample_select",
    )(sc)


def _dsa_sample_attend_kernel(pt_ref, q_ref, sc_ref, tau_ref, *refs, n_pages, t_real):
    kp, vp = refs[:n_pages], refs[n_pages:2 * n_pages]
    knew_ref, vnew_ref, o_ref = refs[2 * n_pages:]
    q = q_ref[0]
    nrow = t_real * A_HEADS
    rows_of = lambda a: jnp.broadcast_to(a[:, None, :], (t_real, A_HEADS, a.shape[-1])).reshape(nrow, a.shape[-1])
    all_pages = lambda pages, new: jnp.concatenate(
        [pr[0].reshape(A_WIDTH, PAGE_SIZE).astype(BF16) for pr in pages] + [new[0]], axis=1)
    s = jnp.where(rows_of(sc_ref[0]) >= rows_of(tau_ref[0][:, 0:1]), _dot(q, all_pages(kp, knew_ref)), MASKED)
    pr = jnp.exp2(s - jnp.max(s, axis=1, keepdims=True))
    l = jnp.sum(pr, axis=1, keepdims=True)
    out = _dot_nt(pr.astype(BF16), all_pages(vp, vnew_ref)) / l
    head_of_row = lax.broadcasted_iota(jnp.int32, (nrow, 1), 0) % A_HEADS
    head_of_lane = lax.broadcasted_iota(jnp.int32, (1, A_WIDTH), 1) // A_HEAD_DIM
    out = jnp.where(head_of_row == head_of_lane, out, 0.0)
    o_ref[0] = jnp.sum(out.reshape(t_real, A_HEADS, A_WIDTH), axis=1).astype(BF16)


def _dsa_sample_attend(page_table, qbd, sc_adj, tau, cache_k_t, cache_v_t, k_new_t, v_new_t, t_real):
    nseq, n_pages = page_table.shape
    ncol = (n_pages + 1) * PAGE_SIZE
    per_seq = lambda a: pl.BlockSpec((1,) + a.shape[1:], lambda b, pt: (b, 0, 0))
    page = lambda p: pl.BlockSpec((1, A_HEADS, A_HEAD_DIM, PAGE_SIZE), lambda b, pt, p=p: (pt[b * n_pages + p], 0, 0, 0))
    pages = lambda: [page(p) for p in range(n_pages)]
    return pl.pallas_call(
        functools.partial(_dsa_sample_attend_kernel, n_pages=n_pages, t_real=t_real),
        grid_spec=pltpu.PrefetchScalarGridSpec(
            num_scalar_prefetch=1,
            grid=(nseq,),
            in_specs=[per_seq(qbd), per_seq(sc_adj), per_seq(tau)] + pages() + pages() + [per_seq(k_new_t), per_seq(v_new_t)],
            out_specs=pl.BlockSpec((1, t_real, A_WIDTH), lambda b, pt: (b, 0, 0)),
        ),
        out_shape=jax.ShapeDtypeStruct((nseq, t_real, A_WIDTH), BF16),
        compiler_params=_cparams(("parallel",)),
        name="dsa_sample_attend",
    )(page_table.reshape(-1), qbd, sc_adj, tau, *([cache_k_t] * n_pages), *([cache_v_t] * n_pages), k_new_t, v_new_t)


PROJ_TM = 256
OUT_TM = 256
MLSTM_CHUNK = 256
DSA_TQ = 512
DSA_TK = 512
SAMPLE_SEQ_BLK = 8
SELECT_ROWS = 128


def _unpad_heads(xpad, n_heads):
    x = xpad.reshape(xpad.shape[:-1] + (n_heads, 2, LANES // 2))
    return x[..., 0, :] + x[..., 1, :]


def kernel(x_prompt, x_sample, cache_k, cache_v, cache_kidx, state_C, state_n, state_m, page_table,
           w_in, b_gate, w_out, ln1_g, ln1_b, w_gate, w_up, w_down, ln2_g, ln2_b):
    bp, sp, _ = x_prompt.shape
    bs, ts, _ = x_sample.shape
    n_pages = page_table.shape[1]
    past = n_pages * PAGE_SIZE
    assert DEPTH == 1 and w_in.shape[0] == 1

    w = _prep_w_in(w_in[0], b_gate[0])
    wo, wg, wu, wd = (a[0].astype(BF16) for a in (w_out, w_gate, w_up, w_down))
    g1, b1, g2, b2 = (v[0].astype(F32)[None, :] for v in (ln1_g, ln1_b, ln2_g, ln2_b))

    mp = bp * sp
    xp = x_prompt.reshape(mp, D_MODEL)
    tabs_p, tabs_pt = _rope_tables(jnp.arange(sp, dtype=jnp.int32))
    (mqkv, so, gates, gatest, qpad, qipad, kt_p, ktb, vt_p, vtb, kit_p, kit2) = _proj(
        xp, w, tabs_p, tabs_pt, PROJ_TM, bp, sp)
    hm_p, caug, m_p = _mlstm_prompt(mqkv.reshape(bp, sp, -1), so.reshape(bp, sp, -1), gates.reshape(bp, sp, -1),
                                    gatest, bp, sp, MLSTM_CHUNK)
    ha_p = _dsa_prompt(qipad, gates, qpad, kit2, ktb, vtb, bp, sp, DSA_TQ, DSA_TK)
    y_p = _block_out(xp, hm_p.reshape(mp, -1), ha_p, wo, g1, b1, wg, wu, wd, g2, b2, OUT_TM)

    ms = bs * ts
    xs = x_sample.reshape(ms, D_MODEL)
    tabs_s, tabs_st = _rope_tables(jnp.tile(past + jnp.arange(ts, dtype=jnp.int32), bs))
    (mqkv_s, so_s, gates_s, _, qpad_s, qipad_s, kt_s, ktb_s, vt_s, vtb_s, kit_s, _) = _proj(
        xs, w, tabs_s, tabs_st, ms, 1, ms)
    pad_rows = lambda a: jnp.pad(a.reshape(bs, ts, -1), ((0, 0), (0, ROWS_PAD - ts), (0, 0))).reshape(bs * ROWS_PAD, -1)
    hm_s, c_s, n_s, m_s = _mlstm_sample(pad_rows(mqkv_s), pad_rows(so_s), pad_rows(gates_s),
                                        state_C[0].astype(F32), state_n[0].astype(F32), state_m[0].astype(F32),
                                        SAMPLE_SEQ_BLK, ts)
    hm_s = hm_s.reshape(bs, ROWS_PAD, -1)[:, :ts].reshape(ms, -1)

    new_page = lambda a: jnp.pad(a.reshape(a.shape[0], bs, ts).transpose(1, 0, 2), ((0, 0), (0, 0), (0, PAGE_SIZE - ts)))
    qi_s = _unpad_heads(qipad_s, IDX_HEADS).reshape(bs, ts * IDX_HEADS, IDX_DIM)
    wi_s = gates_s[:, L_WI:L_WI + IDX_HEADS].reshape(bs, ts * IDX_HEADS, 1)
    sc_s = _dsa_sample_scores(page_table, qi_s, wi_s, cache_kidx[0].transpose(0, 2, 1), new_page(kit_s[0]).astype(BF16), ts)
    ncol = sc_s.shape[-1]
    topk_s = min(TOPK_MAX, (past + ts) // 4)
    sc_adj, tau_s = _dsa_sample_select(sc_s.reshape(ms, ncol), topk_s, SELECT_ROWS)
    q_s = _unpad_heads(qpad_s, A_HEADS).reshape(bs, ts, A_HEADS, A_HEAD_DIM)
    eye = jnp.eye(A_HEADS, dtype=q_s.dtype)
    qbd = (q_s[:, :, :, None, :] * eye[None, None, :, :, None]).reshape(bs, ts * A_HEADS, A_WIDTH)
    ha_s = _dsa_sample_attend(page_table, qbd, sc_adj.reshape(bs, ts, ncol), tau_s.reshape(bs, ts, LANES),
                              cache_k[0].transpose(0, 2, 3, 1), cache_v[0].transpose(0, 2, 3, 1),
                              new_page(ktb_s[0]), new_page(vtb_s[0]), ts)
    y_s = _block_out(xs, hm_s, ha_s.reshape(ms, -1), wo, g1, b1, wg, wu, wd, g2, b2, min(OUT_TM, ms))

    heads = lambda a, b, t: a.reshape(1, b, A_HEADS, A_HEAD_DIM, t).transpose(0, 1, 4, 2, 3)
    heads_s = lambda a: a[0].T.reshape(1, bs, ts, A_HEADS, A_HEAD_DIM)
    return (y_p.reshape(bp, sp, D_MODEL), y_s.reshape(bs, ts, D_MODEL),
            heads(kt_p, bp, sp), heads(vt_p, bp, sp), kit_p.transpose(0, 2, 1)[None],
            caug[None, :, :, :, 0:M_HEAD_DIM], caug[None, :, :, :, M_HEAD_DIM], m_p[:, 0].reshape(1, bp, M_HEADS),
            heads_s(kt_s), heads_s(vt_s), kit_s[0].T.reshape(1, bs, ts, IDX_DIM),
            c_s[None], n_s[None], m_s[None])
```

```python
import functools
import math

import jax
import jax.numpy as jnp
from jax import lax
from jax.experimental import pallas as pl
from jax.experimental.pallas import tpu as pltpu

D_MODEL = 1024
M_HEADS = 4
M_HEAD_DIM = 128
M_WIDTH = 512
A_HEADS = 8
A_HEAD_DIM = 64
A_WIDTH = 512
IDX_HEADS = 8
IDX_DIM = 64
TOPK_MAX = 256
PAGE_SIZE = 128
ROPE_THETA = 500000.0
ROT = A_HEAD_DIM // 4
HALF = ROT // 2
D_FF = 2816
DEPTH = 1
ALPHA = (2 * DEPTH) ** 0.25
LN_EPS = 1e-5
LOG2E = math.log2(math.e)

LANES = 128
SUBLANES = 8
VMEM_LIMIT = 56 * 1024 * 1024

NEG_INF = float("-inf")
BF16 = jnp.bfloat16
F32 = jnp.float32

C_MQ, C_MK, C_MV, C_MO = 0, 512, 1024, 1536
C_AQ, C_IQ, C_SM = 2048, 2560, 3072
N_PERM = 3200
L_WI, L_IG, L_FG = 64, 72, 76
R_AK, R_AV, R_IK, N_ROWS_T = 0, 512, 1024, 1152


def _cparams(sem):
    return pltpu.CompilerParams(dimension_semantics=sem, vmem_limit_bytes=VMEM_LIMIT)


def _dot(a, b):
    return jnp.dot(a, b, preferred_element_type=F32)


def _dot_nt(a, b):
    return lax.dot_general(a, b, (((1,), (1,)), ((), ())), preferred_element_type=F32)


def _dot_tn(a, b):
    return lax.dot_general(a, b, (((0,), (0,)), ((), ())), preferred_element_type=F32)


def _rope128(x, cos, sa, sb):
    return x * cos + pltpu.roll(x, LANES - HALF, 1) * sa + pltpu.roll(x, HALF, 1) * sb


def _rope_rows(z, cos_t, sin_t):
    a, b = z[0:HALF], z[HALF:ROT]
    return jnp.concatenate([a * cos_t - b * sin_t, b * cos_t + a * sin_t, z[ROT:]], axis=0)


def _proj_kernel(x_ref, w_ref, wt_ref, wgt_ref, bias_ref, biast_ref, cos_ref, sa_ref, sb_ref, cost_ref, sint_ref,
                 mqkv_ref, so_ref, gates_ref, gatest_ref, qpad_ref, qipad_ref,
                 kt_ref, ktb_ref, vt_ref, vtb_ref, kit_ref, kit2_ref):
    xb = x_ref[...].astype(BF16)
    cos, sa, sb = cos_ref[...], sa_ref[...], sb_ref[...]
    cos_t, sin_t = cost_ref[...], sint_ref[...]
    lane = lax.broadcasted_iota(jnp.int32, (1, LANES), 1)
    lo_half = lane < A_HEAD_DIM

    zm = _dot(xb, w_ref[:, C_MQ:C_MO])
    mqkv_ref[:, 0:M_WIDTH] = zm[:, 0:M_WIDTH].astype(BF16)
    mqkv_ref[:, M_WIDTH:2 * M_WIDTH] = (zm[:, M_WIDTH:2 * M_WIDTH] * (M_HEAD_DIM ** -0.5)).astype(BF16)
    mqkv_ref[:, 2 * M_WIDTH:3 * M_WIDTH] = zm[:, 2 * M_WIDTH:3 * M_WIDTH].astype(BF16)
    so_ref[...] = jax.nn.sigmoid(_dot(xb, w_ref[:, C_MO:C_AQ]))

    def padded_heads(z, scale, out_ref):
        for p in range(A_WIDTH // LANES):
            r = _rope128(z[:, p * LANES:(p + 1) * LANES], cos, sa, sb) * scale
            out_ref[:, (2 * p) * LANES:(2 * p + 1) * LANES] = jnp.where(lo_half, r, 0.0).astype(BF16)
            out_ref[:, (2 * p + 1) * LANES:(2 * p + 2) * LANES] = jnp.where(lo_half, 0.0, r).astype(BF16)

    padded_heads(_dot(xb, w_ref[:, C_AQ:C_IQ]), (A_HEAD_DIM ** -0.5) * LOG2E, qpad_ref)
    padded_heads(_dot(xb, w_ref[:, C_IQ:C_SM]), IDX_DIM ** -0.5, qipad_ref)

    zs = _dot(xb, w_ref[:, C_SM:N_PERM]) + bias_ref[...]
    is_wi = (lane >= L_WI) & (lane < L_IG)
    is_fg = (lane >= L_FG) & (lane < L_FG + M_HEADS)
    g = jnp.where(is_wi, zs * (IDX_HEADS ** -0.5), zs)
    gates_ref[...] = jnp.where(is_fg, jax.nn.log_sigmoid(zs), g)

    zt = _dot_nt(wgt_ref[...], xb) + biast_ref[...]
    row = lax.broadcasted_iota(jnp.int32, (2 * M_HEADS, 1), 0)
    gatest_ref[...] = jnp.where(row >= M_HEADS, jax.nn.log_sigmoid(zt), zt)

    zkt = _dot_nt(wt_ref[R_AK:R_AV, :], xb)
    for h in range(A_HEADS):
        r = _rope_rows(zkt[h * A_HEAD_DIM:(h + 1) * A_HEAD_DIM], cos_t, sin_t)
        kt_ref[h * A_HEAD_DIM:(h + 1) * A_HEAD_DIM, :] = r
        ktb_ref[h * A_HEAD_DIM:(h + 1) * A_HEAD_DIM, :] = r.astype(BF16)
    zvt = _dot_nt(wt_ref[R_AV:R_IK, :], xb)
    vt_ref[...] = zvt
    vtb_ref[...] = zvt.astype(BF16)
    zit = _dot_nt(wt_ref[R_IK:N_ROWS_T, :], xb)
    r = _rope_rows(zit[0:IDX_DIM], cos_t, sin_t)
    kit_ref[...] = r
    kit2_ref[...] = jnp.concatenate([r, r], axis=0).astype(BF16)


def _rope_tables(pos):
    inv = ROPE_THETA ** (-jnp.arange(HALF, dtype=F32) / HALF)
    posf = pos.astype(F32)
    ang_t = inv[:, None] * posf[None, :]
    dim = jnp.arange(LANES) % A_HEAD_DIM
    ang = posf[:, None] * inv[dim % HALF][None, :]
    c, s = jnp.cos(ang), jnp.sin(ang)
    cos_m = jnp.where(dim < ROT, c, 1.0)
    sa_m = jnp.where(dim < HALF, -s, 0.0)
    sb_m = jnp.where((dim >= HALF) & (dim < ROT), s, 0.0)
    return (cos_m, sa_m, sb_m), (jnp.cos(ang_t), jnp.sin(ang_t))


def _proj(x, w, tabs, tabs_t, tm, nb, seq):
    wp, wt, wgt, bias, biast = w
    m = x.shape[0]
    nblk = seq // tm
    row = lambda wd: pl.BlockSpec((tm, wd), lambda i: (i, 0))
    full = lambda a: pl.BlockSpec(a.shape, lambda i: (0, 0))
    tab = pl.BlockSpec((tm, LANES), lambda i: (i % nblk, 0))
    tab_t = pl.BlockSpec((HALF, tm), lambda i: (0, i % nblk))
    fmaj = lambda r: pl.BlockSpec((None, r, tm), lambda i: (i // nblk, 0, i % nblk))
    outs = [
        ((m, 3 * M_WIDTH), BF16, row(3 * M_WIDTH)),
        ((m, M_WIDTH), F32, row(M_WIDTH)),
        ((m, LANES), F32, row(LANES)),
        ((2 * M_HEADS, m), F32, pl.BlockSpec((2 * M_HEADS, tm), lambda i: (0, i))),
        ((m, 2 * A_WIDTH), BF16, row(2 * A_WIDTH)),
        ((m, 2 * A_WIDTH), BF16, row(2 * A_WIDTH)),
        ((nb, A_WIDTH, seq), F32, fmaj(A_WIDTH)),
        ((nb, A_WIDTH, seq), BF16, fmaj(A_WIDTH)),
        ((nb, A_WIDTH, seq), F32, fmaj(A_WIDTH)),
        ((nb, A_WIDTH, seq), BF16, fmaj(A_WIDTH)),
        ((nb, IDX_DIM, seq), F32, fmaj(IDX_DIM)),
        ((nb, 2 * IDX_DIM, seq), BF16, fmaj(2 * IDX_DIM)),
    ]
    return pl.pallas_call(
        _proj_kernel,
        grid=(m // tm,),
        in_specs=[row(D_MODEL), full(wp), full(wt), full(wgt), full(bias), full(biast), tab, tab, tab, tab_t, tab_t],
        out_specs=[o[2] for o in outs],
        out_shape=[jax.ShapeDtypeStruct(o[0], o[1]) for o in outs],
        compiler_params=_cparams(("parallel",)),
        name="proj",
    )(x, wp, wt, wgt, bias, biast, *tabs, *tabs_t)


def _prep_w_in(w_in, b_gate):
    s = [0, 512, 1024, 1536, 2048, 2052, 2056, 2568, 3080, 3592, 4104, 4168, 4176]
    w16 = w_in.astype(BF16)
    mq, mk, mv, mo, mi, mf, aq, ak, av, iq, ik, iw = [w16[:, s[i]:s[i + 1]] for i in range(12)]
    z64 = jnp.zeros((D_MODEL, L_WI), BF16)
    pad = jnp.zeros((D_MODEL, N_PERM - C_SM - L_FG - M_HEADS), BF16)
    wp = jnp.concatenate([mq, mk, mv, mo, aq, iq, z64, iw, mi, mf, pad], axis=1)
    wt = jnp.concatenate([ak, av, ik, ik], axis=1).T
    wgt = jnp.concatenate([mi, mf], axis=1).T
    bg = b_gate.astype(F32)
    bias = jnp.zeros((1, LANES), F32).at[0, L_IG:L_IG + 2 * M_HEADS].set(bg)
    return wp, wt, wgt, bias, bg[:, None]


FF_CHUNK = D_FF // 2


def _layer_norm(x, g, b):
    mu = jnp.mean(x, axis=-1, keepdims=True)
    xc = x - mu
    var = jnp.mean(xc * xc, axis=-1, keepdims=True)
    return xc * lax.rsqrt(var + LN_EPS) * g + b


def _block_out_kernel(x_ref, hm_ref, ha_ref, wo_ref, g1_ref, b1_ref, wg_ref, wu_ref, wd_ref, g2_ref, b2_ref, y_ref):
    mix = _dot(hm_ref[...], wo_ref[0:M_WIDTH, :]) + _dot(ha_ref[...], wo_ref[M_WIDTH:2 * M_WIDTH, :])
    x1 = _layer_norm(ALPHA * x_ref[...] + mix, g1_ref[...], b1_ref[...])
    xb = x1.astype(BF16)
    ff = jnp.zeros_like(x1)
    for c in range(D_FF // FF_CHUNK):
        sl = slice(c * FF_CHUNK, (c + 1) * FF_CHUNK)
        act = jax.nn.silu(_dot(xb, wg_ref[:, sl])) * _dot(xb, wu_ref[:, sl])
        ff = ff + _dot(act.astype(BF16), wd_ref[sl, :])
    y_ref[...] = _layer_norm(ALPHA * x1 + ff, g2_ref[...], b2_ref[...])


def _block_out(x, hm, ha, wo, g1, b1, wg, wu, wd, g2, b2, tm):
    m = x.shape[0]
    row = lambda w: pl.BlockSpec((tm, w), lambda i: (i, 0))
    full = lambda a: pl.BlockSpec(a.shape, lambda i: (0, 0), pipeline_mode=pl.Buffered(1))
    return pl.pallas_call(
        _block_out_kernel,
        grid=(m // tm,),
        in_specs=[row(D_MODEL), row(M_WIDTH), row(A_WIDTH), full(wo), full(g1), full(b1),
                  full(wg), full(wu), full(wd), full(g2), full(b2)],
        out_specs=row(D_MODEL),
        out_shape=jax.ShapeDtypeStruct((m, D_MODEL), F32),
        compiler_params=_cparams(("parallel",)),
        name="block_out",
    )(x, hm, ha, wo, g1, b1, wg, wu, wd, g2, b2)


def _split3(x):
    h1 = x.astype(BF16)
    r1 = x - h1.astype(F32)
    h2 = r1.astype(BF16)
    h3 = (r1 - h2.astype(F32)).astype(BF16)
    return h1, h2, h3


def _dot3(x, w):
    h1, h2, h3 = _split3(x)
    return _dot(h1, w) + _dot(h2, w) + _dot(h3, w)


def _mlstm_chunk(q, k, v, lf_row, i_row, i_col, c_aug, m_prev):
    L = q.shape[0]
    t_idx = lax.broadcasted_iota(jnp.int32, (L, L), 0)
    s_idx = lax.broadcasted_iota(jnp.int32, (L, L), 1)
    causal = s_idx <= t_idx
    ones_b = jnp.ones((L, LANES), BF16)
    G = _dot3(jnp.where(causal, lf_row, 0.0), ones_b)
    tri_u = jnp.where(t_idx <= s_idx, 1.0, 0.0).astype(BF16)
    f_row = _dot3(jnp.broadcast_to(lf_row, (SUBLANES, L)), tri_u)[0:1, :]
    f_col = G[:, 0:1]
    dlog = jnp.where(causal, f_col - f_row + i_row, NEG_INF)
    inter = m_prev + f_col
    m_t = jnp.maximum(jnp.max(dlog, axis=1, keepdims=True), inter)
    w = jnp.exp(dlog - m_t)
    g = jnp.exp(inter - m_t)
    s = _dot_nt(q, k) * w
    lane = lax.broadcasted_iota(jnp.int32, (L, LANES), 1)
    v_aug = jnp.concatenate([v, jnp.where(lane == 0, 1.0, 0.0).astype(BF16)], axis=1)
    nd = _dot(s.astype(BF16), v_aug) + g * _dot(q, c_aug.astype(BF16))
    den = nd[:, M_HEAD_DIM:M_HEAD_DIM + 1]
    h = nd[:, 0:M_HEAD_DIM] / jnp.maximum(jnp.abs(den), jnp.exp(-m_t))
    m_new = m_t[L - 1:L, :]
    g_c = g[L - 1:L, :]
    wk = jnp.exp(f_col[L - 1:L, :] - f_col + i_col - m_new)
    kw = (k.astype(F32) * wk).astype(BF16)
    c_new = g_c * c_aug + _dot_tn(kw, v_aug)
    return h, c_new, m_new


def _mlstm_prompt_kernel(*refs, nb):
    mqkv_ref, so_ref, g_ref = refs[0:3]
    gt_refs = refs[3:3 + nb]
    hm_ref, c_ref, m_ref = refs[3 + nb:]
    step = pl.program_id(0)

    @pl.when(step == 0)
    def _():
        c_ref[...] = jnp.zeros_like(c_ref)
        m_ref[...] = jnp.zeros_like(m_ref)

    for b in range(nb):
        for h in range(M_HEADS):
            hs = slice(h * M_HEAD_DIM, (h + 1) * M_HEAD_DIM)
            q = mqkv_ref[b, :, hs]
            k = mqkv_ref[b, :, M_WIDTH + h * M_HEAD_DIM:M_WIDTH + (h + 1) * M_HEAD_DIM]
            v = mqkv_ref[b, :, 2 * M_WIDTH + h * M_HEAD_DIM:2 * M_WIDTH + (h + 1) * M_HEAD_DIM]
            i_row = gt_refs[b][h:h + 1, :]
            lf_row = gt_refs[b][M_HEADS + h:M_HEADS + h + 1, :]
            i_col = g_ref[b, :, L_IG + h:L_IG + h + 1]
            r = b * M_HEADS + h
            m_prev = m_ref[r:r + 1, 0:1]
            hh, c_new, m_new = _mlstm_chunk(q, k, v, lf_row, i_row, i_col, c_ref[b, h], m_prev)
            c_ref[b, h] = c_new
            m_ref[r:r + 1, :] = jnp.broadcast_to(m_new, (1, LANES))
            hm_ref[b, :, hs] = (hh * so_ref[b, :, hs]).astype(BF16)


def _mlstm_prompt(mqkv, so, gates, gatest, nb, seq, chunk):
    nc = seq // chunk
    blk = lambda w: pl.BlockSpec((nb, chunk, w), lambda c: (0, c, 0))
    gt_specs = [pl.BlockSpec((2 * M_HEADS, chunk), lambda c, b=b: (0, b * nc + c)) for b in range(nb)]
    return pl.pallas_call(
        functools.partial(_mlstm_prompt_kernel, nb=nb),
        grid=(nc,),
        in_specs=[blk(3 * M_WIDTH), blk(M_WIDTH), blk(LANES)] + gt_specs,
        out_specs=[blk(M_WIDTH),
                   pl.BlockSpec((nb, M_HEADS, M_HEAD_DIM, 2 * M_HEAD_DIM), lambda c: (0, 0, 0, 0)),
                   pl.BlockSpec((nb * M_HEADS, LANES), lambda c: (0, 0))],
        out_shape=[jax.ShapeDtypeStruct((nb, seq, M_WIDTH), BF16),
                   jax.ShapeDtypeStruct((nb, M_HEADS, M_HEAD_DIM, 2 * M_HEAD_DIM), F32),
                   jax.ShapeDtypeStruct((nb * M_HEADS, LANES), F32)],
        compiler_params=_cparams(("arbitrary",)),
        name="mlstm_prompt",
    )(mqkv, so, gates, *([gatest] * nb))


ROWS_PAD = SUBLANES


def _mlstm_sample_kernel(mqkv_ref, so_ref, g_ref, c0_ref, n0_ref, m0_ref, hm_ref, c_ref, n_ref, m_ref, *, nseq, t_real):
    row = lax.broadcasted_iota(jnp.int32, (ROWS_PAD, 1), 0)
    real = row < t_real

    def per_seq(s, carry):
        r0 = pl.multiple_of(s * ROWS_PAD, ROWS_PAD)
        gt = g_ref[pl.ds(r0, ROWS_PAD), :]
        cum = gt
        for d in range(1, t_real):
            cum = cum + jnp.where(row >= d, pltpu.roll(gt, d, 0), 0.0)
        for h in range(M_HEADS):
            hs = slice(h * M_HEAD_DIM, (h + 1) * M_HEAD_DIM)
            q = mqkv_ref[pl.ds(r0, ROWS_PAD), hs]
            k = mqkv_ref[pl.ds(r0, ROWS_PAD), M_WIDTH + h * M_HEAD_DIM:M_WIDTH + (h + 1) * M_HEAD_DIM]
            v = mqkv_ref[pl.ds(r0, ROWS_PAD), 2 * M_WIDTH + h * M_HEAD_DIM:2 * M_WIDTH + (h + 1) * M_HEAD_DIM]
            qf, kf, vf = q.astype(F32), k.astype(F32), v.astype(F32)
            i_col = gt[:, L_IG + h:L_IG + h + 1]
            f_col = cum[:, L_FG + h:L_FG + h + 1]
            c0 = c0_ref[s, h]
            n0 = n0_ref[s, h:h + 1, :]
            m0 = m0_ref[pl.ds(s, 1), h:h + 1]
            inter = m0 + f_col
            dl = [jnp.where(real & (row >= u), f_col - f_col[u:u + 1, :] + i_col[u:u + 1, :], NEG_INF)
                  for u in range(t_real)]
            m_t = inter
            for u in range(t_real):
                m_t = jnp.maximum(m_t, dl[u])
            g = jnp.exp(inter - m_t)
            qc = _dot(q, c0.astype(BF16))
            num = g * qc
            den = g * jnp.sum(qf * n0, axis=1, keepdims=True)
            for u in range(t_real):
                su = jnp.sum(qf * kf[u:u + 1, :], axis=1, keepdims=True) * jnp.exp(dl[u] - m_t)
                num = num + su * vf[u:u + 1, :]
                den = den + su
            hh = num / jnp.maximum(jnp.abs(den), jnp.exp(-m_t))
            hm_ref[pl.ds(r0, ROWS_PAD), hs] = (hh * so_ref[pl.ds(r0, ROWS_PAD), hs]).astype(BF16)
            last = t_real - 1
            m_new = m_t[last:last + 1, :]
            g_c = g[last:last + 1, :]
            wk = jnp.where(real, jnp.exp(f_col[last:last + 1, :] - f_col + i_col - m_new), 0.0)
            kw = kf * wk
            c_ref[s, h] = g_c * c0 + _dot_tn(kw.astype(BF16), v)
            n_ref[s, h:h + 1, :] = g_c * n0 + jnp.sum(kw, axis=0, keepdims=True)
            m_ref[pl.ds(s, 1), h:h + 1] = m_new
        return carry

    lax.fori_loop(0, nseq, per_seq, 0)


def _mlstm_sample(mqkv, so, gates, c0, n0, m0, nseq_blk, t_real):
    nseq = c0.shape[0]
    rows = nseq_blk * ROWS_PAD
    rblk = lambda w: pl.BlockSpec((rows, w), lambda i: (i, 0))
    cblk = pl.BlockSpec((nseq_blk, M_HEADS, M_HEAD_DIM, M_HEAD_DIM), lambda i: (i, 0, 0, 0))
    nblk = pl.BlockSpec((nseq_blk, M_HEADS, M_HEAD_DIM), lambda i: (i, 0, 0))
    mblk = pl.BlockSpec((nseq_blk, M_HEADS), lambda i: (i, 0))
    return pl.pallas_call(
        functools.partial(_mlstm_sample_kernel, nseq=nseq_blk, t_real=t_real),
        grid=(nseq // nseq_blk,),
        in_specs=[rblk(3 * M_WIDTH), rblk(M_WIDTH), rblk(LANES), cblk, nblk, mblk],
        out_specs=[rblk(M_WIDTH), cblk, nblk, mblk],
        out_shape=[jax.ShapeDtypeStruct((nseq * ROWS_PAD, M_WIDTH), BF16),
                   jax.ShapeDtypeStruct(c0.shape, F32),
                   jax.ShapeDtypeStruct(n0.shape, F32),
                   jax.ShapeDtypeStruct(m0.shape, F32)],
        compiler_params=_cparams(("parallel",)),
        name="mlstm_sample",
    )(mqkv, so, gates, c0, n0, m0)


N_BISECT = 19
MASKED = -1e30
ROW_BLOCK = 128
SELECT_TK = 512
NORM_TK = 1024
MIN_ROW_SUM = 2.0 ** -100


def _block_pass(sc_ref, r0, rb, nch, tk, init, fn, params=()):
    def body(c, acc):
        c0 = pl.multiple_of(c * tk, tk)
        blk = sc_ref[r0:r0 + rb, pl.ds(c0, tk)]
        for j in range(tk // LANES):
            acc = fn(acc, blk[:, j * LANES:(j + 1) * LANES], c0 + j * LANES, params)
        return acc

    return lax.fori_loop(0, nch, body, init)


def _row_pass(sc_ref, nch, tk, init, fn, params=()):
    rows = sc_ref.shape[0]
    rb = min(ROW_BLOCK, rows)
    rsl = lambda t, r0: jax.tree.map(lambda a: a[r0:r0 + rb], t)
    outs = [_block_pass(sc_ref, r0, rb, nch, tk, rsl(init, r0), fn, rsl(params, r0)) for r0 in range(0, rows, rb)]
    return jax.tree.map(lambda *a: jnp.concatenate(a, axis=0), *outs)


def _select_threshold(sc_ref, nch, tk, kr, smax, stats=None):
    rows = sc_ref.shape[0]
    zeros = jnp.zeros((rows, LANES), F32)
    lsum = lambda a: jnp.sum(a, axis=1, keepdims=True)
    full = lambda col: jnp.broadcast_to(col, (rows, LANES))

    def count_ge(thr):
        return lsum(_row_pass(sc_ref, nch, tk, zeros, lambda a, x, _, t: a + jnp.where(x >= t, 1.0, 0.0), full(thr)))

    def range_fn(a, x, _, prm):
        fin = x > NEG_INF
        return (jnp.maximum(a[0], x), jnp.minimum(a[1], jnp.where(fin, x, jnp.inf)), a[2] + jnp.where(fin, 1.0, 0.0))

    if stats is None:
        mx, mn, nf = _row_pass(
            sc_ref, nch, tk, (jnp.full((rows, LANES), NEG_INF, F32), jnp.full((rows, LANES), jnp.inf, F32), zeros),
            range_fn)
        rmax, rmin, n_fin = jnp.max(mx, axis=1, keepdims=True), jnp.min(mn, axis=1, keepdims=True), lsum(nf)
    else:
        rmax, rmin, n_fin = stats

    above = rmax + jnp.maximum(jnp.abs(rmax) * (2.0 ** -20), 2.0 ** -100)

    def bisect(_, st):
        lo, hi, c_lo, c_hi = st
        mid = 0.5 * (lo + hi)
        c = count_ge(mid)
        ge = c >= kr
        return jnp.where(ge, mid, lo), jnp.where(ge, hi, mid), jnp.where(ge, c, c_lo), jnp.where(ge, c_hi, c)

    lo, hi, c_lo, c_hi = lax.fori_loop(0, N_BISECT, bisect, (rmin, above, n_fin, jnp.zeros_like(rmax)))

    rb = min(ROW_BLOCK, rows)
    tau_b, c_gt_b, c_ge_b = [], [], []
    for r0 in range(0, rows, rb):
        blk = lambda a, r0=r0: a[r0:r0 + rb]
        kr_b = blk(kr)
        bfull = lambda col: jnp.broadcast_to(col, (rb, LANES))

        def count_ge_b(thr, r0=r0):
            return lsum(_block_pass(sc_ref, r0, rb, nch, tk, jnp.zeros((rb, LANES), F32),
                                    lambda a, x, _, t: a + jnp.where(x >= t, 1.0, 0.0), bfull(thr)))

        def max_below_b(thr, r0=r0):
            acc = _block_pass(sc_ref, r0, rb, nch, tk, jnp.full((rb, LANES), NEG_INF, F32),
                              lambda a, x, _, t: jnp.maximum(a, jnp.where(x < t, x, NEG_INF)), bfull(thr))
            return jnp.max(acc, axis=1, keepdims=True)

        def peel_cond(st):
            it, _, _, _, _, _, done = st
            return jnp.logical_and(it <= smax, jnp.min(done) < 0.5)

        def peel(st, kr_b=kr_b, count_ge_b=count_ge_b, max_below_b=max_below_b):
            it, hi_b, c_hi_b, tau, c_gt, c_ge, done = st
            t1 = max_below_b(hi_b)
            c1 = count_ge_b(t1)
            fin = jnp.logical_and(c1 >= kr_b, done < 0.5)
            tau = jnp.where(fin, t1, tau)
            c_gt = jnp.where(fin, c_hi_b, c_gt)
            c_ge = jnp.where(fin, c1, c_ge)
            done = jnp.where(fin, 1.0, done)
            live = done < 0.5
            return it + 1, jnp.where(live, t1, hi_b), jnp.where(live, c1, c_hi_b), tau, c_gt, c_ge, done

        done0 = jnp.where(blk(c_lo) - kr_b > 0.5, 0.0, 1.0)
        st = (jnp.int32(0), blk(hi), blk(c_hi), blk(lo), jnp.zeros((rb, 1), F32), kr_b, done0)
        _, _, _, t_b, g_b, e_b, _ = lax.while_loop(peel_cond, peel, st)
        tau_b.append(t_b)
        c_gt_b.append(g_b)
        c_ge_b.append(e_b)
    tau, c_gt, c_ge = (jnp.concatenate(v, axis=0) for v in (tau_b, c_gt_b, c_ge_b))

    need = kr - c_gt
    surplus = (c_ge - c_gt) - need
    k_idx = lax.broadcasted_iota(jnp.int32, (LANES, 2 * LANES), 0)
    j_idx = lax.broadcasted_iota(jnp.int32, (LANES, 2 * LANES), 1)
    tri_ones = jnp.where(jnp.logical_or(k_idx <= j_idx, j_idx >= LANES), 1.0, 0.0).astype(BF16)
    for r0 in range(0, rows, rb):
        @pl.when(jnp.max(surplus[r0:r0 + rb]) > 0.5)
        def _(r0=r0):
            tau_f = jnp.broadcast_to(tau[r0:r0 + rb], (rb, LANES))
            need_f = jnp.broadcast_to(need[r0:r0 + rb], (rb, LANES))

            def sweep(c, seen):
                c0 = pl.multiple_of(c * tk, tk)
                x = sc_ref[r0:r0 + rb, pl.ds(c0, tk)]
                tiles = [x[:, j * LANES:(j + 1) * LANES] for j in range(tk // LANES)]
                cnts = [_dot(jnp.where(t == tau_f, 1.0, 0.0).astype(BF16), tri_ones) for t in tiles]
                out = []
                for t, cnt in zip(tiles, cnts):
                    rank = seen + cnt[:, 0:LANES]
                    out.append(jnp.where(jnp.logical_and(t == tau_f, rank > need_f), NEG_INF, t))
                    seen = seen + cnt[:, LANES:2 * LANES]
                sc_ref[r0:r0 + rb, pl.ds(c0, tk)] = jnp.concatenate(out, axis=1)
                return seen

            lax.fori_loop(0, nch, sweep, jnp.zeros((rb, LANES), F32))

    return tau


def _ones_rows(tk):
    return jnp.where(lax.broadcasted_iota(jnp.int32, (LANES, tk), 0) == 0, 1.0, 0.0).astype(BF16)


def _dsa_prompt_kernel(qi_ref, g_ref, q_ref, kit_ref, kt_ref, vt_ref, o_ref, sc_ref, tau_scr, m_scr, acc_scr, kn_scr,
                       *, tq, tk, topk, smax):
    i = pl.program_id(1)
    nch = ((i + 1) * tq + tk - 1) // tk
    row = lax.broadcasted_iota(jnp.int32, (tq, 1), 0) + i * tq

    m_scr[0] = jnp.full((tq, LANES), NEG_INF, F32)
    m_scr[1] = jnp.full((tq, LANES), jnp.inf, F32)

    def score_chunk(c, carry):
        c0 = pl.multiple_of(c * tk, tk)
        kblk = kit_ref[:, pl.ds(c0, tk)]
        acc = jnp.zeros((tq, tk), F32)
        for h in range(IDX_HEADS):
            x = _dot(qi_ref[:, h * LANES:(h + 1) * LANES], kblk)
            acc = acc + jnp.maximum(x, 0.0) * g_ref[:, L_WI + h:L_WI + h + 1]
        col = lax.broadcasted_iota(jnp.int32, (1, LANES), 1) + c0
        hi_p, lo_p = m_scr[0], m_scr[1]
        for j in range(tk // LANES):
            causal = col + j * LANES <= row
            a = acc[:, j * LANES:(j + 1) * LANES]
            masked = jnp.where(causal, a, NEG_INF)
            sc_ref[:, pl.ds(pl.multiple_of(c0 + j * LANES, LANES), LANES)] = masked
            hi_p = jnp.maximum(hi_p, masked)
            lo_p = jnp.minimum(lo_p, jnp.where(causal, a, jnp.inf))
        m_scr[0], m_scr[1] = hi_p, lo_p
        return carry

    lax.fori_loop(0, nch, score_chunk, 0)

    kr = jnp.minimum(row + 1, topk).astype(F32)
    stats = (jnp.max(m_scr[0], axis=1, keepdims=True), jnp.min(m_scr[1], axis=1, keepdims=True), (row + 1).astype(F32))
    tau = _select_threshold(sc_ref, nch * (tk // SELECT_TK), SELECT_TK, kr, smax, stats)

    nslab = tk // LANES
    tau_scr[...] = jnp.broadcast_to(tau, (tq, LANES))

    def masked_logits(c0, h):
        pr = slice((h // 2) * LANES, (h // 2 + 1) * LANES)
        s = _dot(q_ref[:, h * LANES:(h + 1) * LANES], kt_ref[pr, pl.ds(c0, tk)])
        x = sc_ref[:, pl.ds(c0, tk)]
        thr = tau_scr[...]
        return [jnp.where(x[:, j * LANES:(j + 1) * LANES] >= thr, s[:, j * LANES:(j + 1) * LANES], MASKED)
                for j in range(nslab)]

    def max_sweep():
        m_scr[...] = jnp.full(m_scr.shape, MASKED, F32)

        def max_chunk(c, carry):
            c0 = pl.multiple_of(c * tk, tk)
            for h in range(A_HEADS):
                slabs = masked_logits(c0, h)
                part = slabs[0]
                for j in range(1, nslab):
                    part = jnp.maximum(part, slabs[j])
                m_scr[h] = jnp.maximum(m_scr[h], part)
            return carry

        lax.fori_loop(0, nch, max_chunk, 0)
        for h in range(A_HEADS):
            m_scr[h] = jnp.broadcast_to(jnp.max(m_scr[h], axis=1, keepdims=True), (tq, LANES))

    ones = _ones_rows(tk)

    def attend_sweep():
        acc_scr[...] = jnp.zeros(acc_scr.shape, F32)

        def attend_chunk(c, carry):
            c0 = pl.multiple_of(c * tk, tk)
            for h in range(A_HEADS):
                pr = slice((h // 2) * LANES, (h // 2 + 1) * LANES)
                m_h = m_scr[h]
                pb = jnp.concatenate([jnp.exp2(sl - m_h).astype(BF16) for sl in masked_logits(c0, h)], axis=1)
                vt_aug = jnp.concatenate([vt_ref[pr, pl.ds(c0, tk)], ones], axis=0)
                acc_scr[h] = acc_scr[h] + _dot_nt(pb, vt_aug)
            return carry

        lax.fori_loop(0, nch, attend_chunk, 0)

    @pl.when(i == 0)
    def _():
        for h in range(A_HEADS):
            def norm_chunk(c, best, h=h):
                c0 = pl.multiple_of(c * NORM_TK, NORM_TK)
                kk = kt_ref[h * A_HEAD_DIM:(h + 1) * A_HEAD_DIM, pl.ds(c0, NORM_TK)].astype(F32)
                return jnp.maximum(best, jnp.sum(kk * kk, axis=0, keepdims=True))
            best = lax.fori_loop(0, smax // NORM_TK, norm_chunk, jnp.zeros((1, NORM_TK), F32))
            kn_scr[h:h + 1, :] = jnp.broadcast_to(jnp.max(best, axis=1, keepdims=True), (1, LANES))

    for h in range(A_HEADS):
        qh = q_ref[:, h * LANES:(h + 1) * LANES].astype(F32)
        n2 = jnp.sum(qh * qh, axis=1, keepdims=True) * kn_scr[h:h + 1, 0:1]
        m_scr[h] = jnp.broadcast_to(jnp.where(n2 > 0.0, n2 * lax.rsqrt(n2), 0.0), (tq, LANES))
    attend_sweep()
    l_min = acc_scr[0][:, LANES:LANES + 1]
    for h in range(1, A_HEADS):
        l_min = jnp.minimum(l_min, acc_scr[h][:, LANES:LANES + 1])

    @pl.when(jnp.logical_not(jnp.min(l_min) >= MIN_ROW_SUM))
    def _():
        max_sweep()
        attend_sweep()

    lo_half = lax.broadcasted_iota(jnp.int32, (1, LANES), 1) < A_HEAD_DIM
    for p in range(A_HEADS // 2):
        a_e, a_o = acc_scr[2 * p], acc_scr[2 * p + 1]
        even = a_e[:, 0:LANES] * (1.0 / a_e[:, LANES:LANES + 1])
        odd = a_o[:, 0:LANES] * (1.0 / a_o[:, LANES:LANES + 1])
        o_ref[:, p * LANES:(p + 1) * LANES] = jnp.where(lo_half, even, odd).astype(BF16)


def _dsa_prompt(qipad, gates, qpad, kit2, ktb, vtb, nb, seq, tq, tk):
    nq = seq // tq
    topk = min(TOPK_MAX, seq // 4)
    qrow = lambda w: pl.BlockSpec((tq, w), lambda b, i: (b * nq + i, 0))
    kfull = lambda r: pl.BlockSpec((None, r, seq), lambda b, i: (b, 0, 0), pipeline_mode=pl.Buffered(1))
    return pl.pallas_call(
        functools.partial(_dsa_prompt_kernel, tq=tq, tk=tk, topk=topk, smax=seq),
        grid=(nb, nq),
        in_specs=[qrow(2 * A_WIDTH), qrow(LANES), qrow(2 * A_WIDTH), kfull(2 * IDX_DIM), kfull(A_WIDTH), kfull(A_WIDTH)],
        out_specs=qrow(A_WIDTH),
        out_shape=jax.ShapeDtypeStruct((nb * seq, A_WIDTH), BF16),
        scratch_shapes=[pltpu.VMEM((tq, seq + LANES), F32),
                        pltpu.VMEM((tq, LANES), F32),
                        pltpu.VMEM((A_HEADS, tq, LANES), F32),
                        pltpu.VMEM((A_HEADS, tq, 2 * LANES), F32),
                        pltpu.VMEM((A_HEADS, LANES), F32)],
        compiler_params=_cparams(("parallel", "arbitrary")),
        name="dsa_prompt",
    )(qipad, gates, qpad, kit2, ktb, vtb)


def _dsa_sample_scores_kernel(pt_ref, q_ref, w_ref, *refs, n_pages, t_real):
    page_refs, new_ref, o_ref = refs[:n_pages], refs[n_pages], refs[n_pages + 1]
    q = q_ref[0]
    w = w_ref[0]
    ncol = (n_pages + 1) * PAGE_SIZE
    keys_t = jnp.concatenate([pr[0].astype(BF16) for pr in page_refs] + [new_ref[0]], axis=1)
    r = jnp.maximum(_dot(q, keys_t), 0.0) * w
    sc = jnp.sum(r.reshape(t_real, IDX_HEADS, ncol), axis=1)
    col = lax.broadcasted_iota(jnp.int32, (1, ncol), 1) - n_pages * PAGE_SIZE
    trow = lax.broadcasted_iota(jnp.int32, (t_real, 1), 0)
    o_ref[0] = jnp.where(col <= trow, sc, NEG_INF)


def _dsa_sample_scores(page_table, qi, wi, cache_kidx_t, ki_new_t, t_real):
    nseq, n_pages = page_table.shape
    ncol = (n_pages + 1) * PAGE_SIZE
    per_seq = lambda a: pl.BlockSpec((1,) + a.shape[1:], lambda b, pt: (b, 0, 0))
    page = lambda p: pl.BlockSpec((1, IDX_DIM, PAGE_SIZE), lambda b, pt, p=p: (pt[b * n_pages + p], 0, 0))
    return pl.pallas_call(
        functools.partial(_dsa_sample_scores_kernel, n_pages=n_pages, t_real=t_real),
        grid_spec=pltpu.PrefetchScalarGridSpec(
            num_scalar_prefetch=1,
            grid=(nseq,),
            in_specs=[per_seq(qi), per_seq(wi)] + [page(p) for p in range(n_pages)] + [per_seq(ki_new_t)],
            out_specs=pl.BlockSpec((1, t_real, ncol), lambda b, pt: (b, 0, 0)),
        ),
        out_shape=jax.ShapeDtypeStruct((nseq, t_real, ncol), F32),
        compiler_params=_cparams(("parallel",)),
        name="dsa_sample_scores",
    )(page_table.reshape(-1), qi, wi, *([cache_kidx_t] * n_pages), ki_new_t)


def _dsa_sample_select_kernel(sc_ref, adj_ref, tau_ref, *, tk, topk):
    adj_ref[...] = sc_ref[...]
    rows, ncol = adj_ref.shape
    kr = jnp.full((rows, 1), float(topk), F32)
    tau = _select_threshold(adj_ref, ncol // tk, tk, kr, ncol)
    tau_ref[...] = jnp.broadcast_to(tau, (rows, LANES))


def _dsa_sample_select(sc, topk, rblk):
    rows, ncol = sc.shape
    return pl.pallas_call(
        functools.partial(_dsa_sample_select_kernel, tk=LANES, topk=topk),
        grid=(rows // rblk,),
        in_specs=[pl.BlockSpec((rblk, ncol), lambda i: (i, 0))],
        out_specs=[pl.BlockSpec((rblk, ncol), lambda i: (i, 0)), pl.BlockSpec((rblk, LANES), lambda i: (i, 0))],
        out_shape=[jax.ShapeDtypeStruct((rows, ncol), F32), jax.ShapeDtypeStruct((rows, LANES), F32)],
        compiler_params=_cparams(("parallel",)),
        name="dsa_sample_select",
    )(sc)


def _dsa_sample_attend_kernel(pt_ref, q_ref, sc_ref, tau_ref, *refs, n_pages, t_real):
    kp, vp = refs[:n_pages], refs[n_pages:2 * n_pages]
    knew_ref, vnew_ref, o_ref = refs[2 * n_pages:]
    q = q_ref[0]
    nrow = t_real * A_HEADS
    rows_of = lambda a: jnp.broadcast_to(a[:, None, :], (t_real, A_HEADS, a.shape[-1])).reshape(nrow, a.shape[-1])
    all_pages = lambda pages, new: jnp.concatenate(
        [pr[0].reshape(A_WIDTH, PAGE_SIZE).astype(BF16) for pr in pages] + [new[0]], axis=1)
    s = jnp.where(rows_of(sc_ref[0]) >= rows_of(tau_ref[0][:, 0:1]), _dot(q, all_pages(kp, knew_ref)), MASKED)
    pr = jnp.exp2(s - jnp.max(s, axis=1, keepdims=True))
    l = jnp.sum(pr, axis=1, keepdims=True)
    out = _dot_nt(pr.astype(BF16), all_pages(vp, vnew_ref)) / l
    head_of_row = lax.broadcasted_iota(jnp.int32, (nrow, 1), 0) % A_HEADS
    head_of_lane = lax.broadcasted_iota(jnp.int32, (1, A_WIDTH), 1) // A_HEAD_DIM
    out = jnp.where(head_of_row == head_of_lane, out, 0.0)
    o_ref[0] = jnp.sum(out.reshape(t_real, A_HEADS, A_WIDTH), axis=1).astype(BF16)


def _dsa_sample_attend(page_table, qbd, sc_adj, tau, cache_k_t, cache_v_t, k_new_t, v_new_t, t_real):
    nseq, n_pages = page_table.shape
    ncol = (n_pages + 1) * PAGE_SIZE
    per_seq = lambda a: pl.BlockSpec((1,) + a.shape[1:], lambda b, pt: (b, 0, 0))
    page = lambda p: pl.BlockSpec((1, A_HEADS, A_HEAD_DIM, PAGE_SIZE), lambda b, pt, p=p: (pt[b * n_pages + p], 0, 0, 0))
    pages = lambda: [page(p) for p in range(n_pages)]
    return pl.pallas_call(
        functools.partial(_dsa_sample_attend_kernel, n_pages=n_pages, t_real=t_real),
        grid_spec=pltpu.PrefetchScalarGridSpec(
            num_scalar_prefetch=1,
            grid=(nseq,),
            in_specs=[per_seq(qbd), per_seq(sc_adj), per_seq(tau)] + pages() + pages() + [per_seq(k_new_t), per_seq(v_new_t)],
            out_specs=pl.BlockSpec((1, t_real, A_WIDTH), lambda b, pt: (b, 0, 0)),
        ),
        out_shape=jax.ShapeDtypeStruct((nseq, t_real, A_WIDTH), BF16),
        compiler_params=_cparams(("parallel",)),
        name="dsa_sample_attend",
    )(page_table.reshape(-1), qbd, sc_adj, tau, *([cache_k_t] * n_pages), *([cache_v_t] * n_pages), k_new_t, v_new_t)


PROJ_TM = 512
OUT_TM = 512
MLSTM_CHUNK = 256
DSA_TQ = 512
DSA_TK = 512
SAMPLE_SEQ_BLK = 8
SELECT_ROWS = 128


def _unpad_heads(xpad, n_heads):
    x = xpad.reshape(xpad.shape[:-1] + (n_heads, 2, LANES // 2))
    return x[..., 0, :] + x[..., 1, :]


def kernel(x_prompt, x_sample, cache_k, cache_v, cache_kidx, state_C, state_n, state_m, page_table,
           w_in, b_gate, w_out, ln1_g, ln1_b, w_gate, w_up, w_down, ln2_g, ln2_b):
    bp, sp, _ = x_prompt.shape
    bs, ts, _ = x_sample.shape
    n_pages = page_table.shape[1]
    past = n_pages * PAGE_SIZE
    assert DEPTH == 1 and w_in.shape[0] == 1

    w = _prep_w_in(w_in[0], b_gate[0])
    wo, wg, wu, wd = (a[0].astype(BF16) for a in (w_out, w_gate, w_up, w_down))
    g1, b1, g2, b2 = (v[0].astype(F32)[None, :] for v in (ln1_g, ln1_b, ln2_g, ln2_b))

    mp = bp * sp
    xp = x_prompt.reshape(mp, D_MODEL)
    tabs_p, tabs_pt = _rope_tables(jnp.arange(sp, dtype=jnp.int32))
    (mqkv, so, gates, gatest, qpad, qipad, kt_p, ktb, vt_p, vtb, kit_p, kit2) = _proj(
        xp, w, tabs_p, tabs_pt, PROJ_TM, bp, sp)
    hm_p, caug, m_p = _mlstm_prompt(mqkv.reshape(bp, sp, -1), so.reshape(bp, sp, -1), gates.reshape(bp, sp, -1),
                                    gatest, bp, sp, MLSTM_CHUNK)
    ha_p = _dsa_prompt(qipad, gates, qpad, kit2, ktb, vtb, bp, sp, DSA_TQ, DSA_TK)
    y_p = _block_out(xp, hm_p.reshape(mp, -1), ha_p, wo, g1, b1, wg, wu, wd, g2, b2, OUT_TM)

    ms = bs * ts
    xs = x_sample.reshape(ms, D_MODEL)
    tabs_s, tabs_st = _rope_tables(jnp.tile(past + jnp.arange(ts, dtype=jnp.int32), bs))
    (mqkv_s, so_s, gates_s, _, qpad_s, qipad_s, kt_s, ktb_s, vt_s, vtb_s, kit_s, _) = _proj(
        xs, w, tabs_s, tabs_st, ms, 1, ms)
    pad_rows = lambda a: jnp.pad(a.reshape(bs, ts, -1), ((0, 0), (0, ROWS_PAD - ts), (0, 0))).reshape(bs * ROWS_PAD, -1)
    hm_s, c_s, n_s, m_s = _mlstm_sample(pad_rows(mqkv_s), pad_rows(so_s), pad_rows(gates_s),
                                        state_C[0].astype(F32), state_n[0].astype(F32), state_m[0].astype(F32),
                                        SAMPLE_SEQ_BLK, ts)
    hm_s = hm_s.reshape(bs, ROWS_PAD, -1)[:, :ts].reshape(ms, -1)

    new_page = lambda a: jnp.pad(a.reshape(a.shape[0], bs, ts).transpose(1, 0, 2), ((0, 0), (0, 0), (0, PAGE_SIZE - ts)))
    qi_s = _unpad_heads(qipad_s, IDX_HEADS).reshape(bs, ts * IDX_HEADS, IDX_DIM)
    wi_s = gates_s[:, L_WI:L_WI + IDX_HEADS].reshape(bs, ts * IDX_HEADS, 1)
    sc_s = _dsa_sample_scores(page_table, qi_s, wi_s, cache_kidx[0].transpose(0, 2, 1), new_page(kit_s[0]).astype(BF16), ts)
    ncol = sc_s.shape[-1]
    topk_s = min(TOPK_MAX, (past + ts) // 4)
    sc_adj, tau_s = _dsa_sample_select(sc_s.reshape(ms, ncol), topk_s, SELECT_ROWS)
    q_s = _unpad_heads(qpad_s, A_HEADS).reshape(bs, ts, A_HEADS, A_HEAD_DIM)
    eye = jnp.eye(A_HEADS, dtype=q_s.dtype)
    qbd = (q_s[:, :, :, None, :] * eye[None, None, :, :, None]).reshape(bs, ts * A_HEADS, A_WIDTH)
    ha_s = _dsa_sample_attend(page_table, qbd, sc_adj.reshape(bs, ts, ncol), tau_s.reshape(bs, ts, LANES),
                              cache_k[0].transpose(0, 2, 3, 1), cache_v[0].transpose(0, 2, 3, 1),
                              new_page(ktb_s[0]), new_page(vtb_s[0]), ts)
    y_s = _block_out(xs, hm_s, ha_s.reshape(ms, -1), wo, g1, b1, wg, wu, wd, g2, b2, min(OUT_TM, ms))

    heads = lambda a, b, t: a.reshape(1, b, A_HEADS, A_HEAD_DIM, t).transpose(0, 1, 4, 2, 3)
    heads_s = lambda a: a[0].T.reshape(1, bs, ts, A_HEADS, A_HEAD_DIM)
    return (y_p.reshape(bp, sp, D_MODEL), y_s.reshape(bs, ts, D_MODEL),
            heads(kt_p, bp, sp), heads(vt_p, bp, sp), kit_p.transpose(0, 2, 1)[None],
            caug[None, :, :, :, 0:M_HEAD_DIM], caug[None, :, :, :, M_HEAD_DIM], m_p[:, 0].reshape(1, bp, M_HEADS),
            heads_s(kt_s), heads_s(vt_s), kit_s[0].T.reshape(1, bs, ts, IDX_DIM),
            c_s[None], n_s[None], m_s[None])
```

```python
import functools
import math

import jax
import jax.numpy as jnp
from jax import lax
from jax.experimental import pallas as pl
from jax.experimental.pallas import tpu as pltpu

D_MODEL = 1024
M_HEADS = 4
M_HEAD_DIM = 128
M_WIDTH = 512
A_HEADS = 8
A_HEAD_DIM = 64
A_WIDTH = 512
IDX_HEADS = 8
IDX_DIM = 64
TOPK_MAX = 256
PAGE_SIZE = 128
ROPE_THETA = 500000.0
ROT = A_HEAD_DIM // 4
HALF = ROT // 2
D_FF = 2816
DEPTH = 1
ALPHA = (2 * DEPTH) ** 0.25
LN_EPS = 1e-5
LOG2E = math.log2(math.e)

LANES = 128
SUBLANES = 8
VMEM_LIMIT = 56 * 1024 * 1024

NEG_INF = float("-inf")
BF16 = jnp.bfloat16
F32 = jnp.float32

C_MQ, C_MK, C_MV, C_MO = 0, 512, 1024, 1536
C_AQ, C_IQ, C_SM = 2048, 2560, 3072
N_PERM = 3200
L_WI, L_IG, L_FG = 64, 72, 76
R_AK, R_AV, R_IK, N_ROWS_T = 0, 512, 1024, 1152


def _cparams(sem):
    return pltpu.CompilerParams(dimension_semantics=sem, vmem_limit_bytes=VMEM_LIMIT)


def _dot(a, b):
    return jnp.dot(a, b, preferred_element_type=F32)


def _dot_nt(a, b):
    return lax.dot_general(a, b, (((1,), (1,)), ((), ())), preferred_element_type=F32)


def _dot_tn(a, b):
    return lax.dot_general(a, b, (((0,), (0,)), ((), ())), preferred_element_type=F32)


def _rope128(x, cos, sa, sb):
    return x * cos + pltpu.roll(x, LANES - HALF, 1) * sa + pltpu.roll(x, HALF, 1) * sb


def _rope_rows(z, cos_t, sin_t):
    a, b = z[0:HALF], z[HALF:ROT]
    return jnp.concatenate([a * cos_t - b * sin_t, b * cos_t + a * sin_t, z[ROT:]], axis=0)


def _proj_kernel(x_ref, w_ref, wt_ref, wgt_ref, bias_ref, biast_ref, cos_ref, sa_ref, sb_ref, cost_ref, sint_ref,
                 mqkv_ref, so_ref, gates_ref, gatest_ref, qpad_ref, qipad_ref,
                 kt_ref, ktb_ref, vt_ref, vtb_ref, kit_ref, kit2_ref):
    xb = x_ref[...].astype(BF16)
    cos, sa, sb = cos_ref[...], sa_ref[...], sb_ref[...]
    cos_t, sin_t = cost_ref[...], sint_ref[...]
    lane = lax.broadcasted_iota(jnp.int32, (1, LANES), 1)
    lo_half = lane < A_HEAD_DIM

    zm = _dot(xb, w_ref[:, C_MQ:C_MO])
    mqkv_ref[:, 0:M_WIDTH] = zm[:, 0:M_WIDTH].astype(BF16)
    mqkv_ref[:, M_WIDTH:2 * M_WIDTH] = (zm[:, M_WIDTH:2 * M_WIDTH] * (M_HEAD_DIM ** -0.5)).astype(BF16)
    mqkv_ref[:, 2 * M_WIDTH:3 * M_WIDTH] = zm[:, 2 * M_WIDTH:3 * M_WIDTH].astype(BF16)
    so_ref[...] = jax.nn.sigmoid(_dot(xb, w_ref[:, C_MO:C_AQ]))

    def padded_heads(z, scale, out_ref):
        for p in range(A_WIDTH // LANES):
            r = _rope128(z[:, p * LANES:(p + 1) * LANES], cos, sa, sb) * scale
            out_ref[:, (2 * p) * LANES:(2 * p + 1) * LANES] = jnp.where(lo_half, r, 0.0).astype(BF16)
            out_ref[:, (2 * p + 1) * LANES:(2 * p + 2) * LANES] = jnp.where(lo_half, 0.0, r).astype(BF16)

    padded_heads(_dot(xb, w_ref[:, C_AQ:C_IQ]), (A_HEAD_DIM ** -0.5) * LOG2E, qpad_ref)
    padded_heads(_dot(xb, w_ref[:, C_IQ:C_SM]), IDX_DIM ** -0.5, qipad_ref)

    zs = _dot(xb, w_ref[:, C_SM:N_PERM]) + bias_ref[...]
    is_wi = (lane >= L_WI) & (lane < L_IG)
    is_fg = (lane >= L_FG) & (lane < L_FG + M_HEADS)
    g = jnp.where(is_wi, zs * (IDX_HEADS ** -0.5), zs)
    gates_ref[...] = jnp.where(is_fg, jax.nn.log_sigmoid(zs), g)

    zt = _dot_nt(wgt_ref[...], xb) + biast_ref[...]
    row = lax.broadcasted_iota(jnp.int32, (2 * M_HEADS, 1), 0)
    gatest_ref[...] = jnp.where(row >= M_HEADS, jax.nn.log_sigmoid(zt), zt)

    zkt = _dot_nt(wt_ref[R_AK:R_AV, :], xb)
    for h in range(A_HEADS):
        r = _rope_rows(zkt[h * A_HEAD_DIM:(h + 1) * A_HEAD_DIM], cos_t, sin_t)
        kt_ref[h * A_HEAD_DIM:(h + 1) * A_HEAD_DIM, :] = r
        ktb_ref[h * A_HEAD_DIM:(h + 1) * A_HEAD_DIM, :] = r.astype(BF16)
    zvt = _dot_nt(wt_ref[R_AV:R_IK, :], xb)
    vt_ref[...] = zvt
    vtb_ref[...] = zvt.astype(BF16)
    zit = _dot_nt(wt_ref[R_IK:N_ROWS_T, :], xb)
    r = _rope_rows(zit[0:IDX_DIM], cos_t, sin_t)
    kit_ref[...] = r
    kit2_ref[...] = jnp.concatenate([r, r], axis=0).astype(BF16)


def _rope_tables(pos):
    inv = ROPE_THETA ** (-jnp.arange(HALF, dtype=F32) / HALF)
    posf = pos.astype(F32)
    ang_t = inv[:, None] * posf[None, :]
    dim = jnp.arange(LANES) % A_HEAD_DIM
    ang = posf[:, None] * inv[dim % HALF][None, :]
    c, s = jnp.cos(ang), jnp.sin(ang)
    cos_m = jnp.where(dim < ROT, c, 1.0)
    sa_m = jnp.where(dim < HALF, -s, 0.0)
    sb_m = jnp.where((dim >= HALF) & (dim < ROT), s, 0.0)
    return (cos_m, sa_m, sb_m), (jnp.cos(ang_t), jnp.sin(ang_t))


def _proj(x, w, tabs, tabs_t, tm, nb, seq):
    wp, wt, wgt, bias, biast = w
    m = x.shape[0]
    nblk = seq // tm
    row = lambda wd: pl.BlockSpec((tm, wd), lambda i: (i, 0))
    full = lambda a: pl.BlockSpec(a.shape, lambda i: (0, 0))
    tab = pl.BlockSpec((tm, LANES), lambda i: (i % nblk, 0))
    tab_t = pl.BlockSpec((HALF, tm), lambda i: (0, i % nblk))
    fmaj = lambda r: pl.BlockSpec((None, r, tm), lambda i: (i // nblk, 0, i % nblk))
    outs = [
        ((m, 3 * M_WIDTH), BF16, row(3 * M_WIDTH)),
        ((m, M_WIDTH), F32, row(M_WIDTH)),
        ((m, LANES), F32, row(LANES)),
        ((2 * M_HEADS, m), F32, pl.BlockSpec((2 * M_HEADS, tm), lambda i: (0, i))),
        ((m, 2 * A_WIDTH), BF16, row(2 * A_WIDTH)),
        ((m, 2 * A_WIDTH), BF16, row(2 * A_WIDTH)),
        ((nb, A_WIDTH, seq), F32, fmaj(A_WIDTH)),
        ((nb, A_WIDTH, seq), BF16, fmaj(A_WIDTH)),
        ((nb, A_WIDTH, seq), F32, fmaj(A_WIDTH)),
        ((nb, A_WIDTH, seq), BF16, fmaj(A_WIDTH)),
        ((nb, IDX_DIM, seq), F32, fmaj(IDX_DIM)),
        ((nb, 2 * IDX_DIM, seq), BF16, fmaj(2 * IDX_DIM)),
    ]
    return pl.pallas_call(
        _proj_kernel,
        grid=(m // tm,),
        in_specs=[row(D_MODEL), full(wp), full(wt), full(wgt), full(bias), full(biast), tab, tab, tab, tab_t, tab_t],
        out_specs=[o[2] for o in outs],
        out_shape=[jax.ShapeDtypeStruct(o[0], o[1]) for o in outs],
        compiler_params=_cparams(("parallel",)),
        name="proj",
    )(x, wp, wt, wgt, bias, biast, *tabs, *tabs_t)


def _prep_w_in(w_in, b_gate):
    s = [0, 512, 1024, 1536, 2048, 2052, 2056, 2568, 3080, 3592, 4104, 4168, 4176]
    w16 = w_in.astype(BF16)
    mq, mk, mv, mo, mi, mf, aq, ak, av, iq, ik, iw = [w16[:, s[i]:s[i + 1]] for i in range(12)]
    z64 = jnp.zeros((D_MODEL, L_WI), BF16)
    pad = jnp.zeros((D_MODEL, N_PERM - C_SM - L_FG - M_HEADS), BF16)
    wp = jnp.concatenate([mq, mk, mv, mo, aq, iq, z64, iw, mi, mf, pad], axis=1)
    wt = jnp.concatenate([ak, av, ik, ik], axis=1).T
    wgt = jnp.concatenate([mi, mf], axis=1).T
    bg = b_gate.astype(F32)
    bias = jnp.zeros((1, LANES), F32).at[0, L_IG:L_IG + 2 * M_HEADS].set(bg)
    return wp, wt, wgt, bias, bg[:, None]


FF_CHUNK = D_FF // 2


def _layer_norm(x, g, b):
    mu = jnp.mean(x, axis=-1, keepdims=True)
    xc = x - mu
    var = jnp.mean(xc * xc, axis=-1, keepdims=True)
    return xc * lax.rsqrt(var + LN_EPS) * g + b


def _block_out_kernel(x_ref, hm_ref, ha_ref, wo_ref, g1_ref, b1_ref, wg_ref, wu_ref, wd_ref, g2_ref, b2_ref, y_ref):
    mix = _dot(hm_ref[...], wo_ref[0:M_WIDTH, :]) + _dot(ha_ref[...], wo_ref[M_WIDTH:2 * M_WIDTH, :])
    x1 = _layer_norm(ALPHA * x_ref[...] + mix, g1_ref[...], b1_ref[...])
    xb = x1.astype(BF16)
    ff = jnp.zeros_like(x1)
    for c in range(D_FF // FF_CHUNK):
        sl = slice(c * FF_CHUNK, (c + 1) * FF_CHUNK)
        act = jax.nn.silu(_dot(xb, wg_ref[:, sl])) * _dot(xb, wu_ref[:, sl])
        ff = ff + _dot(act.astype(BF16), wd_ref[sl, :])
    y_ref[...] = _layer_norm(ALPHA * x1 + ff, g2_ref[...], b2_ref[...])


def _block_out(x, hm, ha, wo, g1, b1, wg, wu, wd, g2, b2, tm):
    m = x.shape[0]
    row = lambda w: pl.BlockSpec((tm, w), lambda i: (i, 0))
    full = lambda a: pl.BlockSpec(a.shape, lambda i: (0, 0), pipeline_mode=pl.Buffered(1))
    return pl.pallas_call(
        _block_out_kernel,
        grid=(m // tm,),
        in_specs=[row(D_MODEL), row(M_WIDTH), row(A_WIDTH), full(wo), full(g1), full(b1),
                  full(wg), full(wu), full(wd), full(g2), full(b2)],
        out_specs=row(D_MODEL),
        out_shape=jax.ShapeDtypeStruct((m, D_MODEL), F32),
        compiler_params=_cparams(("parallel",)),
        name="block_out",
    )(x, hm, ha, wo, g1, b1, wg, wu, wd, g2, b2)


def _split3(x):
    h1 = x.astype(BF16)
    r1 = x - h1.astype(F32)
    h2 = r1.astype(BF16)
    h3 = (r1 - h2.astype(F32)).astype(BF16)
    return h1, h2, h3


def _dot3(x, w):
    h1, h2, h3 = _split3(x)
    return _dot(h1, w) + _dot(h2, w) + _dot(h3, w)


def _mlstm_chunk(q, k, v, lf_row, i_row, i_col, c_aug, m_prev):
    L = q.shape[0]
    t_idx = lax.broadcasted_iota(jnp.int32, (L, L), 0)
    s_idx = lax.broadcasted_iota(jnp.int32, (L, L), 1)
    causal = s_idx <= t_idx
    ones_b = jnp.ones((L, LANES), BF16)
    G = _dot3(jnp.where(causal, lf_row, 0.0), ones_b)
    tri_u = jnp.where(t_idx <= s_idx, 1.0, 0.0).astype(BF16)
    f_row = _dot3(jnp.broadcast_to(lf_row, (SUBLANES, L)), tri_u)[0:1, :]
    f_col = G[:, 0:1]
    dlog = jnp.where(causal, f_col - f_row + i_row, NEG_INF)
    inter = m_prev + f_col
    m_t = jnp.maximum(jnp.max(dlog, axis=1, keepdims=True), inter)
    w = jnp.exp(dlog - m_t)
    g = jnp.exp(inter - m_t)
    s = _dot_nt(q, k) * w
    lane = lax.broadcasted_iota(jnp.int32, (L, LANES), 1)
    v_aug = jnp.concatenate([v, jnp.where(lane == 0, 1.0, 0.0).astype(BF16)], axis=1)
    nd = _dot(s.astype(BF16), v_aug) + g * _dot(q, c_aug.astype(BF16))
    den = nd[:, M_HEAD_DIM:M_HEAD_DIM + 1]
    h = nd[:, 0:M_HEAD_DIM] / jnp.maximum(jnp.abs(den), jnp.exp(-m_t))
    m_new = m_t[L - 1:L, :]
    g_c = g[L - 1:L, :]
    wk = jnp.exp(f_col[L - 1:L, :] - f_col + i_col - m_new)
    kw = (k.astype(F32) * wk).astype(BF16)
    c_new = g_c * c_aug + _dot_tn(kw, v_aug)
    return h, c_new, m_new


def _mlstm_prompt_kernel(*refs, nb):
    mqkv_ref, so_ref, g_ref = refs[0:3]
    gt_refs = refs[3:3 + nb]
    hm_ref, c_ref, m_ref = refs[3 + nb:]
    step = pl.program_id(0)

    @pl.when(step == 0)
    def _():
        c_ref[...] = jnp.zeros_like(c_ref)
        m_ref[...] = jnp.zeros_like(m_ref)

    for b in range(nb):
        for h in range(M_HEADS):
            hs = slice(h * M_HEAD_DIM, (h + 1) * M_HEAD_DIM)
            q = mqkv_ref[b, :, hs]
            k = mqkv_ref[b, :, M_WIDTH + h * M_HEAD_DIM:M_WIDTH + (h + 1) * M_HEAD_DIM]
            v = mqkv_ref[b, :, 2 * M_WIDTH + h * M_HEAD_DIM:2 * M_WIDTH + (h + 1) * M_HEAD_DIM]
            i_row = gt_refs[b][h:h + 1, :]
            lf_row = gt_refs[b][M_HEADS + h:M_HEADS + h + 1, :]
            i_col = g_ref[b, :, L_IG + h:L_IG + h + 1]
            r = b * M_HEADS + h
            m_prev = m_ref[r:r + 1, 0:1]
            hh, c_new, m_new = _mlstm_chunk(q, k, v, lf_row, i_row, i_col, c_ref[b, h], m_prev)
            c_ref[b, h] = c_new
            m_ref[r:r + 1, :] = jnp.broadcast_to(m_new, (1, LANES))
            hm_ref[b, :, hs] = (hh * so_ref[b, :, hs]).astype(BF16)


def _mlstm_prompt(mqkv, so, gates, gatest, nb, seq, chunk):
    nc = seq // chunk
    blk = lambda w: pl.BlockSpec((nb, chunk, w), lambda c: (0, c, 0))
    gt_specs = [pl.BlockSpec((2 * M_HEADS, chunk), lambda c, b=b: (0, b * nc + c)) for b in range(nb)]
    return pl.pallas_call(
        functools.partial(_mlstm_prompt_kernel, nb=nb),
        grid=(nc,),
        in_specs=[blk(3 * M_WIDTH), blk(M_WIDTH), blk(LANES)] + gt_specs,
        out_specs=[blk(M_WIDTH),
                   pl.BlockSpec((nb, M_HEADS, M_HEAD_DIM, 2 * M_HEAD_DIM), lambda c: (0, 0, 0, 0)),
                   pl.BlockSpec((nb * M_HEADS, LANES), lambda c: (0, 0))],
        out_shape=[jax.ShapeDtypeStruct((nb, seq, M_WIDTH), BF16),
                   jax.ShapeDtypeStruct((nb, M_HEADS, M_HEAD_DIM, 2 * M_HEAD_DIM), F32),
                   jax.ShapeDtypeStruct((nb * M_HEADS, LANES), F32)],
        compiler_params=_cparams(("arbitrary",)),
        name="mlstm_prompt",
    )(mqkv, so, gates, *([gatest] * nb))


ROWS_PAD = SUBLANES


def _mlstm_sample_kernel(mqkv_ref, so_ref, g_ref, c0_ref, n0_ref, m0_ref, hm_ref, c_ref, n_ref, m_ref, *, nseq, t_real):
    row = lax.broadcasted_iota(jnp.int32, (ROWS_PAD, 1), 0)
    real = row < t_real

    def per_seq(s, carry):
        r0 = pl.multiple_of(s * ROWS_PAD, ROWS_PAD)
        gt = g_ref[pl.ds(r0, ROWS_PAD), :]
        cum = gt
        for d in range(1, t_real):
            cum = cum + jnp.where(row >= d, pltpu.roll(gt, d, 0), 0.0)
        for h in range(M_HEADS):
            hs = slice(h * M_HEAD_DIM, (h + 1) * M_HEAD_DIM)
            q = mqkv_ref[pl.ds(r0, ROWS_PAD), hs]
            k = mqkv_ref[pl.ds(r0, ROWS_PAD), M_WIDTH + h * M_HEAD_DIM:M_WIDTH + (h + 1) * M_HEAD_DIM]
            v = mqkv_ref[pl.ds(r0, ROWS_PAD), 2 * M_WIDTH + h * M_HEAD_DIM:2 * M_WIDTH + (h + 1) * M_HEAD_DIM]
            qf, kf, vf = q.astype(F32), k.astype(F32), v.astype(F32)
            i_col = gt[:, L_IG + h:L_IG + h + 1]
            f_col = cum[:, L_FG + h:L_FG + h + 1]
            c0 = c0_ref[s, h]
            n0 = n0_ref[s, h:h + 1, :]
            m0 = m0_ref[pl.ds(s, 1), h:h + 1]
            inter = m0 + f_col
            dl = [jnp.where(real & (row >= u), f_col - f_col[u:u + 1, :] + i_col[u:u + 1, :], NEG_INF)
                  for u in range(t_real)]
            m_t = inter
            for u in range(t_real):
                m_t = jnp.maximum(m_t, dl[u])
            g = jnp.exp(inter - m_t)
            qc = _dot(q, c0.astype(BF16))
            num = g * qc
            den = g * jnp.sum(qf * n0, axis=1, keepdims=True)
            for u in range(t_real):
                su = jnp.sum(qf * kf[u:u + 1, :], axis=1, keepdims=True) * jnp.exp(dl[u] - m_t)
                num = num + su * vf[u:u + 1, :]
                den = den + su
            hh = num / jnp.maximum(jnp.abs(den), jnp.exp(-m_t))
            hm_ref[pl.ds(r0, ROWS_PAD), hs] = (hh * so_ref[pl.ds(r0, ROWS_PAD), hs]).astype(BF16)
            last = t_real - 1
            m_new = m_t[last:last + 1, :]
            g_c = g[last:last + 1, :]
            wk = jnp.where(real, jnp.exp(f_col[last:last + 1, :] - f_col + i_col - m_new), 0.0)
            kw = kf * wk
            c_ref[s, h] = g_c * c0 + _dot_tn(kw.astype(BF16), v)
            n_ref[s, h:h + 1, :] = g_c * n0 + jnp.sum(kw, axis=0, keepdims=True)
            m_ref[pl.ds(s, 1), h:h + 1] = m_new
        return carry

    lax.fori_loop(0, nseq, per_seq, 0)


def _mlstm_sample(mqkv, so, gates, c0, n0, m0, nseq_blk, t_real):
    nseq = c0.shape[0]
    rows = nseq_blk * ROWS_PAD
    rblk = lambda w: pl.BlockSpec((rows, w), lambda i: (i, 0))
    cblk = pl.BlockSpec((nseq_blk, M_HEADS, M_HEAD_DIM, M_HEAD_DIM), lambda i: (i, 0, 0, 0))
    nblk = pl.BlockSpec((nseq_blk, M_HEADS, M_HEAD_DIM), lambda i: (i, 0, 0))
    mblk = pl.BlockSpec((nseq_blk, M_HEADS), lambda i: (i, 0))
    return pl.pallas_call(
        functools.partial(_mlstm_sample_kernel, nseq=nseq_blk, t_real=t_real),
        grid=(nseq // nseq_blk,),
        in_specs=[rblk(3 * M_WIDTH), rblk(M_WIDTH), rblk(LANES), cblk, nblk, mblk],
        out_specs=[rblk(M_WIDTH), cblk, nblk, mblk],
        out_shape=[jax.ShapeDtypeStruct((nseq * ROWS_PAD, M_WIDTH), BF16),
                   jax.ShapeDtypeStruct(c0.shape, F32),
                   jax.ShapeDtypeStruct(n0.shape, F32),
                   jax.ShapeDtypeStruct(m0.shape, F32)],
        compiler_params=_cparams(("parallel",)),
        name="mlstm_sample",
    )(mqkv, so, gates, c0, n0, m0)


N_BISECT = 15
MASKED = -1e30
ROW_BLOCK = 128
SELECT_TK = 512
NORM_TK = 1024
SCORE_ROWS = 64
MIN_ROW_SUM = 2.0 ** -100


def _block_pass(sc_ref, r0, rb, nch, tk, init, fn, params=()):
    def body(c, acc):
        c0 = pl.multiple_of(c * tk, tk)
        blk = sc_ref[r0:r0 + rb, pl.ds(c0, tk)]
        for j in range(tk // LANES):
            acc = fn(acc, blk[:, j * LANES:(j + 1) * LANES], c0 + j * LANES, params)
        return acc

    return lax.fori_loop(0, nch, body, init)


def _row_pass(sc_ref, nch, tk, init, fn, params=()):
    rows = sc_ref.shape[0]
    rb = min(ROW_BLOCK, rows)
    rsl = lambda t, r0: jax.tree.map(lambda a: a[r0:r0 + rb], t)
    outs = [_block_pass(sc_ref, r0, rb, nch, tk, rsl(init, r0), fn, rsl(params, r0)) for r0 in range(0, rows, rb)]
    return jax.tree.map(lambda *a: jnp.concatenate(a, axis=0), *outs)


def _select_threshold(sc_ref, nch, tk, kr, smax, stats=None):
    rows = sc_ref.shape[0]
    zeros = jnp.zeros((rows, LANES), F32)
    lsum = lambda a: jnp.sum(a, axis=1, keepdims=True)
    full = lambda col: jnp.broadcast_to(col, (rows, LANES))

    def count_ge(thr):
        return lsum(_row_pass(sc_ref, nch, tk, zeros, lambda a, x, _, t: a + jnp.where(x >= t, 1.0, 0.0), full(thr)))

    def range_fn(a, x, _, prm):
        fin = x > NEG_INF
        return (jnp.maximum(a[0], x), jnp.minimum(a[1], jnp.where(fin, x, jnp.inf)), a[2] + jnp.where(fin, 1.0, 0.0))

    if stats is None:
        mx, mn, nf = _row_pass(
            sc_ref, nch, tk, (jnp.full((rows, LANES), NEG_INF, F32), jnp.full((rows, LANES), jnp.inf, F32), zeros),
            range_fn)
        rmax, rmin, n_fin = jnp.max(mx, axis=1, keepdims=True), jnp.min(mn, axis=1, keepdims=True), lsum(nf)
    else:
        rmax, rmin, n_fin = stats

    above = rmax + jnp.maximum(jnp.abs(rmax) * (2.0 ** -20), 2.0 ** -100)

    def bisect(_, st):
        lo, hi, c_lo, c_hi = st
        mid = 0.5 * (lo + hi)
        c = count_ge(mid)
        ge = c >= kr
        return jnp.where(ge, mid, lo), jnp.where(ge, hi, mid), jnp.where(ge, c, c_lo), jnp.where(ge, c_hi, c)

    lo, hi, c_lo, c_hi = lax.fori_loop(0, N_BISECT, bisect, (rmin, above, n_fin, jnp.zeros_like(rmax)))

    rb = min(ROW_BLOCK, rows)
    tau_b, c_gt_b, c_ge_b = [], [], []
    for r0 in range(0, rows, rb):
        blk = lambda a, r0=r0: a[r0:r0 + rb]
        kr_b = blk(kr)
        bfull = lambda col: jnp.broadcast_to(col, (rb, LANES))

        def count_ge_b(thr, r0=r0):
            return lsum(_block_pass(sc_ref, r0, rb, nch, tk, jnp.zeros((rb, LANES), F32),
                                    lambda a, x, _, t: a + jnp.where(x >= t, 1.0, 0.0), bfull(thr)))

        def max_below_b(thr, r0=r0):
            acc = _block_pass(sc_ref, r0, rb, nch, tk, jnp.full((rb, LANES), NEG_INF, F32),
                              lambda a, x, _, t: jnp.maximum(a, jnp.where(x < t, x, NEG_INF)), bfull(thr))
            return jnp.max(acc, axis=1, keepdims=True)

        def peel_cond(st):
            it, _, _, _, _, _, done = st
            return jnp.logical_and(it <= smax, jnp.min(done) < 0.5)

        def peel(st, kr_b=kr_b, count_ge_b=count_ge_b, max_below_b=max_below_b):
            it, hi_b, c_hi_b, tau, c_gt, c_ge, done = st
            t1 = max_below_b(hi_b)
            c1 = count_ge_b(t1)
            fin = jnp.logical_and(c1 >= kr_b, done < 0.5)
            tau = jnp.where(fin, t1, tau)
            c_gt = jnp.where(fin, c_hi_b, c_gt)
            c_ge = jnp.where(fin, c1, c_ge)
            done = jnp.where(fin, 1.0, done)
            live = done < 0.5
            return it + 1, jnp.where(live, t1, hi_b), jnp.where(live, c1, c_hi_b), tau, c_gt, c_ge, done

        done0 = jnp.where(blk(c_lo) - kr_b > 0.5, 0.0, 1.0)
        st = (jnp.int32(0), blk(hi), blk(c_hi), blk(lo), jnp.zeros((rb, 1), F32), kr_b, done0)
        _, _, _, t_b, g_b, e_b, _ = lax.while_loop(peel_cond, peel, st)
        tau_b.append(t_b)
        c_gt_b.append(g_b)
        c_ge_b.append(e_b)
    tau, c_gt, c_ge = (jnp.concatenate(v, axis=0) for v in (tau_b, c_gt_b, c_ge_b))

    need = kr - c_gt
    surplus = (c_ge - c_gt) - need
    k_idx = lax.broadcasted_iota(jnp.int32, (LANES, 2 * LANES), 0)
    j_idx = lax.broadcasted_iota(jnp.int32, (LANES, 2 * LANES), 1)
    tri_ones = jnp.where(jnp.logical_or(k_idx <= j_idx, j_idx >= LANES), 1.0, 0.0).astype(BF16)
    for r0 in range(0, rows, rb):
        @pl.when(jnp.max(surplus[r0:r0 + rb]) > 0.5)
        def _(r0=r0):
            tau_f = jnp.broadcast_to(tau[r0:r0 + rb], (rb, LANES))
            need_f = jnp.broadcast_to(need[r0:r0 + rb], (rb, LANES))

            def sweep(c, seen):
                c0 = pl.multiple_of(c * tk, tk)
                x = sc_ref[r0:r0 + rb, pl.ds(c0, tk)]
                tiles = [x[:, j * LANES:(j + 1) * LANES] for j in range(tk // LANES)]
                cnts = [_dot(jnp.where(t == tau_f, 1.0, 0.0).astype(BF16), tri_ones) for t in tiles]
                out = []
                for t, cnt in zip(tiles, cnts):
                    rank = seen + cnt[:, 0:LANES]
                    out.append(jnp.where(jnp.logical_and(t == tau_f, rank > need_f), NEG_INF, t))
                    seen = seen + cnt[:, LANES:2 * LANES]
                sc_ref[r0:r0 + rb, pl.ds(c0, tk)] = jnp.concatenate(out, axis=1)
                return seen

            lax.fori_loop(0, nch, sweep, jnp.zeros((rb, LANES), F32))

    return tau


def _ones_rows(tk):
    return jnp.where(lax.broadcasted_iota(jnp.int32, (LANES, tk), 0) == 0, 1.0, 0.0).astype(BF16)


def _dsa_prompt_kernel(qi_ref, g_ref, q_ref, kit_ref, kt_ref, vt_ref, o_ref, sc_ref, tau_scr, m_scr, acc_scr, kn_scr,
                       *, tq, tk, topk, smax):
    i = pl.program_id(1)
    nch = ((i + 1) * tq + tk - 1) // tk
    row = lax.broadcasted_iota(jnp.int32, (tq, 1), 0) + i * tq

    m_scr[0] = jnp.full((tq, LANES), NEG_INF, F32)
    m_scr[1] = jnp.full((tq, LANES), jnp.inf, F32)

    def score_chunk(c, carry):
        c0 = pl.multiple_of(c * tk, tk)
        kblk = kit_ref[:, pl.ds(c0, tk)]
        col = lax.broadcasted_iota(jnp.int32, (1, LANES), 1) + c0
        for r0 in range(0, tq, SCORE_ROWS):
            rs = slice(r0, r0 + SCORE_ROWS)
            acc = jnp.zeros((SCORE_ROWS, tk), F32)
            for h in range(IDX_HEADS):
                x = _dot(qi_ref[rs, h * LANES:(h + 1) * LANES], kblk)
                acc = acc + jnp.maximum(x, 0.0) * g_ref[rs, L_WI + h:L_WI + h + 1]
            hi_p, lo_p = m_scr[0, rs], m_scr[1, rs]
            for j in range(tk // LANES):
                causal = col + j * LANES <= row[rs]
                a = acc[:, j * LANES:(j + 1) * LANES]
                masked = jnp.where(causal, a, NEG_INF)
                sc_ref[rs, pl.ds(pl.multiple_of(c0 + j * LANES, LANES), LANES)] = masked
                hi_p = jnp.maximum(hi_p, masked)
                lo_p = jnp.minimum(lo_p, jnp.where(causal, a, jnp.inf))
            m_scr[0, rs], m_scr[1, rs] = hi_p, lo_p
        return carry

    lax.fori_loop(0, nch, score_chunk, 0)

    kr = jnp.minimum(row + 1, topk).astype(F32)
    stats = (jnp.max(m_scr[0], axis=1, keepdims=True), jnp.min(m_scr[1], axis=1, keepdims=True), (row + 1).astype(F32))
    tau = _select_threshold(sc_ref, nch * (tk // SELECT_TK), SELECT_TK, kr, smax, stats)

    nslab = tk // LANES
    tau_scr[...] = jnp.broadcast_to(tau, (tq, LANES))

    def masked_logits(c0, h):
        pr = slice((h // 2) * LANES, (h // 2 + 1) * LANES)
        s = _dot(q_ref[:, h * LANES:(h + 1) * LANES], kt_ref[pr, pl.ds(c0, tk)])
        x = sc_ref[:, pl.ds(c0, tk)]
        thr = tau_scr[...]
        return [jnp.where(x[:, j * LANES:(j + 1) * LANES] >= thr, s[:, j * LANES:(j + 1) * LANES], MASKED)
                for j in range(nslab)]

    def max_sweep():
        m_scr[...] = jnp.full(m_scr.shape, MASKED, F32)

        def max_chunk(c, carry):
            c0 = pl.multiple_of(c * tk, tk)
            for h in range(A_HEADS):
                slabs = masked_logits(c0, h)
                part = slabs[0]
                for j in range(1, nslab):
                    part = jnp.maximum(part, slabs[j])
                m_scr[h] = jnp.maximum(m_scr[h], part)
            return carry

        lax.fori_loop(0, nch, max_chunk, 0)
        for h in range(A_HEADS):
            m_scr[h] = jnp.broadcast_to(jnp.max(m_scr[h], axis=1, keepdims=True), (tq, LANES))

    ones = _ones_rows(tk)

    def attend_sweep():
        acc_scr[...] = jnp.zeros(acc_scr.shape, F32)

        def attend_chunk(c, carry):
            c0 = pl.multiple_of(c * tk, tk)
            for h in range(A_HEADS):
                pr = slice((h // 2) * LANES, (h // 2 + 1) * LANES)
                m_h = m_scr[h]
                pb = jnp.concatenate([jnp.exp2(sl - m_h).astype(BF16) for sl in masked_logits(c0, h)], axis=1)
                vt_aug = jnp.concatenate([vt_ref[pr, pl.ds(c0, tk)], ones], axis=0)
                acc_scr[h] = acc_scr[h] + _dot_nt(pb, vt_aug)
            return carry

        lax.fori_loop(0, nch, attend_chunk, 0)

    @pl.when(i == 0)
    def _():
        for h in range(A_HEADS):
            def norm_chunk(c, best, h=h):
                c0 = pl.multiple_of(c * NORM_TK, NORM_TK)
                kk = kt_ref[h * A_HEAD_DIM:(h + 1) * A_HEAD_DIM, pl.ds(c0, NORM_TK)].astype(F32)
                return jnp.maximum(best, jnp.sum(kk * kk, axis=0, keepdims=True))
            best = lax.fori_loop(0, smax // NORM_TK, norm_chunk, jnp.zeros((1, NORM_TK), F32))
            kn_scr[h:h + 1, :] = jnp.broadcast_to(jnp.max(best, axis=1, keepdims=True), (1, LANES))

    for h in range(A_HEADS):
        qh = q_ref[:, h * LANES:(h + 1) * LANES].astype(F32)
        n2 = jnp.sum(qh * qh, axis=1, keepdims=True) * kn_scr[h:h + 1, 0:1]
        m_scr[h] = jnp.broadcast_to(jnp.where(n2 > 0.0, n2 * lax.rsqrt(n2), 0.0), (tq, LANES))
    attend_sweep()
    l_min = acc_scr[0][:, LANES:LANES + 1]
    for h in range(1, A_HEADS):
        l_min = jnp.minimum(l_min, acc_scr[h][:, LANES:LANES + 1])

    @pl.when(jnp.logical_not(jnp.min(l_min) >= MIN_ROW_SUM))
    def _():
        max_sweep()
        attend_sweep()

    lo_half = lax.broadcasted_iota(jnp.int32, (1, LANES), 1) < A_HEAD_DIM
    for p in range(A_HEADS // 2):
        a_e, a_o = acc_scr[2 * p], acc_scr[2 * p + 1]
        even = a_e[:, 0:LANES] * (1.0 / a_e[:, LANES:LANES + 1])
        odd = a_o[:, 0:LANES] * (1.0 / a_o[:, LANES:LANES + 1])
        o_ref[:, p * LANES:(p + 1) * LANES] = jnp.where(lo_half, even, odd).astype(BF16)


def _dsa_prompt(qipad, gates, qpad, kit2, ktb, vtb, nb, seq, tq, tk):
    nq = seq // tq
    topk = min(TOPK_MAX, seq // 4)
    qrow = lambda w: pl.BlockSpec((tq, w), lambda b, i: (b * nq + i, 0))
    kfull = lambda r: pl.BlockSpec((None, r, seq), lambda b, i: (b, 0, 0), pipeline_mode=pl.Buffered(1))
    return pl.pallas_call(
        functools.partial(_dsa_prompt_kernel, tq=tq, tk=tk, topk=topk, smax=seq),
        grid=(nb, nq),
        in_specs=[qrow(2 * A_WIDTH), qrow(LANES), qrow(2 * A_WIDTH), kfull(2 * IDX_DIM), kfull(A_WIDTH), kfull(A_WIDTH)],
        out_specs=qrow(A_WIDTH),
        out_shape=jax.ShapeDtypeStruct((nb * seq, A_WIDTH), BF16),
        scratch_shapes=[pltpu.VMEM((tq, seq + LANES), F32),
                        pltpu.VMEM((tq, LANES), F32),
                        pltpu.VMEM((A_HEADS, tq, LANES), F32),
                        pltpu.VMEM((A_HEADS, tq, 2 * LANES), F32),
                        pltpu.VMEM((A_HEADS, LANES), F32)],
        compiler_params=_cparams(("parallel", "arbitrary")),
        name="dsa_prompt",
    )(qipad, gates, qpad, kit2, ktb, vtb)


def _dsa_sample_scores_kernel(pt_ref, q_ref, w_ref, *refs, n_pages, t_real):
    page_refs, new_ref, o_ref = refs[:n_pages], refs[n_pages], refs[n_pages + 1]
    q = q_ref[0]
    w = w_ref[0]
    ncol = (n_pages + 1) * PAGE_SIZE
    keys_t = jnp.concatenate([pr[0].astype(BF16) for pr in page_refs] + [new_ref[0]], axis=1)
    r = jnp.maximum(_dot(q, keys_t), 0.0) * w
    sc = jnp.sum(r.reshape(t_real, IDX_HEADS, ncol), axis=1)
    col = lax.broadcasted_iota(jnp.int32, (1, ncol), 1) - n_pages * PAGE_SIZE
    trow = lax.broadcasted_iota(jnp.int32, (t_real, 1), 0)
    o_ref[0] = jnp.where(col <= trow, sc, NEG_INF)


def _dsa_sample_scores(page_table, qi, wi, cache_kidx_t, ki_new_t, t_real):
    nseq, n_pages = page_table.shape
    ncol = (n_pages + 1) * PAGE_SIZE
    per_seq = lambda a: pl.BlockSpec((1,) + a.shape[1:], lambda b, pt: (b, 0, 0))
    page = lambda p: pl.BlockSpec((1, IDX_DIM, PAGE_SIZE), lambda b, pt, p=p: (pt[b * n_pages + p], 0, 0))
    return pl.pallas_call(
        functools.partial(_dsa_sample_scores_kernel, n_pages=n_pages, t_real=t_real),
        grid_spec=pltpu.PrefetchScalarGridSpec(
            num_scalar_prefetch=1,
            grid=(nseq,),
            in_specs=[per_seq(qi), per_seq(wi)] + [page(p) for p in range(n_pages)] + [per_seq(ki_new_t)],
            out_specs=pl.BlockSpec((1, t_real, ncol), lambda b, pt: (b, 0, 0)),
        ),
        out_shape=jax.ShapeDtypeStruct((nseq, t_real, ncol), F32),
        compiler_params=_cparams(("parallel",)),
        name="dsa_sample_scores",
    )(page_table.reshape(-1), qi, wi, *([cache_kidx_t] * n_pages), ki_new_t)


def _dsa_sample_select_kernel(sc_ref, adj_ref, tau_ref, *, tk, topk):
    adj_ref[...] = sc_ref[...]
    rows, ncol = adj_ref.shape
    kr = jnp.full((rows, 1), float(topk), F32)
    tau = _select_threshold(adj_ref, ncol // tk, tk, kr, ncol)
    tau_ref[...] = jnp.broadcast_to(tau, (rows, LANES))


def _dsa_sample_select(sc, topk, rblk):
    rows, ncol = sc.shape
    return pl.pallas_call(
        functools.partial(_dsa_sample_select_kernel, tk=LANES, topk=topk),
        grid=(rows // rblk,),
        in_specs=[pl.BlockSpec((rblk, ncol), lambda i: (i, 0))],
        out_specs=[pl.BlockSpec((rblk, ncol), lambda i: (i, 0)), pl.BlockSpec((rblk, LANES), lambda i: (i, 0))],
        out_shape=[jax.ShapeDtypeStruct((rows, ncol), F32), jax.ShapeDtypeStruct((rows, LANES), F32)],
        compiler_params=_cparams(("parallel",)),
        name="dsa_sample_select",
    )(sc)


def _dsa_sample_attend_kernel(pt_ref, q_ref, sc_ref, tau_ref, *refs, n_pages, t_real):
    kp, vp = refs[:n_pages], refs[n_pages:2 * n_pages]
    knew_ref, vnew_ref, o_ref = refs[2 * n_pages:]
    q = q_ref[0]
    nrow = t_real * A_HEADS
    rows_of = lambda a: jnp.broadcast_to(a[:, None, :], (t_real, A_HEADS, a.shape[-1])).reshape(nrow, a.shape[-1])
    all_pages = lambda pages, new: jnp.concatenate(
        [pr[0].reshape(A_WIDTH, PAGE_SIZE).astype(BF16) for pr in pages] + [new[0]], axis=1)
    s = jnp.where(rows_of(sc_ref[0]) >= rows_of(tau_ref[0][:, 0:1]), _dot(q, all_pages(kp, knew_ref)), MASKED)
    pr = jnp.exp2(s - jnp.max(s, axis=1, keepdims=True))
    l = jnp.sum(pr, axis=1, keepdims=True)
    out = _dot_nt(pr.astype(BF16), all_pages(vp, vnew_ref)) / l
    head_of_row = lax.broadcasted_iota(jnp.int32, (nrow, 1), 0) % A_HEADS
    head_of_lane = lax.broadcasted_iota(jnp.int32, (1, A_WIDTH), 1) // A_HEAD_DIM
    out = jnp.where(head_of_row == head_of_lane, out, 0.0)
    o_ref[0] = jnp.sum(out.reshape(t_real, A_HEADS, A_WIDTH), axis=1).astype(BF16)


def _dsa_sample_attend(page_table, qbd, sc_adj, tau, cache_k_t, cache_v_t, k_new_t, v_new_t, t_real):
    nseq, n_pages = page_table.shape
    ncol = (n_pages + 1) * PAGE_SIZE
    per_seq = lambda a: pl.BlockSpec((1,) + a.shape[1:], lambda b, pt: (b, 0, 0))
    page = lambda p: pl.BlockSpec((1, A_HEADS, A_HEAD_DIM, PAGE_SIZE), lambda b, pt, p=p: (pt[b * n_pages + p], 0, 0, 0))
    pages = lambda: [page(p) for p in range(n_pages)]
    return pl.pallas_call(
        functools.partial(_dsa_sample_attend_kernel, n_pages=n_pages, t_real=t_real),
        grid_spec=pltpu.PrefetchScalarGridSpec(
            num_scalar_prefetch=1,
            grid=(nseq,),
            in_specs=[per_seq(qbd), per_seq(sc_adj), per_seq(tau)] + pages() + pages() + [per_seq(k_new_t), per_seq(v_new_t)],
            out_specs=pl.BlockSpec((1, t_real, A_WIDTH), lambda b, pt: (b, 0, 0)),
        ),
        out_shape=jax.ShapeDtypeStruct((nseq, t_real, A_WIDTH), BF16),
        compiler_params=_cparams(("parallel",)),
        name="dsa_sample_attend",
    )(page_table.reshape(-1), qbd, sc_adj, tau, *([cache_k_t] * n_pages), *([cache_v_t] * n_pages), k_new_t, v_new_t)


PROJ_TM = 512
OUT_TM = 512
MLSTM_CHUNK = 256
DSA_TQ = 512
DSA_TK = 512
SAMPLE_SEQ_BLK = 8
SELECT_ROWS = 128


def _unpad_heads(xpad, n_heads):
    x = xpad.reshape(xpad.shape[:-1] + (n_heads, 2, LANES // 2))
    return x[..., 0, :] + x[..., 1, :]


def kernel(x_prompt, x_sample, cache_k, cache_v, cache_kidx, state_C, state_n, state_m, page_table,
           w_in, b_gate, w_out, ln1_g, ln1_b, w_gate, w_up, w_down, ln2_g, ln2_b):
    bp, sp, _ = x_prompt.shape
    bs, ts, _ = x_sample.shape
    n_pages = page_table.shape[1]
    past = n_pages * PAGE_SIZE
    assert DEPTH == 1 and w_in.shape[0] == 1

    w = _prep_w_in(w_in[0], b_gate[0])
    wo, wg, wu, wd = (a[0].astype(BF16) for a in (w_out, w_gate, w_up, w_down))
    g1, b1, g2, b2 = (v[0].astype(F32)[None, :] for v in (ln1_g, ln1_b, ln2_g, ln2_b))

    mp = bp * sp
    xp = x_prompt.reshape(mp, D_MODEL)
    tabs_p, tabs_pt = _rope_tables(jnp.arange(sp, dtype=jnp.int32))
    (mqkv, so, gates, gatest, qpad, qipad, kt_p, ktb, vt_p, vtb, kit_p, kit2) = _proj(
        xp, w, tabs_p, tabs_pt, PROJ_TM, bp, sp)
    hm_p, caug, m_p = _mlstm_prompt(mqkv.reshape(bp, sp, -1), so.reshape(bp, sp, -1), gates.reshape(bp, sp, -1),
                                    gatest, bp, sp, MLSTM_CHUNK)
    ha_p = _dsa_prompt(qipad, gates, qpad, kit2, ktb, vtb, bp, sp, DSA_TQ, DSA_TK)
    y_p = _block_out(xp, hm_p.reshape(mp, -1), ha_p, wo, g1, b1, wg, wu, wd, g2, b2, OUT_TM)

    ms = bs * ts
    xs = x_sample.reshape(ms, D_MODEL)
    tabs_s, tabs_st = _rope_tables(jnp.tile(past + jnp.arange(ts, dtype=jnp.int32), bs))
    (mqkv_s, so_s, gates_s, _, qpad_s, qipad_s, kt_s, ktb_s, vt_s, vtb_s, kit_s, _) = _proj(
        xs, w, tabs_s, tabs_st, ms, 1, ms)
    pad_rows = lambda a: jnp.pad(a.reshape(bs, ts, -1), ((0, 0), (0, ROWS_PAD - ts), (0, 0))).reshape(bs * ROWS_PAD, -1)
    hm_s, c_s, n_s, m_s = _mlstm_sample(pad_rows(mqkv_s), pad_rows(so_s), pad_rows(gates_s),
                                        state_C[0].astype(F32), state_n[0].astype(F32), state_m[0].astype(F32),
                                        SAMPLE_SEQ_BLK, ts)
    hm_s = hm_s.reshape(bs, ROWS_PAD, -1)[:, :ts].reshape(ms, -1)

    new_page = lambda a: jnp.pad(a.reshape(a.shape[0], bs, ts).transpose(1, 0, 2), ((0, 0), (0, 0), (0, PAGE_SIZE - ts)))
    qi_s = _unpad_heads(qipad_s, IDX_HEADS).reshape(bs, ts * IDX_HEADS, IDX_DIM)
    wi_s = gates_s[:, L_WI:L_WI + IDX_HEADS].reshape(bs, ts * IDX_HEADS, 1)
    sc_s = _dsa_sample_scores(page_table, qi_s, wi_s, cache_kidx[0].transpose(0, 2, 1), new_page(kit_s[0]).astype(BF16), ts)
    ncol = sc_s.shape[-1]
    topk_s = min(TOPK_MAX, (past + ts) // 4)
    sc_adj, tau_s = _dsa_sample_select(sc_s.reshape(ms, ncol), topk_s, SELECT_ROWS)
    q_s = _unpad_heads(qpad_s, A_HEADS).reshape(bs, ts, A_HEADS, A_HEAD_DIM)
    eye = jnp.eye(A_HEADS, dtype=q_s.dtype)
    qbd = (q_s[:, :, :, None, :] * eye[None, None, :, :, None]).reshape(bs, ts * A_HEADS, A_WIDTH)
    ha_s = _dsa_sample_attend(page_table, qbd, sc_adj.reshape(bs, ts, ncol), tau_s.reshape(bs, ts, LANES),
                              cache_k[0].transpose(0, 2, 3, 1), cache_v[0].transpose(0, 2, 3, 1),
                              new_page(ktb_s[0]), new_page(vtb_s[0]), ts)
    y_s = _block_out(xs, hm_s, ha_s.reshape(ms, -1), wo, g1, b1, wg, wu, wd, g2, b2, min(OUT_TM, ms))

    heads = lambda a, b, t: a.reshape(1, b, A_HEADS, A_HEAD_DIM, t).transpose(0, 1, 4, 2, 3)
    heads_s = lambda a: a[0].T.reshape(1, bs, ts, A_HEADS, A_HEAD_DIM)
    return (y_p.reshape(bp, sp, D_MODEL), y_s.reshape(bs, ts, D_MODEL),
            heads(kt_p, bp, sp), heads(vt_p, bp, sp), kit_p.transpose(0, 2, 1)[None],
            caug[None, :, :, :, 0:M_HEAD_DIM], caug[None, :, :, :, M_HEAD_DIM], m_p[:, 0].reshape(1, bp, M_HEADS),
            heads_s(kt_s), heads_s(vt_s), kit_s[0].T.reshape(1, bs, ts, IDX_DIM),
            c_s[None], n_s[None], m_s[None])
```

```python
import functools
import math

import jax
import jax.numpy as jnp
from jax import lax
from jax.experimental import pallas as pl
from jax.experimental.pallas import tpu as pltpu

D_MODEL = 1024
M_HEADS = 4
M_HEAD_DIM = 128
M_WIDTH = 512
A_HEADS = 8
A_HEAD_DIM = 64
A_WIDTH = 512
IDX_HEADS = 8
IDX_DIM = 64
TOPK_MAX = 256
PAGE_SIZE = 128
ROPE_THETA = 500000.0
ROT = A_HEAD_DIM // 4
HALF = ROT // 2
D_FF = 2816
DEPTH = 1
ALPHA = (2 * DEPTH) ** 0.25
LN_EPS = 1e-5
LOG2E = math.log2(math.e)

LANES = 128
SUBLANES = 8
VMEM_LIMIT = 56 * 1024 * 1024

NEG_INF = float("-inf")
BF16 = jnp.bfloat16
F32 = jnp.float32

C_MQ, C_MK, C_MV, C_MO = 0, 512, 1024, 1536
C_AQ, C_IQ, C_SM = 2048, 2560, 3072
N_PERM = 3200
L_WI, L_IG, L_FG = 64, 72, 76
R_AK, R_AV, R_IK, N_ROWS_T = 0, 512, 1024, 1152


def _cparams(sem):
    return pltpu.CompilerParams(dimension_semantics=sem, vmem_limit_bytes=VMEM_LIMIT)


def _dot(a, b):
    return jnp.dot(a, b, preferred_element_type=F32)


def _dot_nt(a, b):
    return lax.dot_general(a, b, (((1,), (1,)), ((), ())), preferred_element_type=F32)


def _dot_tn(a, b):
    return lax.dot_general(a, b, (((0,), (0,)), ((), ())), preferred_element_type=F32)


def _rope128(x, cos, sa, sb):
    return x * cos + pltpu.roll(x, LANES - HALF, 1) * sa + pltpu.roll(x, HALF, 1) * sb


def _rope_rows(z, cos_t, sin_t):
    a, b = z[0:HALF], z[HALF:ROT]
    return jnp.concatenate([a * cos_t - b * sin_t, b * cos_t + a * sin_t, z[ROT:]], axis=0)


def _proj_kernel(x_ref, w_ref, wt_ref, wgt_ref, bias_ref, biast_ref, cos_ref, sa_ref, sb_ref, cost_ref, sint_ref,
                 mqkv_ref, so_ref, gates_ref, gatest_ref, qpad_ref, qipad_ref,
                 kt_ref, ktb_ref, vt_ref, vtb_ref, kit_ref, kit2_ref):
    xb = x_ref[...].astype(BF16)
    cos, sa, sb = cos_ref[...], sa_ref[...], sb_ref[...]
    cos_t, sin_t = cost_ref[...], sint_ref[...]
    lane = lax.broadcasted_iota(jnp.int32, (1, LANES), 1)
    lo_half = lane < A_HEAD_DIM

    zm = _dot(xb, w_ref[:, C_MQ:C_MO])
    mqkv_ref[:, 0:M_WIDTH] = zm[:, 0:M_WIDTH].astype(BF16)
    mqkv_ref[:, M_WIDTH:2 * M_WIDTH] = (zm[:, M_WIDTH:2 * M_WIDTH] * (M_HEAD_DIM ** -0.5)).astype(BF16)
    mqkv_ref[:, 2 * M_WIDTH:3 * M_WIDTH] = zm[:, 2 * M_WIDTH:3 * M_WIDTH].astype(BF16)
    so_ref[...] = jax.nn.sigmoid(_dot(xb, w_ref[:, C_MO:C_AQ]))

    def padded_heads(z, scale, out_ref):
        for p in range(A_WIDTH // LANES):
            r = _rope128(z[:, p * LANES:(p + 1) * LANES], cos, sa, sb) * scale
            out_ref[:, (2 * p) * LANES:(2 * p + 1) * LANES] = jnp.where(lo_half, r, 0.0).astype(BF16)
            out_ref[:, (2 * p + 1) * LANES:(2 * p + 2) * LANES] = jnp.where(lo_half, 0.0, r).astype(BF16)

    padded_heads(_dot(xb, w_ref[:, C_AQ:C_IQ]), (A_HEAD_DIM ** -0.5) * LOG2E, qpad_ref)
    padded_heads(_dot(xb, w_ref[:, C_IQ:C_SM]), IDX_DIM ** -0.5, qipad_ref)

    zs = _dot(xb, w_ref[:, C_SM:N_PERM]) + bias_ref[...]
    is_wi = (lane >= L_WI) & (lane < L_IG)
    is_fg = (lane >= L_FG) & (lane < L_FG + M_HEADS)
    g = jnp.where(is_wi, zs * (IDX_HEADS ** -0.5), zs)
    gates_ref[...] = jnp.where(is_fg, jax.nn.log_sigmoid(zs), g)

    zt = _dot_nt(wgt_ref[...], xb) + biast_ref[...]
    row = lax.broadcasted_iota(jnp.int32, (2 * M_HEADS, 1), 0)
    gatest_ref[...] = jnp.where(row >= M_HEADS, jax.nn.log_sigmoid(zt), zt)

    zkt = _dot_nt(wt_ref[R_AK:R_AV, :], xb)
    for h in range(A_HEADS):
        r = _rope_rows(zkt[h * A_HEAD_DIM:(h + 1) * A_HEAD_DIM], cos_t, sin_t)
        kt_ref[h * A_HEAD_DIM:(h + 1) * A_HEAD_DIM, :] = r
        ktb_ref[h * A_HEAD_DIM:(h + 1) * A_HEAD_DIM, :] = r.astype(BF16)
    zvt = _dot_nt(wt_ref[R_AV:R_IK, :], xb)
    vt_ref[...] = zvt
    vtb_ref[...] = zvt.astype(BF16)
    zit = _dot_nt(wt_ref[R_IK:N_ROWS_T, :], xb)
    r = _rope_rows(zit[0:IDX_DIM], cos_t, sin_t)
    kit_ref[...] = r
    kit2_ref[...] = jnp.concatenate([r, r], axis=0).astype(BF16)


def _rope_tables(pos):
    inv = ROPE_THETA ** (-jnp.arange(HALF, dtype=F32) / HALF)
    posf = pos.astype(F32)
    ang_t = inv[:, None] * posf[None, :]
    dim = jnp.arange(LANES) % A_HEAD_DIM
    ang = posf[:, None] * inv[dim % HALF][None, :]
    c, s = jnp.cos(ang), jnp.sin(ang)
    cos_m = jnp.where(dim < ROT, c, 1.0)
    sa_m = jnp.where(dim < HALF, -s, 0.0)
    sb_m = jnp.where((dim >= HALF) & (dim < ROT), s, 0.0)
    return (cos_m, sa_m, sb_m), (jnp.cos(ang_t), jnp.sin(ang_t))


def _proj(x, w, tabs, tabs_t, tm, nb, seq):
    wp, wt, wgt, bias, biast = w
    m = x.shape[0]
    nblk = seq // tm
    row = lambda wd: pl.BlockSpec((tm, wd), lambda i: (i, 0))
    full = lambda a: pl.BlockSpec(a.shape, lambda i: (0, 0))
    tab = pl.BlockSpec((tm, LANES), lambda i: (i % nblk, 0))
    tab_t = pl.BlockSpec((HALF, tm), lambda i: (0, i % nblk))
    fmaj = lambda r: pl.BlockSpec((None, r, tm), lambda i: (i // nblk, 0, i % nblk))
    outs = [
        ((m, 3 * M_WIDTH), BF16, row(3 * M_WIDTH)),
        ((m, M_WIDTH), F32, row(M_WIDTH)),
        ((m, LANES), F32, row(LANES)),
        ((2 * M_HEADS, m), F32, pl.BlockSpec((2 * M_HEADS, tm), lambda i: (0, i))),
        ((m, 2 * A_WIDTH), BF16, row(2 * A_WIDTH)),
        ((m, 2 * A_WIDTH), BF16, row(2 * A_WIDTH)),
        ((nb, A_WIDTH, seq), F32, fmaj(A_WIDTH)),
        ((nb, A_WIDTH, seq), BF16, fmaj(A_WIDTH)),
        ((nb, A_WIDTH, seq), F32, fmaj(A_WIDTH)),
        ((nb, A_WIDTH, seq), BF16, fmaj(A_WIDTH)),
        ((nb, IDX_DIM, seq), F32, fmaj(IDX_DIM)),
        ((nb, 2 * IDX_DIM, seq), BF16, fmaj(2 * IDX_DIM)),
    ]
    return pl.pallas_call(
        _proj_kernel,
        grid=(m // tm,),
        in_specs=[row(D_MODEL), full(wp), full(wt), full(wgt), full(bias), full(biast), tab, tab, tab, tab_t, tab_t],
        out_specs=[o[2] for o in outs],
        out_shape=[jax.ShapeDtypeStruct(o[0], o[1]) for o in outs],
        compiler_params=_cparams(("parallel",)),
        name="proj",
    )(x, wp, wt, wgt, bias, biast, *tabs, *tabs_t)


def _prep_w_in(w_in, b_gate):
    s = [0, 512, 1024, 1536, 2048, 2052, 2056, 2568, 3080, 3592, 4104, 4168, 4176]
    w16 = w_in.astype(BF16)
    mq, mk, mv, mo, mi, mf, aq, ak, av, iq, ik, iw = [w16[:, s[i]:s[i + 1]] for i in range(12)]
    z64 = jnp.zeros((D_MODEL, L_WI), BF16)
    pad = jnp.zeros((D_MODEL, N_PERM - C_SM - L_FG - M_HEADS), BF16)
    wp = jnp.concatenate([mq, mk, mv, mo, aq, iq, z64, iw, mi, mf, pad], axis=1)
    wt = jnp.concatenate([ak, av, ik, ik], axis=1).T
    wgt = jnp.concatenate([mi, mf], axis=1).T
    bg = b_gate.astype(F32)
    bias = jnp.zeros((1, LANES), F32).at[0, L_IG:L_IG + 2 * M_HEADS].set(bg)
    return wp, wt, wgt, bias, bg[:, None]


FF_CHUNK = D_FF // 2


def _layer_norm(x, g, b):
    mu = jnp.mean(x, axis=-1, keepdims=True)
    xc = x - mu
    var = jnp.mean(xc * xc, axis=-1, keepdims=True)
    return xc * lax.rsqrt(var + LN_EPS) * g + b


def _block_out_kernel(x_ref, hm_ref, ha_ref, wo_ref, g1_ref, b1_ref, wg_ref, wu_ref, wd_ref, g2_ref, b2_ref, y_ref):
    mix = _dot(hm_ref[...], wo_ref[0:M_WIDTH, :]) + _dot(ha_ref[...], wo_ref[M_WIDTH:2 * M_WIDTH, :])
    x1 = _layer_norm(ALPHA * x_ref[...] + mix, g1_ref[...], b1_ref[...])
    xb = x1.astype(BF16)
    ff = jnp.zeros_like(x1)
    for c in range(D_FF // FF_CHUNK):
        sl = slice(c * FF_CHUNK, (c + 1) * FF_CHUNK)
        act = jax.nn.silu(_dot(xb, wg_ref[:, sl])) * _dot(xb, wu_ref[:, sl])
        ff = ff + _dot(act.astype(BF16), wd_ref[sl, :])
    y_ref[...] = _layer_norm(ALPHA * x1 + ff, g2_ref[...], b2_ref[...])


def _block_out(x, hm, ha, wo, g1, b1, wg, wu, wd, g2, b2, tm):
    m = x.shape[0]
    row = lambda w: pl.BlockSpec((tm, w), lambda i: (i, 0))
    full = lambda a: pl.BlockSpec(a.shape, lambda i: (0, 0), pipeline_mode=pl.Buffered(1))
    return pl.pallas_call(
        _block_out_kernel,
        grid=(m // tm,),
        in_specs=[row(D_MODEL), row(M_WIDTH), row(A_WIDTH), full(wo), full(g1), full(b1),
                  full(wg), full(wu), full(wd), full(g2), full(b2)],
        out_specs=row(D_MODEL),
        out_shape=jax.ShapeDtypeStruct((m, D_MODEL), F32),
        compiler_params=_cparams(("parallel",)),
        name="block_out",
    )(x, hm, ha, wo, g1, b1, wg, wu, wd, g2, b2)


def _split3(x):
    h1 = x.astype(BF16)
    r1 = x - h1.astype(F32)
    h2 = r1.astype(BF16)
    h3 = (r1 - h2.astype(F32)).astype(BF16)
    return h1, h2, h3


def _dot3(x, w):
    h1, h2, h3 = _split3(x)
    return _dot(h1, w) + _dot(h2, w) + _dot(h3, w)


def _mlstm_chunk(q, k, v, lf_row, i_row, i_col, c_aug, m_prev):
    L = q.shape[0]
    t_idx = lax.broadcasted_iota(jnp.int32, (L, L), 0)
    s_idx = lax.broadcasted_iota(jnp.int32, (L, L), 1)
    causal = s_idx <= t_idx
    ones_b = jnp.ones((L, LANES), BF16)
    G = _dot3(jnp.where(causal, lf_row, 0.0), ones_b)
    tri_u = jnp.where(t_idx <= s_idx, 1.0, 0.0).astype(BF16)
    f_row = _dot3(jnp.broadcast_to(lf_row, (SUBLANES, L)), tri_u)[0:1, :]
    f_col = G[:, 0:1]
    dlog = jnp.where(causal, f_col - f_row + i_row, NEG_INF)
    inter = m_prev + f_col
    m_t = jnp.maximum(jnp.max(dlog, axis=1, keepdims=True), inter)
    w = jnp.exp(dlog - m_t)
    g = jnp.exp(inter - m_t)
    s = _dot_nt(q, k) * w
    lane = lax.broadcasted_iota(jnp.int32, (L, LANES), 1)
    v_aug = jnp.concatenate([v, jnp.where(lane == 0, 1.0, 0.0).astype(BF16)], axis=1)
    nd = _dot(s.astype(BF16), v_aug) + g * _dot(q, c_aug.astype(BF16))
    den = nd[:, M_HEAD_DIM:M_HEAD_DIM + 1]
    h = nd[:, 0:M_HEAD_DIM] / jnp.maximum(jnp.abs(den), jnp.exp(-m_t))
    m_new = m_t[L - 1:L, :]
    g_c = g[L - 1:L, :]
    wk = jnp.exp(f_col[L - 1:L, :] - f_col + i_col - m_new)
    kw = (k.astype(F32) * wk).astype(BF16)
    c_new = g_c * c_aug + _dot_tn(kw, v_aug)
    return h, c_new, m_new


def _mlstm_prompt_kernel(*refs, nb):
    mqkv_ref, so_ref, g_ref = refs[0:3]
    gt_refs = refs[3:3 + nb]
    hm_ref, c_ref, m_ref = refs[3 + nb:]
    step = pl.program_id(0)

    @pl.when(step == 0)
    def _():
        c_ref[...] = jnp.zeros_like(c_ref)
        m_ref[...] = jnp.zeros_like(m_ref)

    for b in range(nb):
        for h in range(M_HEADS):
            hs = slice(h * M_HEAD_DIM, (h + 1) * M_HEAD_DIM)
            q = mqkv_ref[b, :, hs]
            k = mqkv_ref[b, :, M_WIDTH + h * M_HEAD_DIM:M_WIDTH + (h + 1) * M_HEAD_DIM]
            v = mqkv_ref[b, :, 2 * M_WIDTH + h * M_HEAD_DIM:2 * M_WIDTH + (h + 1) * M_HEAD_DIM]
            i_row = gt_refs[b][h:h + 1, :]
            lf_row = gt_refs[b][M_HEADS + h:M_HEADS + h + 1, :]
            i_col = g_ref[b, :, L_IG + h:L_IG + h + 1]
            r = b * M_HEADS + h
            m_prev = m_ref[r:r + 1, 0:1]
            hh, c_new, m_new = _mlstm_chunk(q, k, v, lf_row, i_row, i_col, c_ref[b, h], m_prev)
            c_ref[b, h] = c_new
            m_ref[r:r + 1, :] = jnp.broadcast_to(m_new, (1, LANES))
            hm_ref[b, :, hs] = (hh * so_ref[b, :, hs]).astype(BF16)


def _mlstm_prompt(mqkv, so, gates, gatest, nb, seq, chunk):
    nc = seq // chunk
    blk = lambda w: pl.BlockSpec((nb, chunk, w), lambda c: (0, c, 0))
    gt_specs = [pl.BlockSpec((2 * M_HEADS, chunk), lambda c, b=b: (0, b * nc + c)) for b in range(nb)]
    return pl.pallas_call(
        functools.partial(_mlstm_prompt_kernel, nb=nb),
        grid=(nc,),
        in_specs=[blk(3 * M_WIDTH), blk(M_WIDTH), blk(LANES)] + gt_specs,
        out_specs=[blk(M_WIDTH),
                   pl.BlockSpec((nb, M_HEADS, M_HEAD_DIM, 2 * M_HEAD_DIM), lambda c: (0, 0, 0, 0)),
                   pl.BlockSpec((nb * M_HEADS, LANES), lambda c: (0, 0))],
        out_shape=[jax.ShapeDtypeStruct((nb, seq, M_WIDTH), BF16),
                   jax.ShapeDtypeStruct((nb, M_HEADS, M_HEAD_DIM, 2 * M_HEAD_DIM), F32),
                   jax.ShapeDtypeStruct((nb * M_HEADS, LANES), F32)],
        compiler_params=_cparams(("arbitrary",)),
        name="mlstm_prompt",
    )(mqkv, so, gates, *([gatest] * nb))


ROWS_PAD = SUBLANES


def _mlstm_sample_kernel(mqkv_ref, so_ref, g_ref, c0_ref, n0_ref, m0_ref, hm_ref, c_ref, n_ref, m_ref, *, nseq, t_real):
    row = lax.broadcasted_iota(jnp.int32, (ROWS_PAD, 1), 0)
    real = row < t_real

    def per_seq(s, carry):
        r0 = pl.multiple_of(s * ROWS_PAD, ROWS_PAD)
        gt = g_ref[pl.ds(r0, ROWS_PAD), :]
        cum = gt
        for d in range(1, t_real):
            cum = cum + jnp.where(row >= d, pltpu.roll(gt, d, 0), 0.0)
        for h in range(M_HEADS):
            hs = slice(h * M_HEAD_DIM, (h + 1) * M_HEAD_DIM)
            q = mqkv_ref[pl.ds(r0, ROWS_PAD), hs]
            k = mqkv_ref[pl.ds(r0, ROWS_PAD), M_WIDTH + h * M_HEAD_DIM:M_WIDTH + (h + 1) * M_HEAD_DIM]
            v = mqkv_ref[pl.ds(r0, ROWS_PAD), 2 * M_WIDTH + h * M_HEAD_DIM:2 * M_WIDTH + (h + 1) * M_HEAD_DIM]
            qf, kf, vf = q.astype(F32), k.astype(F32), v.astype(F32)
            i_col = gt[:, L_IG + h:L_IG + h + 1]
            f_col = cum[:, L_FG + h:L_FG + h + 1]
            c0 = c0_ref[s, h]
            n0 = n0_ref[s, h:h + 1, :]
            m0 = m0_ref[pl.ds(s, 1), h:h + 1]
            inter = m0 + f_col
            dl = [jnp.where(real & (row >= u), f_col - f_col[u:u + 1, :] + i_col[u:u + 1, :], NEG_INF)
                  for u in range(t_real)]
            m_t = inter
            for u in range(t_real):
                m_t = jnp.maximum(m_t, dl[u])
            g = jnp.exp(inter - m_t)
            qc = _dot(q, c0.astype(BF16))
            num = g * qc
            den = g * jnp.sum(qf * n0, axis=1, keepdims=True)
            for u in range(t_real):
                su = jnp.sum(qf * kf[u:u + 1, :], axis=1, keepdims=True) * jnp.exp(dl[u] - m_t)
                num = num + su * vf[u:u + 1, :]
                den = den + su
            hh = num / jnp.maximum(jnp.abs(den), jnp.exp(-m_t))
            hm_ref[pl.ds(r0, ROWS_PAD), hs] = (hh * so_ref[pl.ds(r0, ROWS_PAD), hs]).astype(BF16)
            last = t_real - 1
            m_new = m_t[last:last + 1, :]
            g_c = g[last:last + 1, :]
            wk = jnp.where(real, jnp.exp(f_col[last:last + 1, :] - f_col + i_col - m_new), 0.0)
            kw = kf * wk
            c_ref[s, h] = g_c * c0 + _dot_tn(kw.astype(BF16), v)
            n_ref[s, h:h + 1, :] = g_c * n0 + jnp.sum(kw, axis=0, keepdims=True)
            m_ref[pl.ds(s, 1), h:h + 1] = m_new
        return carry

    lax.fori_loop(0, nseq, per_seq, 0)


def _mlstm_sample(mqkv, so, gates, c0, n0, m0, nseq_blk, t_real):
    nseq = c0.shape[0]
    rows = nseq_blk * ROWS_PAD
    rblk = lambda w: pl.BlockSpec((rows, w), lambda i: (i, 0))
    cblk = pl.BlockSpec((nseq_blk, M_HEADS, M_HEAD_DIM, M_HEAD_DIM), lambda i: (i, 0, 0, 0))
    nblk = pl.BlockSpec((nseq_blk, M_HEADS, M_HEAD_DIM), lambda i: (i, 0, 0))
    mblk = pl.BlockSpec((nseq_blk, M_HEADS), lambda i: (i, 0))
    return pl.pallas_call(
        functools.partial(_mlstm_sample_kernel, nseq=nseq_blk, t_real=t_real),
        grid=(nseq // nseq_blk,),
        in_specs=[rblk(3 * M_WIDTH), rblk(M_WIDTH), rblk(LANES), cblk, nblk, mblk],
        out_specs=[rblk(M_WIDTH), cblk, nblk, mblk],
        out_shape=[jax.ShapeDtypeStruct((nseq * ROWS_PAD, M_WIDTH), BF16),
                   jax.ShapeDtypeStruct(c0.shape, F32),
                   jax.ShapeDtypeStruct(n0.shape, F32),
                   jax.ShapeDtypeStruct(m0.shape, F32)],
        compiler_params=_cparams(("parallel",)),
        name="mlstm_sample",
    )(mqkv, so, gates, c0, n0, m0)


N_BISECT = 15
MASKED = -1e30
ROW_BLOCK = 128
SELECT_TK = 512
NORM_TK = 1024
SCORE_ROWS = 64
MIN_ROW_SUM = 2.0 ** -100


def _block_pass(sc_ref, r0, rb, nch, tk, init, fn, params=()):
    def body(c, acc):
        c0 = pl.multiple_of(c * tk, tk)
        blk = sc_ref[r0:r0 + rb, pl.ds(c0, tk)]
        for j in range(tk // LANES):
            acc = fn(acc, blk[:, j * LANES:(j + 1) * LANES], c0 + j * LANES, params)
        return acc

    return lax.fori_loop(0, nch, body, init)


def _row_pass(sc_ref, nch, tk, init, fn, params=()):
    rows = sc_ref.shape[0]
    rb = min(ROW_BLOCK, rows)
    rsl = lambda t, r0: jax.tree.map(lambda a: a[r0:r0 + rb], t)
    outs = [_block_pass(sc_ref, r0, rb, nch, tk, rsl(init, r0), fn, rsl(params, r0)) for r0 in range(0, rows, rb)]
    return jax.tree.map(lambda *a: jnp.concatenate(a, axis=0), *outs)


def _select_threshold(sc_ref, nch, tk, kr, smax, stats=None):
    rows = sc_ref.shape[0]
    zeros = jnp.zeros((rows, LANES), F32)
    lsum = lambda a: jnp.sum(a, axis=1, keepdims=True)
    full = lambda col: jnp.broadcast_to(col, (rows, LANES))

    def count_ge(thr):
        return lsum(_row_pass(sc_ref, nch, tk, zeros, lambda a, x, _, t: a + jnp.where(x >= t, 1.0, 0.0), full(thr)))

    def range_fn(a, x, _, prm):
        fin = x > NEG_INF
        return (jnp.maximum(a[0], x), jnp.minimum(a[1], jnp.where(fin, x, jnp.inf)), a[2] + jnp.where(fin, 1.0, 0.0))

    if stats is None:
        mx, mn, nf = _row_pass(
            sc_ref, nch, tk, (jnp.full((rows, LANES), NEG_INF, F32), jnp.full((rows, LANES), jnp.inf, F32), zeros),
            range_fn)
        rmax, rmin, n_fin = jnp.max(mx, axis=1, keepdims=True), jnp.min(mn, axis=1, keepdims=True), lsum(nf)
    else:
        rmax, rmin, n_fin = stats

    above = rmax + jnp.maximum(jnp.abs(rmax) * (2.0 ** -20), 2.0 ** -100)

    def bisect(_, st):
        lo, hi, c_lo, c_hi = st
        mid = 0.5 * (lo + hi)
        c = count_ge(mid)
        ge = c >= kr
        return jnp.where(ge, mid, lo), jnp.where(ge, hi, mid), jnp.where(ge, c, c_lo), jnp.where(ge, c_hi, c)

    lo, hi, c_lo, c_hi = lax.fori_loop(0, N_BISECT, bisect, (rmin, above, n_fin, jnp.zeros_like(rmax)))

    rb = min(ROW_BLOCK, rows)
    tau_b, c_gt_b, c_ge_b = [], [], []
    for r0 in range(0, rows, rb):
        blk = lambda a, r0=r0: a[r0:r0 + rb]
        kr_b = blk(kr)
        bfull = lambda col: jnp.broadcast_to(col, (rb, LANES))

        def count_ge_b(thr, r0=r0):
            return lsum(_block_pass(sc_ref, r0, rb, nch, tk, jnp.zeros((rb, LANES), F32),
                                    lambda a, x, _, t: a + jnp.where(x >= t, 1.0, 0.0), bfull(thr)))

        def max_below_b(thr, r0=r0):
            acc = _block_pass(sc_ref, r0, rb, nch, tk, jnp.full((rb, LANES), NEG_INF, F32),
                              lambda a, x, _, t: jnp.maximum(a, jnp.where(x < t, x, NEG_INF)), bfull(thr))
            return jnp.max(acc, axis=1, keepdims=True)

        def peel_cond(st):
            it, _, _, _, _, _, done = st
            return jnp.logical_and(it <= smax, jnp.min(done) < 0.5)

        def peel(st, kr_b=kr_b, count_ge_b=count_ge_b, max_below_b=max_below_b):
            it, hi_b, c_hi_b, tau, c_gt, c_ge, done = st
            t1 = max_below_b(hi_b)
            c1 = count_ge_b(t1)
            fin = jnp.logical_and(c1 >= kr_b, done < 0.5)
            tau = jnp.where(fin, t1, tau)
            c_gt = jnp.where(fin, c_hi_b, c_gt)
            c_ge = jnp.where(fin, c1, c_ge)
            done = jnp.where(fin, 1.0, done)
            live = done < 0.5
            return it + 1, jnp.where(live, t1, hi_b), jnp.where(live, c1, c_hi_b), tau, c_gt, c_ge, done

        done0 = jnp.where(blk(c_lo) - kr_b > 0.5, 0.0, 1.0)
        st = (jnp.int32(0), blk(hi), blk(c_hi), blk(lo), jnp.zeros((rb, 1), F32), kr_b, done0)
        _, _, _, t_b, g_b, e_b, _ = lax.while_loop(peel_cond, peel, st)
        tau_b.append(t_b)
        c_gt_b.append(g_b)
        c_ge_b.append(e_b)
    tau, c_gt, c_ge = (jnp.concatenate(v, axis=0) for v in (tau_b, c_gt_b, c_ge_b))

    need = kr - c_gt
    surplus = (c_ge - c_gt) - need
    k_idx = lax.broadcasted_iota(jnp.int32, (LANES, 2 * LANES), 0)
    j_idx = lax.broadcasted_iota(jnp.int32, (LANES, 2 * LANES), 1)
    tri_ones = jnp.where(jnp.logical_or(k_idx <= j_idx, j_idx >= LANES), 1.0, 0.0).astype(BF16)
    for r0 in range(0, rows, rb):
        @pl.when(jnp.max(surplus[r0:r0 + rb]) > 0.5)
        def _(r0=r0):
            tau_f = jnp.broadcast_to(tau[r0:r0 + rb], (rb, LANES))
            need_f = jnp.broadcast_to(need[r0:r0 + rb], (rb, LANES))

            def sweep(c, seen):
                c0 = pl.multiple_of(c * tk, tk)
                x = sc_ref[r0:r0 + rb, pl.ds(c0, tk)]
                tiles = [x[:, j * LANES:(j + 1) * LANES] for j in range(tk // LANES)]
                cnts = [_dot(jnp.where(t == tau_f, 1.0, 0.0).astype(BF16), tri_ones) for t in tiles]
                out = []
                for t, cnt in zip(tiles, cnts):
                    rank = seen + cnt[:, 0:LANES]
                    out.append(jnp.where(jnp.logical_and(t == tau_f, rank > need_f), NEG_INF, t))
                    seen = seen + cnt[:, LANES:2 * LANES]
                sc_ref[r0:r0 + rb, pl.ds(c0, tk)] = jnp.concatenate(out, axis=1)
                return seen

            lax.fori_loop(0, nch, sweep, jnp.zeros((rb, LANES), F32))

    return tau


def _ones_rows(tk):
    return jnp.where(lax.broadcasted_iota(jnp.int32, (LANES, tk), 0) == 0, 1.0, 0.0).astype(BF16)


def _dsa_prompt_kernel(qi_ref, g_ref, q_ref, kit_ref, kt_ref, vt_ref, o_ref, sc_ref, tau_scr, m_scr, acc_scr, kn_scr,
                       *, tq, tk, topk, smax):
    i = pl.program_id(1)
    nch = ((i + 1) * tq + tk - 1) // tk
    row = lax.broadcasted_iota(jnp.int32, (tq, 1), 0) + i * tq

    m_scr[0] = jnp.full((tq, LANES), NEG_INF, F32)
    m_scr[1] = jnp.full((tq, LANES), jnp.inf, F32)

    def score_chunk(c, carry):
        c0 = pl.multiple_of(c * tk, tk)
        kblk = kit_ref[:, pl.ds(c0, tk)]
        col = lax.broadcasted_iota(jnp.int32, (1, LANES), 1) + c0
        for r0 in range(0, tq, SCORE_ROWS):
            rs = slice(r0, r0 + SCORE_ROWS)
            acc = jnp.zeros((SCORE_ROWS, tk), F32)
            for h in range(IDX_HEADS):
                x = _dot(qi_ref[rs, h * LANES:(h + 1) * LANES], kblk)
                acc = acc + jnp.maximum(x, 0.0) * g_ref[rs, L_WI + h:L_WI + h + 1]
            hi_p, lo_p = m_scr[0, rs], m_scr[1, rs]
            for j in range(tk // LANES):
                causal = col + j * LANES <= row[rs]
                a = acc[:, j * LANES:(j + 1) * LANES]
                masked = jnp.where(causal, a, NEG_INF)
                sc_ref[rs, pl.ds(pl.multiple_of(c0 + j * LANES, LANES), LANES)] = masked
                hi_p = jnp.maximum(hi_p, masked)
                lo_p = jnp.minimum(lo_p, jnp.where(causal, a, jnp.inf))
            m_scr[0, rs], m_scr[1, rs] = hi_p, lo_p
        return carry

    lax.fori_loop(0, nch, score_chunk, 0)

    kr = jnp.minimum(row + 1, topk).astype(F32)
    stats = (jnp.max(m_scr[0], axis=1, keepdims=True), jnp.min(m_scr[1], axis=1, keepdims=True), (row + 1).astype(F32))
    tau = _select_threshold(sc_ref, nch * (tk // SELECT_TK), SELECT_TK, kr, smax, stats)

    nslab = tk // LANES
    tau_scr[...] = jnp.broadcast_to(tau, (tq, LANES))

    def masked_logits(c0, h):
        pr = slice((h // 2) * LANES, (h // 2 + 1) * LANES)
        s = _dot(q_ref[:, h * LANES:(h + 1) * LANES], kt_ref[pr, pl.ds(c0, tk)])
        x = sc_ref[:, pl.ds(c0, tk)]
        thr = tau_scr[...]
        return [jnp.where(x[:, j * LANES:(j + 1) * LANES] >= thr, s[:, j * LANES:(j + 1) * LANES], MASKED)
                for j in range(nslab)]

    def max_sweep():
        m_scr[...] = jnp.full(m_scr.shape, MASKED, F32)

        def max_chunk(c, carry):
            c0 = pl.multiple_of(c * tk, tk)
            for h in range(A_HEADS):
                slabs = masked_logits(c0, h)
                part = slabs[0]
                for j in range(1, nslab):
                    part = jnp.maximum(part, slabs[j])
                m_scr[h] = jnp.maximum(m_scr[h], part)
            return carry

        lax.fori_loop(0, nch, max_chunk, 0)
        for h in range(A_HEADS):
            m_scr[h] = jnp.broadcast_to(jnp.max(m_scr[h], axis=1, keepdims=True), (tq, LANES))

    ones = _ones_rows(tk)

    def attend_sweep():
        def attend_chunk(c, first):
            c0 = 0 if first else pl.multiple_of(c * tk, tk)
            for h in range(A_HEADS):
                pr = slice((h // 2) * LANES, (h // 2 + 1) * LANES)
                m_h = m_scr[h]
                pb = jnp.concatenate([jnp.exp2(sl - m_h).astype(BF16) for sl in masked_logits(c0, h)], axis=1)
                vt_aug = jnp.concatenate([vt_ref[pr, pl.ds(c0, tk)], ones], axis=0)
                d = _dot_nt(pb, vt_aug)
                acc_scr[h] = d if first else acc_scr[h] + d

        attend_chunk(0, True)
        lax.fori_loop(1, nch, lambda c, carry: attend_chunk(c, False) or carry, 0)

    @pl.when(i == 0)
    def _():
        for h in range(A_HEADS):
            def norm_chunk(c, best, h=h):
                c0 = pl.multiple_of(c * NORM_TK, NORM_TK)
                kk = kt_ref[h * A_HEAD_DIM:(h + 1) * A_HEAD_DIM, pl.ds(c0, NORM_TK)].astype(F32)
                return jnp.maximum(best, jnp.sum(kk * kk, axis=0, keepdims=True))
            best = lax.fori_loop(0, smax // NORM_TK, norm_chunk, jnp.zeros((1, NORM_TK), F32))
            kn_scr[h:h + 1, :] = jnp.broadcast_to(jnp.max(best, axis=1, keepdims=True), (1, LANES))

    for h in range(A_HEADS):
        qh = q_ref[:, h * LANES:(h + 1) * LANES].astype(F32)
        n2 = jnp.sum(qh * qh, axis=1, keepdims=True) * kn_scr[h:h + 1, 0:1]
        m_scr[h] = jnp.broadcast_to(jnp.where(n2 > 0.0, n2 * lax.rsqrt(n2), 0.0), (tq, LANES))
    attend_sweep()
    l_min = acc_scr[0][:, LANES:LANES + 1]
    for h in range(1, A_HEADS):
        l_min = jnp.minimum(l_min, acc_scr[h][:, LANES:LANES + 1])

    @pl.when(jnp.logical_not(jnp.min(l_min) >= MIN_ROW_SUM))
    def _():
        max_sweep()
        attend_sweep()

    lo_half = lax.broadcasted_iota(jnp.int32, (1, LANES), 1) < A_HEAD_DIM
    for p in range(A_HEADS // 2):
        a_e, a_o = acc_scr[2 * p], acc_scr[2 * p + 1]
        inv_l = jnp.where(lo_half, 1.0 / a_e[:, LANES:LANES + 1], 1.0 / a_o[:, LANES:LANES + 1])
        o_ref[:, p * LANES:(p + 1) * LANES] = (jnp.where(lo_half, a_e[:, 0:LANES], a_o[:, 0:LANES]) * inv_l).astype(BF16)


def _dsa_prompt(qipad, gates, qpad, kit2, ktb, vtb, nb, seq, tq, tk):
    nq = seq // tq
    topk = min(TOPK_MAX, seq // 4)
    qrow = lambda w: pl.BlockSpec((tq, w), lambda b, i: (b * nq + i, 0))
    kfull = lambda r: pl.BlockSpec((None, r, seq), lambda b, i: (b, 0, 0), pipeline_mode=pl.Buffered(1))
    return pl.pallas_call(
        functools.partial(_dsa_prompt_kernel, tq=tq, tk=tk, topk=topk, smax=seq),
        grid=(nb, nq),
        in_specs=[qrow(2 * A_WIDTH), qrow(LANES), qrow(2 * A_WIDTH), kfull(2 * IDX_DIM), kfull(A_WIDTH), kfull(A_WIDTH)],
        out_specs=qrow(A_WIDTH),
        out_shape=jax.ShapeDtypeStruct((nb * seq, A_WIDTH), BF16),
        scratch_shapes=[pltpu.VMEM((tq, seq + LANES), F32),
                        pltpu.VMEM((tq, LANES), F32),
                        pltpu.VMEM((A_HEADS, tq, LANES), F32),
                        pltpu.VMEM((A_HEADS, tq, 2 * LANES), F32),
                        pltpu.VMEM((A_HEADS, LANES), F32)],
        compiler_params=_cparams(("parallel", "arbitrary")),
        name="dsa_prompt",
    )(qipad, gates, qpad, kit2, ktb, vtb)


def _dsa_sample_scores_kernel(pt_ref, q_ref, w_ref, *refs, n_pages, t_real):
    page_refs, new_ref, o_ref = refs[:n_pages], refs[n_pages], refs[n_pages + 1]
    q = q_ref[0]
    w = w_ref[0]
    ncol = (n_pages + 1) * PAGE_SIZE
    keys_t = jnp.concatenate([pr[0].astype(BF16) for pr in page_refs] + [new_ref[0]], axis=1)
    r = jnp.maximum(_dot(q, keys_t), 0.0) * w
    sc = jnp.sum(r.reshape(t_real, IDX_HEADS, ncol), axis=1)
    col = lax.broadcasted_iota(jnp.int32, (1, ncol), 1) - n_pages * PAGE_SIZE
    trow = lax.broadcasted_iota(jnp.int32, (t_real, 1), 0)
    o_ref[0] = jnp.where(col <= trow, sc, NEG_INF)


def _dsa_sample_scores(page_table, qi, wi, cache_kidx_t, ki_new_t, t_real):
    nseq, n_pages = page_table.shape
    ncol = (n_pages + 1) * PAGE_SIZE
    per_seq = lambda a: pl.BlockSpec((1,) + a.shape[1:], lambda b, pt: (b, 0, 0))
    page = lambda p: pl.BlockSpec((1, IDX_DIM, PAGE_SIZE), lambda b, pt, p=p: (pt[b * n_pages + p], 0, 0))
    return pl.pallas_call(
        functools.partial(_dsa_sample_scores_kernel, n_pages=n_pages, t_real=t_real),
        grid_spec=pltpu.PrefetchScalarGridSpec(
            num_scalar_prefetch=1,
            grid=(nseq,),
            in_specs=[per_seq(qi), per_seq(wi)] + [page(p) for p in range(n_pages)] + [per_seq(ki_new_t)],
            out_specs=pl.BlockSpec((1, t_real, ncol), lambda b, pt: (b, 0, 0)),
        ),
        out_shape=jax.ShapeDtypeStruct((nseq, t_real, ncol), F32),
        compiler_params=_cparams(("parallel",)),
        name="dsa_sample_scores",
    )(page_table.reshape(-1), qi, wi, *([cache_kidx_t] * n_pages), ki_new_t)


def _dsa_sample_select_kernel(sc_ref, adj_ref, tau_ref, *, tk, topk):
    adj_ref[...] = sc_ref[...]
    rows, ncol = adj_ref.shape
    kr = jnp.full((rows, 1), float(topk), F32)
    tau = _select_threshold(adj_ref, ncol // tk, tk, kr, ncol)
    tau_ref[...] = jnp.broadcast_to(tau, (rows, LANES))


def _dsa_sample_select(sc, topk, rblk):
    rows, ncol = sc.shape
    return pl.pallas_call(
        functools.partial(_dsa_sample_select_kernel, tk=LANES, topk=topk),
        grid=(rows // rblk,),
        in_specs=[pl.BlockSpec((rblk, ncol), lambda i: (i, 0))],
        out_specs=[pl.BlockSpec((rblk, ncol), lambda i: (i, 0)), pl.BlockSpec((rblk, LANES), lambda i: (i, 0))],
        out_shape=[jax.ShapeDtypeStruct((rows, ncol), F32), jax.ShapeDtypeStruct((rows, LANES), F32)],
        compiler_params=_cparams(("parallel",)),
        name="dsa_sample_select",
    )(sc)


def _dsa_sample_attend_kernel(pt_ref, q_ref, sc_ref, tau_ref, *refs, n_pages, t_real):
    kp, vp = refs[:n_pages], refs[n_pages:2 * n_pages]
    knew_ref, vnew_ref, o_ref = refs[2 * n_pages:]
    q = q_ref[0]
    nrow = t_real * A_HEADS
    rows_of = lambda a: jnp.broadcast_to(a[:, None, :], (t_real, A_HEADS, a.shape[-1])).reshape(nrow, a.shape[-1])
    all_pages = lambda pages, new: jnp.concatenate(
        [pr[0].reshape(A_WIDTH, PAGE_SIZE).astype(BF16) for pr in pages] + [new[0]], axis=1)
    s = jnp.where(rows_of(sc_ref[0]) >= rows_of(tau_ref[0][:, 0:1]), _dot(q, all_pages(kp, knew_ref)), MASKED)
    pr = jnp.exp2(s - jnp.max(s, axis=1, keepdims=True))
    l = jnp.sum(pr, axis=1, keepdims=True)
    out = _dot_nt(pr.astype(BF16), all_pages(vp, vnew_ref)) / l
    head_of_row = lax.broadcasted_iota(jnp.int32, (nrow, 1), 0) % A_HEADS
    head_of_lane = lax.broadcasted_iota(jnp.int32, (1, A_WIDTH), 1) // A_HEAD_DIM
    out = jnp.where(head_of_row == head_of_lane, out, 0.0)
    o_ref[0] = jnp.sum(out.reshape(t_real, A_HEADS, A_WIDTH), axis=1).astype(BF16)


def _dsa_sample_attend(page_table, qbd, sc_adj, tau, cache_k_t, cache_v_t, k_new_t, v_new_t, t_real):
    nseq, n_pages = page_table.shape
    ncol = (n_pages + 1) * PAGE_SIZE
    per_seq = lambda a: pl.BlockSpec((1,) + a.shape[1:], lambda b, pt: (b, 0, 0))
    page = lambda p: pl.BlockSpec((1, A_HEADS, A_HEAD_DIM, PAGE_SIZE), lambda b, pt, p=p: (pt[b * n_pages + p], 0, 0, 0))
    pages = lambda: [page(p) for p in range(n_pages)]
    return pl.pallas_call(
        functools.partial(_dsa_sample_attend_kernel, n_pages=n_pages, t_real=t_real),
        grid_spec=pltpu.PrefetchScalarGridSpec(
            num_scalar_prefetch=1,
            grid=(nseq,),
            in_specs=[per_seq(qbd), per_seq(sc_adj), per_seq(tau)] + pages() + pages() + [per_seq(k_new_t), per_seq(v_new_t)],
            out_specs=pl.BlockSpec((1, t_real, A_WIDTH), lambda b, pt: (b, 0, 0)),
        ),
        out_shape=jax.ShapeDtypeStruct((nseq, t_real, A_WIDTH), BF16),
        compiler_params=_cparams(("parallel",)),
        name="dsa_sample_attend",
    )(page_table.reshape(-1), qbd, sc_adj, tau, *([cache_k_t] * n_pages), *([cache_v_t] * n_pages), k_new_t, v_new_t)


PROJ_TM = 512
OUT_TM = 512
MLSTM_CHUNK = 256
DSA_TQ = 512
DSA_TK = 512
SAMPLE_SEQ_BLK = 8
SELECT_ROWS = 128


def _unpad_heads(xpad, n_heads):
    x = xpad.reshape(xpad.shape[:-1] + (n_heads, 2, LANES // 2))
    return x[..., 0, :] + x[..., 1, :]


def kernel(x_prompt, x_sample, cache_k, cache_v, cache_kidx, state_C, state_n, state_m, page_table,
           w_in, b_gate, w_out, ln1_g, ln1_b, w_gate, w_up, w_down, ln2_g, ln2_b):
    bp, sp, _ = x_prompt.shape
    bs, ts, _ = x_sample.shape
    n_pages = page_table.shape[1]
    past = n_pages * PAGE_SIZE
    assert DEPTH == 1 and w_in.shape[0] == 1

    w = _prep_w_in(w_in[0], b_gate[0])
    wo, wg, wu, wd = (a[0].astype(BF16) for a in (w_out, w_gate, w_up, w_down))
    g1, b1, g2, b2 = (v[0].astype(F32)[None, :] for v in (ln1_g, ln1_b, ln2_g, ln2_b))

    mp = bp * sp
    xp = x_prompt.reshape(mp, D_MODEL)
    tabs_p, tabs_pt = _rope_tables(jnp.arange(sp, dtype=jnp.int32))
    (mqkv, so, gates, gatest, qpad, qipad, kt_p, ktb, vt_p, vtb, kit_p, kit2) = _proj(
        xp, w, tabs_p, tabs_pt, PROJ_TM, bp, sp)
    hm_p, caug, m_p = _mlstm_prompt(mqkv.reshape(bp, sp, -1), so.reshape(bp, sp, -1), gates.reshape(bp, sp, -1),
                                    gatest, bp, sp, MLSTM_CHUNK)
    ha_p = _dsa_prompt(qipad, gates, qpad, kit2, ktb, vtb, bp, sp, DSA_TQ, DSA_TK)
    y_p = _block_out(xp, hm_p.reshape(mp, -1), ha_p, wo, g1, b1, wg, wu, wd, g2, b2, OUT_TM)

    ms = bs * ts
    xs = x_sample.reshape(ms, D_MODEL)
    tabs_s, tabs_st = _rope_tables(jnp.tile(past + jnp.arange(ts, dtype=jnp.int32), bs))
    (mqkv_s, so_s, gates_s, _, qpad_s, qipad_s, kt_s, ktb_s, vt_s, vtb_s, kit_s, _) = _proj(
        xs, w, tabs_s, tabs_st, ms, 1, ms)
    pad_rows = lambda a: jnp.pad(a.reshape(bs, ts, -1), ((0, 0), (0, ROWS_PAD - ts), (0, 0))).reshape(bs * ROWS_PAD, -1)
    hm_s, c_s, n_s, m_s = _mlstm_sample(pad_rows(mqkv_s), pad_rows(so_s), pad_rows(gates_s),
                                        state_C[0].astype(F32), state_n[0].astype(F32), state_m[0].astype(F32),
                                        SAMPLE_SEQ_BLK, ts)
    hm_s = hm_s.reshape(bs, ROWS_PAD, -1)[:, :ts].reshape(ms, -1)

    new_page = lambda a: jnp.pad(a.reshape(a.shape[0], bs, ts).transpose(1, 0, 2), ((0, 0), (0, 0), (0, PAGE_SIZE - ts)))
    qi_s = _unpad_heads(qipad_s, IDX_HEADS).reshape(bs, ts * IDX_HEADS, IDX_DIM)
    wi_s = gates_s[:, L_WI:L_WI + IDX_HEADS].reshape(bs, ts * IDX_HEADS, 1)
    sc_s = _dsa_sample_scores(page_table, qi_s, wi_s, cache_kidx[0].transpose(0, 2, 1), new_page(kit_s[0]).astype(BF16), ts)
    ncol = sc_s.shape[-1]
    topk_s = min(TOPK_MAX, (past + ts) // 4)
    sc_adj, tau_s = _dsa_sample_select(sc_s.reshape(ms, ncol), topk_s, SELECT_ROWS)
    q_s = _unpad_heads(qpad_s, A_HEADS).reshape(bs, ts, A_HEADS, A_HEAD_DIM)
    eye = jnp.eye(A_HEADS, dtype=q_s.dtype)
    qbd = (q_s[:, :, :, None, :] * eye[None, None, :, :, None]).reshape(bs, ts * A_HEADS, A_WIDTH)
    ha_s = _dsa_sample_attend(page_table, qbd, sc_adj.reshape(bs, ts, ncol), tau_s.reshape(bs, ts, LANES),
                              cache_k[0].transpose(0, 2, 3, 1), cache_v[0].transpose(0, 2, 3, 1),
                              new_page(ktb_s[0]), new_page(vtb_s[0]), ts)
    y_s = _block_out(xs, hm_s, ha_s.reshape(ms, -1), wo, g1, b1, wg, wu, wd, g2, b2, min(OUT_TM, ms))

    heads = lambda a, b, t: a.reshape(1, b, A_HEADS, A_HEAD_DIM, t).transpose(0, 1, 4, 2, 3)
    heads_s = lambda a: a[0].T.reshape(1, bs, ts, A_HEADS, A_HEAD_DIM)
    return (y_p.reshape(bp, sp, D_MODEL), y_s.reshape(bs, ts, D_MODEL),
            heads(kt_p, bp, sp), heads(vt_p, bp, sp), kit_p.transpose(0, 2, 1)[None],
            caug[None, :, :, :, 0:M_HEAD_DIM], caug[None, :, :, :, M_HEAD_DIM], m_p[:, 0].reshape(1, bp, M_HEADS),
            heads_s(kt_s), heads_s(vt_s), kit_s[0].T.reshape(1, bs, ts, IDX_DIM),
            c_s[None], n_s[None], m_s[None])
```
